```python
import jax, jax.numpy as jnp
from jax import lax
import numpy as np

D_MODEL = 2048
BATCH = 4
SEQ = 8192
DEPTH = 2
DEC_BATCH = 8
DEC_SEQ = 32
PAST_LEN = 1024

CHUNK = 64
HEAD_DIM = 128
N_HEADS = D_MODEL // HEAD_DIM
H_A = N_HEADS // 4
H_B = (N_HEADS - H_A) // 2
H_C = N_HEADS - H_A - H_B
A_LEFT_CHUNKS = 8
A_WIN = A_LEFT_CHUNKS * CHUNK
A_BAND = (A_LEFT_CHUNKS + 1) * CHUNK
REL_CLIP = 128
N_REL = 2 * REL_CLIP + 1
D_FF = -(-8 * D_MODEL // (3 * 256)) * 256
N_IN = 3 * D_MODEL + H_B
Q_BLOCK = 128
EPS = 1e-6
ATTN_SCALE = HEAD_DIM ** -0.5
FORGET_BIAS = 4.0

kernel_name = 'hybrid_streaming_encoder_step'

F32 = jnp.float32


def rmsnorm(x, g):
    xf = x.astype(F32)
    y = xf * lax.rsqrt(jnp.mean(xf * xf, axis=-1, keepdims=True) + EPS)
    return (y * g.astype(F32)).astype(x.dtype)


def split_heads(t):
    return t[:, :, :H_A], t[:, :, H_A:H_A + H_B], t[:, :, H_A + H_B:]


def rel_bias_scores(rel_table, rel):
    idx = jnp.clip(rel, -REL_CLIP, REL_CLIP) + REL_CLIP
    return jnp.transpose(rel_table[idx], (2, 0, 1)).astype(F32)


def to_blocks(t):
    b, t_len = t.shape[:2]
    return jnp.swapaxes(t.reshape((b, t_len // Q_BLOCK, Q_BLOCK) + t.shape[2:]), 0, 1)


def from_blocks(o):
    o = jnp.swapaxes(o, 0, 1)
    return o.reshape((o.shape[0], o.shape[1] * o.shape[2]) + o.shape[3:])


def band_attn_prompt(q, k, v, rel_table):
    b, t_len, h, dh = q.shape
    nc = t_len // CHUNK
    left = A_LEFT_CHUNKS * CHUNK

    def band(t):
        tp = jnp.pad(t, ((0, 0), (left, 0), (0, 0), (0, 0))).reshape(b, nc + A_LEFT_CHUNKS, CHUNK, h, dh)
        return jnp.concatenate([tp[:, j:j + nc] for j in range(A_LEFT_CHUNKS + 1)], axis=2)

    kb, vb = band(k), band(v)
    qc = q.reshape(b, nc, CHUNK, h, dh)
    s = jnp.einsum('bnqhd,bnkhd->bnhqk', qc, kb).astype(F32) * ATTN_SCALE
    slot = jnp.arange(A_BAND)
    rel = jnp.arange(CHUNK)[:, None] + left - slot[None, :]
    s = s + rel_bias_scores(rel_table, rel)
    kpos = (jnp.arange(nc)[:, None] - A_LEFT_CHUNKS) * CHUNK + slot[None, :]
    s = jnp.where((kpos >= 0)[None, :, None, None, :], s, -jnp.inf)
    p = jax.nn.softmax(s, axis=-1).astype(v.dtype)
    return jnp.einsum('bnhqk,bnkhd->bnqhd', p, vb).reshape(b, t_len, h, dh)


def band_attn_sample(q, k, v, ck, cv, rel_table):
    w = ck.shape[1]
    tn = q.shape[1]
    kk = jnp.concatenate([ck.astype(k.dtype), k], axis=1)
    vv = jnp.concatenate([cv.astype(v.dtype), v], axis=1)
    s = jnp.einsum('bqhd,bkhd->bhqk', q, kk).astype(F32) * ATTN_SCALE
    rel = jnp.arange(tn)[:, None] - (jnp.arange(w + tn) - w)[None, :]
    s = s + rel_bias_scores(rel_table, rel)
    p = jax.nn.softmax(s, axis=-1).astype(v.dtype)
    return jnp.einsum('bhqk,bkhd->bqhd', p, vv)


def fox_core(q, k, v, fq, fk, qpos, kpos):
    s = jnp.einsum('bqhd,bkhd->bhqk', q, k).astype(F32) * ATTN_SCALE
    s = s + jnp.transpose(fq, (0, 2, 1))[..., :, None] - jnp.transpose(fk, (0, 2, 1))[..., None, :]
    s = jnp.where(kpos[None, :] <= qpos[:, None], s, -jnp.inf)
    p = jax.nn.softmax(s, axis=-1).astype(v.dtype)
    return jnp.einsum('bhqk,bkhd->bqhd', p, v)


def fox_prompt(q, k, v, logf):
    t_len = q.shape[1]
    f_cum = jnp.cumsum(logf, axis=1)
    kpos = jnp.arange(t_len)

    def block(args):
        qi, fi, q0 = args
        return fox_core(qi, k, v, fi, f_cum, q0 + jnp.arange(Q_BLOCK), kpos)

    o = lax.map(block, (to_blocks(q), to_blocks(f_cum), jnp.arange(0, t_len, Q_BLOCK)))
    return from_blocks(o)


def fox_sample(q, k, v, logf, ck, cv, clogf):
    p_len = ck.shape[1]
    tn = q.shape[1]
    kk = jnp.concatenate([ck.astype(k.dtype), k], axis=1)
    vv = jnp.concatenate([cv.astype(v.dtype), v], axis=1)
    f_cum = jnp.cumsum(jnp.concatenate([clogf.astype(F32), logf], axis=1), axis=1)
    return fox_core(q, kk, vv, f_cum[:, p_len:], f_cum, p_len + jnp.arange(tn), jnp.arange(p_len + tn))


def stick_core(q, k, v, qpos, kpos):
    z = jnp.einsum('bqhd,bkhd->bhqk', q, k).astype(F32) * ATTN_SCALE
    mask = kpos[None, :] < qpos[:, None]
    log_stay = jnp.where(mask, jax.nn.log_sigmoid(-z), 0.0)
    after = lax.cumsum(log_stay, axis=3, reverse=True) - log_stay
    weight = jnp.where(mask, jnp.exp(jax.nn.log_sigmoid(z) + after), 0.0)
    return jnp.einsum('bhqk,bkhd->bqhd', weight.astype(v.dtype), v)


def stick_prompt(q, k, v):
    t_len = q.shape[1]
    kpos = jnp.arange(t_len)

    def block(args):
        qi, q0 = args
        return stick_core(qi, k, v, q0 + jnp.arange(Q_BLOCK), kpos)

    o = lax.map(block, (to_blocks(q), jnp.arange(0, t_len, Q_BLOCK)))
    return from_blocks(o)


def stick_sample(q, k, v, ck, cv):
    p_len = ck.shape[1]
    tn = q.shape[1]
    kk = jnp.concatenate([ck.astype(k.dtype), k], axis=1)
    vv = jnp.concatenate([cv.astype(v.dtype), v], axis=1)
    return stick_core(q, kk, vv, p_len + jnp.arange(tn), jnp.arange(p_len + tn))


def mix_prompt(q, k, v, logf, rel_table):
    t_len = q.shape[1]
    qa, qb, qc = split_heads(q)
    ka, kb, kc = split_heads(k)
    va, vb, vc = split_heads(v)
    o = jnp.concatenate([band_attn_prompt(qa, ka, va, rel_table),
                         fox_prompt(qb, kb, vb, logf),
                         stick_prompt(qc, kc, vc)], axis=2)
    keep = min(A_WIN, t_len)
    return o, (ka[:, t_len - keep:], va[:, t_len - keep:], kb, vb, logf, kc, vc)


def mix_sample(q, k, v, logf, cak, cav, cbk, cbv, cblogf, cck, ccv, rel_table):
    qa, qb, qc = split_heads(q)
    ka, kb, kc = split_heads(k)
    va, vb, vc = split_heads(v)
    o = jnp.concatenate([band_attn_sample(qa, ka, va, cak, cav, rel_table),
                         fox_sample(qb, kb, vb, logf, cbk, cbv, cblogf),
                         stick_sample(qc, kc, vc, cck, ccv)], axis=2)
    return o, (ka, va, kb, vb, logf, kc, vc)


def layer_forward(x, c, mixer, mixer_args, w_ada, b_ada, g_attn, g_ffn, w_in, b_f, w_o, w_gate, w_up, w_down):
    b, t_len, _ = x.shape
    mod = jax.nn.silu(c) @ w_ada + b_ada
    sh1, sc1, gt1, sh2, sc2, gt2 = [m[:, None, :] for m in jnp.split(mod, 6, axis=-1)]
    h = rmsnorm(x, g_attn) * (1 + sc1) + sh1
    proj = h @ w_in
    q = proj[..., :D_MODEL].reshape(b, t_len, N_HEADS, HEAD_DIM)
    k = proj[..., D_MODEL:2 * D_MODEL].reshape(b, t_len, N_HEADS, HEAD_DIM)
    v = proj[..., 2 * D_MODEL:3 * D_MODEL].reshape(b, t_len, N_HEADS, HEAD_DIM)
    logf = jax.nn.log_sigmoid((proj[..., 3 * D_MODEL:] + b_f).astype(F32))
    o, state = mixer(q, k, v, logf, *mixer_args)
    x = x + gt1 * (o.reshape(b, t_len, D_MODEL) @ w_o)
    h = rmsnorm(x, g_ffn) * (1 + sc2) + sh2
    x = x + gt2 * ((jax.nn.silu(h @ w_gate) * (h @ w_up)) @ w_down)
    return x, state


def setup_inputs(seed: int = 0) -> dict:
    key = jax.random.key(seed)
    ks = jax.random.split(key, 26)

    def nrm(k, shape, scale):
        return jax.random.normal(k, shape, F32) * scale

    a_cache = min(A_WIN, PAST_LEN)
    return {
        'x_prompt': nrm(ks[0], (BATCH, SEQ, D_MODEL), 1.0),
        'x_sample': nrm(ks[1], (DEC_BATCH, DEC_SEQ, D_MODEL), 1.0),
        'cache_a_k': nrm(ks[2], (DEPTH, DEC_BATCH, a_cache, H_A, HEAD_DIM), 1.0),
        'cache_a_v': nrm(ks[3], (DEPTH, DEC_BATCH, a_cache, H_A, HEAD_DIM), 1.0),
        'cache_b_k': nrm(ks[4], (DEPTH, DEC_BATCH, PAST_LEN, H_B, HEAD_DIM), 1.0),
        'cache_b_v': nrm(ks[5], (DEPTH, DEC_BATCH, PAST_LEN, H_B, HEAD_DIM), 1.0),
        'cache_b_logf': jax.nn.log_sigmoid(FORGET_BIAS + nrm(ks[6], (DEPTH, DEC_BATCH, PAST_LEN, H_B), 1.0)),
        'cache_c_k': nrm(ks[7], (DEPTH, DEC_BATCH, PAST_LEN, H_C, HEAD_DIM), 1.0),
        'cache_c_v': nrm(ks[8], (DEPTH, DEC_BATCH, PAST_LEN, H_C, HEAD_DIM), 1.0),
        'c_prompt': nrm(ks[9], (BATCH, D_MODEL), 1.0),
        'c_sample': nrm(ks[10], (DEC_BATCH, D_MODEL), 1.0),
        'w_ada': nrm(ks[11], (DEPTH, D_MODEL, 6 * D_MODEL), 0.5 * D_MODEL ** -0.5),
        'b_ada': nrm(ks[12], (DEPTH, 6 * D_MODEL), 0.01),
        'g_attn': 1.0 + nrm(ks[13], (DEPTH, D_MODEL), 0.1),
        'g_ffn': 1.0 + nrm(ks[14], (DEPTH, D_MODEL), 0.1),
        'w_in': nrm(ks[15], (DEPTH, D_MODEL, N_IN), D_MODEL ** -0.5),
        'b_f': FORGET_BIAS + nrm(ks[16], (DEPTH, H_B), 0.1),
        'rel_bias': nrm(ks[17], (DEPTH, N_REL, H_A), 0.5),
        'w_o': nrm(ks[18], (DEPTH, D_MODEL, D_MODEL), D_MODEL ** -0.5),
        'w_gate': nrm(ks[19], (DEPTH, D_MODEL, D_FF), D_MODEL ** -0.5),
        'w_up': nrm(ks[20], (DEPTH, D_MODEL, D_FF), D_MODEL ** -0.5),
        'w_down': nrm(ks[21], (DEPTH, D_FF, D_MODEL), D_FF ** -0.5),
        'g_final': 1.0 + nrm(ks[22], (D_MODEL,), 0.1),
    }


def reference(x_prompt, x_sample, cache_a_k, cache_a_v, cache_b_k, cache_b_v, cache_b_logf, cache_c_k, cache_c_v,
              c_prompt, c_sample, w_ada, b_ada, g_attn, g_ffn, w_in, b_f, rel_bias, w_o, w_gate, w_up, w_down, g_final):
    xp, xs = x_prompt, x_sample
    states_p, states_s = [], []
    for l in range(DEPTH):
        params = (w_ada[l], b_ada[l], g_attn[l], g_ffn[l], w_in[l], b_f[l], w_o[l], w_gate[l], w_up[l], w_down[l])
        xp, st_p = layer_forward(xp, c_prompt, mix_prompt, (rel_bias[l],), *params)
        xs, st_s = layer_forward(xs, c_sample, mix_sample,
                                 (cache_a_k[l], cache_a_v[l], cache_b_k[l], cache_b_v[l], cache_b_logf[l],
                                  cache_c_k[l], cache_c_v[l], rel_bias[l]), *params)
        states_p.append(st_p)
        states_s.append(st_s)
    a_k_p, a_v_p, b_k_p, b_v_p, b_logf_p, c_k_p, c_v_p = [jnp.stack(t) for t in zip(*states_p)]
    a_k_s, a_v_s, b_k_s, b_v_s, b_logf_s, c_k_s, c_v_s = [jnp.stack(t) for t in zip(*states_s)]
    y_prompt = rmsnorm(xp, g_final)
    y_sample = rmsnorm(xs, g_final)
    return (y_prompt, y_sample, a_k_p, a_v_p, b_k_p, b_v_p, b_logf_p, c_k_p, c_v_p,
            a_k_s, a_v_s, b_k_s, b_v_s, b_logf_s, c_k_s, c_v_s)
```

```python
import functools

import jax
import jax.numpy as jnp
from jax import lax
from jax.experimental import pallas as pl
from jax.experimental.pallas import tpu as pltpu

F32 = jnp.float32
BF16 = jnp.bfloat16

D_MODEL = 2048
HEAD_DIM = 128
N_HEADS = D_MODEL // HEAD_DIM
H_A = N_HEADS // 4
H_B = (N_HEADS - H_A) // 2
H_C = N_HEADS - H_A - H_B
CHUNK = 64
A_LEFT = 8 * CHUNK
REL_CLIP = 128
N_REL = 2 * REL_CLIP + 1
EPS = 1e-6
ATTN_SCALE = HEAD_DIM ** -0.5

V7X_VMEM_BYTES = 64 * 1024 * 1024
LANES = 128
HEAD_PAIR = 2 * HEAD_DIM
A_BAND = 640


def _cparams(sem, vmem_mb):
    return pltpu.CompilerParams(dimension_semantics=sem,
                                vmem_limit_bytes=min(vmem_mb * 1024 * 1024, V7X_VMEM_BYTES - (4 << 20)))


def _dot(a, b):
    return jnp.dot(a, b, preferred_element_type=F32)


def _dot_nt(a, b):
    return lax.dot_general(a, b, (((1,), (1,)), ((), ())), preferred_element_type=F32)


def _sigmoid(x):
    return 1.0 / (1.0 + jnp.exp(-x))


def _log_sigmoid(x):
    return jnp.minimum(x, 0.0) - jnp.log1p(jnp.exp(-jnp.abs(x)))


def _modulated_norm(x, g, sc, sh):
    ms = jnp.mean(x * x, axis=-1, keepdims=True)
    y = x * lax.rsqrt(ms + EPS) * g
    return y * (1.0 + sc) + sh


def _norm_rows(x_ref, g_ref, sc_ref, sh_ref, h_ref, rows_chunk=64):
    tm = x_ref.shape[0]
    g = g_ref[...]
    per_row = sc_ref.shape[0] != 1

    def body(r, c):
        r0 = pl.multiple_of(r * rows_chunk, rows_chunk)
        sc = sc_ref[pl.ds(r0, rows_chunk), :] if per_row else sc_ref[...]
        sh = sh_ref[pl.ds(r0, rows_chunk), :] if per_row else sh_ref[...]
        h = _modulated_norm(x_ref[pl.ds(r0, rows_chunk), :], g, sc, sh)
        h_ref[pl.ds(r0, rows_chunk), :] = h.astype(h_ref.dtype)
        return c

    lax.fori_loop(0, tm // rows_chunk, body, 0)


def _ada_kernel(c_ref, w_ref, b_ref, o_ref):
    c = c_ref[...]
    a = (c * _sigmoid(c)).astype(BF16)
    o_ref[...] = _dot(a, w_ref[...].astype(BF16)) + b_ref[...]


def _ada(c_all, w_ada, b_ada):
    depth, _, n = w_ada.shape
    rows = c_all.shape[0]
    tn = 1024
    return pl.pallas_call(
        _ada_kernel,
        grid=(depth, n // tn),
        in_specs=[pl.BlockSpec((rows, D_MODEL), lambda l, j: (0, 0)),
                  pl.BlockSpec((None, D_MODEL, tn), lambda l, j: (l, 0, j)),
                  pl.BlockSpec((None, 1, tn), lambda l, j: (l, 0, j))],
        out_specs=pl.BlockSpec((None, rows, tn), lambda l, j: (l, 0, j)),
        out_shape=jax.ShapeDtypeStruct((depth, rows, n), F32),
        name="ada_mod",
        compiler_params=_cparams(("arbitrary", "arbitrary"), 40),
    )(c_all, w_ada, b_ada.reshape(depth, 1, n))


def _relbias_kernel(tab_ref, o_ref, *, cq, band):
    l = pl.program_id(0)
    h = pl.program_id(1)
    qi = lax.broadcasted_iota(jnp.int32, (cq, band), 0)
    s = lax.broadcasted_iota(jnp.int32, (cq, band), 1)
    idx = jnp.clip(qi + (band - cq) - s, -REL_CLIP, REL_CLIP) + REL_CLIP
    base = l * (N_REL * H_A) + h

    def body(r, acc):
        return jnp.where(idx == r, tab_ref[base + r * H_A], acc)

    o_ref[...] = lax.fori_loop(0, N_REL, body, jnp.zeros((cq, band), F32))


def _relbias(rel_bias, cq, band):
    depth = rel_bias.shape[0]
    return pl.pallas_call(
        functools.partial(_relbias_kernel, cq=cq, band=band),
        grid=(depth, H_A),
        in_specs=[pl.BlockSpec(memory_space=pltpu.SMEM)],
        out_specs=pl.BlockSpec((None, None, cq, band), lambda l, h: (l, h, 0, 0)),
        out_shape=jax.ShapeDtypeStruct((depth, H_A, cq, band), F32),
        name="rel_bias",
    )(rel_bias.reshape(-1))


def _inproj_kernel(x_ref, sc_ref, sh_ref, g_ref, wq_ref, wk_ref, wv_ref, wf_ref, bf_ref,
                   q_ref, ka_ref, kb_ref, kc_ref, va_ref, vb_ref, vc_ref, lf_ref, h_scr):
    j = pl.program_id(1)

    @pl.when(j == 0)
    def _():
        _norm_rows(x_ref, g_ref, sc_ref, sh_ref, h_scr)
        lf_ref[...] = _log_sigmoid(_dot(h_scr[...], wf_ref[...]) + bf_ref[...])

    h = h_scr[...]
    q_ref[...] = _dot(h, wq_ref[...]).astype(BF16)
    kt = _dot(h, wk_ref[...])
    vt = _dot(h, wv_ref[...])
    pa = H_A // 2
    pb = (H_A + H_B) // 2

    @pl.when(j < pa)
    def _():
        ka_ref[...] = kt
        va_ref[...] = vt

    @pl.when((j >= pa) & (j < pb))
    def _():
        kb_ref[...] = kt
        vb_ref[...] = vt

    @pl.when(j >= pb)
    def _():
        kc_ref[...] = kt
        vc_ref[...] = vt


def _mod_spec(mod, tm, tn, tiles_per_batch):
    r = mod.shape[1]
    col = (lambda j: 0) if tn == mod.shape[2] else (lambda j: j)
    if r == 1:
        return pl.BlockSpec((None, 1, tn), lambda i, j: (i // tiles_per_batch, 0, col(j)))
    return pl.BlockSpec((None, tm, tn), lambda i, j: (0, i, col(j)))


def _inproj(x2, sc, sh, g, w_qkv, w_f, b_f, tm, tiles_per_batch):
    m = x2.shape[0]
    tn = HEAD_PAIR
    npair = N_HEADS // 2
    pa, pb = H_A // 2, (H_A + H_B) // 2

    def clamp(lo, n):
        return lambda i, j: (i, jnp.clip(j - lo, 0, n - 1))

    def full_mod(a):
        return _mod_spec(a, tm, D_MODEL, tiles_per_batch)

    out_shape = [jax.ShapeDtypeStruct((m, D_MODEL), BF16)]
    out_specs = [pl.BlockSpec((tm, tn), lambda i, j: (i, j))]
    for _ in range(2):
        for lo, n in ((0, pa), (pa, pb - pa), (pb, npair - pb)):
            out_shape.append(jax.ShapeDtypeStruct((m, n * tn), F32))
            out_specs.append(pl.BlockSpec((tm, tn), clamp(lo, n)))
    out_shape.append(jax.ShapeDtypeStruct((m, LANES), F32))
    out_specs.append(pl.BlockSpec((tm, LANES), lambda i, j: (i, 0)))
    outs = pl.pallas_call(
        _inproj_kernel,
        grid=(m // tm, npair),
        in_specs=[pl.BlockSpec((tm, D_MODEL), lambda i, j: (i, 0)),
                  full_mod(sc), full_mod(sh),
                  pl.BlockSpec((1, D_MODEL), lambda i, j: (0, 0)),
                  pl.BlockSpec((D_MODEL, tn), lambda i, j: (0, j)),
                  pl.BlockSpec((D_MODEL, tn), lambda i, j: (0, npair + j)),
                  pl.BlockSpec((D_MODEL, tn), lambda i, j: (0, 2 * npair + j)),
                  pl.BlockSpec((D_MODEL, LANES), lambda i, j: (0, 0)),
                  pl.BlockSpec((1, LANES), lambda i, j: (0, 0))],
        out_specs=out_specs,
        out_shape=out_shape,
        scratch_shapes=[pltpu.VMEM((tm, D_MODEL), BF16)],
        name="in_proj",
        compiler_params=_cparams(("arbitrary", "arbitrary"), 52),
    )(x2, sc, sh, g, w_qkv, w_qkv, w_qkv, w_f, b_f)
    q, ka, kb, kc, va, vb, vc, lf = outs
    return q, ka, kb, kc, va, vb, vc, lf


def _split3(x):
    hi = x.astype(BF16)
    r1 = x - hi.astype(F32)
    mid = r1.astype(BF16)
    lo = (r1 - mid.astype(F32)).astype(BF16)
    return hi, mid, lo


def _dot_f32_by_01(x, ones01):
    hi, mid, lo = _split3(x)
    return _dot(hi, ones01) + _dot(mid, ones01) + _dot(lo, ones01)


def _cumsum_kernel(x_ref, o_ref):
    x = x_ref[...]
    rows = x.shape[0]
    i0 = lax.broadcasted_iota(jnp.int32, (LANES, LANES), 0)
    i1 = lax.broadcasted_iota(jnp.int32, (LANES, LANES), 1)
    upper = (i0 <= i1).astype(BF16)
    c = _dot_f32_by_01(x, upper)
    tot = jnp.broadcast_to(c[:, LANES - 1:LANES], (rows, LANES))
    r0 = lax.broadcasted_iota(jnp.int32, (rows, rows), 0)
    r1 = lax.broadcasted_iota(jnp.int32, (rows, rows), 1)
    lower = (r1 < r0).astype(BF16)
    hi, mid, lo = _split3(tot)
    off = _dot(lower, hi) + _dot(lower, mid) + _dot(lower, lo)
    o_ref[...] = c + off


def _cumsum_time(logf_hm):
    b, h, t = logf_hm.shape
    rows = t // LANES
    out = pl.pallas_call(
        _cumsum_kernel,
        grid=(b * h,),
        in_specs=[pl.BlockSpec((None, rows, LANES), lambda i: (i, 0, 0))],
        out_specs=pl.BlockSpec((None, rows, LANES), lambda i: (i, 0, 0)),
        out_shape=jax.ShapeDtypeStruct((b * h, rows, LANES), F32),
        name="forget_cumsum",
    )(logf_hm.reshape(b * h, rows, LANES))
    return out.reshape(b, h, t)


def _band_kernel(q_ref, k_ref, v_ref, bias_ref, o_ref, kp_scr, vp_scr, *, cq, npad, nmask, nchunks):
    tk = k_ref.shape[0]
    kp_scr[0:npad, :] = jnp.zeros((npad, HEAD_DIM), BF16)
    vp_scr[0:npad, :] = jnp.zeros((npad, HEAD_DIM), BF16)
    _copy_cast(k_ref, kp_scr, tk, dst_off=npad)
    _copy_cast(v_ref, vp_scr, tk, dst_off=npad)
    bias = bias_ref[...]
    slot = lax.broadcasted_iota(jnp.int32, (cq, A_BAND), 1)

    def chunk(n, carry):
        r0 = pl.multiple_of(n * cq, cq)
        q = q_ref[pl.ds(r0, cq), :]
        k = kp_scr[pl.ds(r0, A_BAND), :]
        v = vp_scr[pl.ds(r0, A_BAND), :]
        s = _dot_nt(q, k) * ATTN_SCALE + bias
        first_valid = jnp.maximum(nmask, npad - n * cq)
        s = jnp.where(slot >= first_valid, s, -jnp.inf)
        m = jnp.max(s, axis=-1, keepdims=True)
        p = jnp.exp(s - m)
        l = jnp.sum(p, axis=-1, keepdims=True)
        o = _dot(p.astype(BF16), v) / l
        o_ref[pl.ds(r0, cq), :] = o.astype(o_ref.dtype)
        return carry

    lax.fori_loop(0, nchunks, chunk, 0)


def _band_attn(q, k, v, bias, cq, npad, nmask):
    b, t, _ = q.shape
    tk = k.shape[1]
    kern = functools.partial(_band_kernel, cq=cq, npad=npad, nmask=nmask, nchunks=t // cq)
    return pl.pallas_call(
        kern,
        grid=(b, H_A),
        in_specs=[pl.BlockSpec((None, t, HEAD_DIM), lambda i, h: (i, 0, h)),
                  pl.BlockSpec((None, tk, HEAD_DIM), lambda i, h: (i, 0, h)),
                  pl.BlockSpec((None, tk, HEAD_DIM), lambda i, h: (i, 0, h)),
                  pl.BlockSpec((None, cq, A_BAND), lambda i, h: (h, 0, 0))],
        out_specs=pl.BlockSpec((None, t, HEAD_DIM), lambda i, h: (i, 0, h)),
        out_shape=jax.ShapeDtypeStruct((b, t, H_A * HEAD_DIM), BF16),
        scratch_shapes=[pltpu.VMEM((npad + tk, HEAD_DIM), BF16),
                        pltpu.VMEM((npad + tk, HEAD_DIM), BF16)],
        name="band_attn",
        compiler_params=_cparams(("arbitrary", "arbitrary"), 48),
    )(q, k, v, bias)


def _copy_cast(src_ref, dst_ref, rows, dst_off=0, step=256):
    n = rows // step

    def body(i, c):
        r = pl.multiple_of(i * step, step)
        d = pl.multiple_of(dst_off + i * step, 16)
        dst_ref[pl.ds(d, step), :] = src_ref[pl.ds(r, step), :].astype(dst_ref.dtype)
        return c

    lax.fori_loop(0, n, body, 0)
    if rows % step:
        dst_ref[dst_off + n * step:dst_off + rows, :] = src_ref[n * step:rows, :].astype(dst_ref.dtype)


def _row_to_col(row):
    n = row.shape[1]
    cols = []
    for c in range(n // LANES):
        blk = jnp.broadcast_to(row[:, c * LANES:(c + 1) * LANES], (LANES, LANES))
        cols.append(blk.T[:, 0:1])
    return cols[0] if len(cols) == 1 else jnp.concatenate(cols, axis=0)


def _fox_kernel(q_ref, k_ref, v_ref, f_ref, o_ref, k_scr, v_scr, acc_scr, m_scr, l_scr,
                *, tq, tkd, tk, nq, qoff):
    tkv = k_ref.shape[0]
    _copy_cast(k_ref, k_scr, tkv)
    _copy_cast(v_ref, v_scr, tkv)
    row = lax.broadcasted_iota(jnp.int32, (tq, tkd), 0)
    col = lax.broadcasted_iota(jnp.int32, (tq, tkd), 1)
    causal = col <= row
    tqp = max(tq, LANES)

    def block(q, fq, kb, width, mask):
        k0 = pl.multiple_of(kb * tk, tk)
        k = k_scr[pl.ds(k0, width), :]
        v = v_scr[pl.ds(k0, width), :]
        fk = f_ref[kb][:, 0:width]
        s = _dot_nt(q, k) * ATTN_SCALE + fq - fk
        if mask is not None:
            s = jnp.where(mask, s, -jnp.inf)
        m_old = m_scr[...]
        m_new = jnp.maximum(m_old, jnp.max(s, axis=-1, keepdims=True))
        alpha = jnp.exp(m_old - m_new)
        p = jnp.exp(s - m_new)
        l_scr[...] = alpha * l_scr[...] + jnp.sum(p, axis=-1, keepdims=True)
        acc_scr[...] = alpha * acc_scr[...] + _dot(p.astype(BF16), v)
        m_scr[...] = m_new

    def qblock(i, carry):
        q0 = pl.multiple_of(i * tq, tq)
        pb = (qoff + i * tq) // tk
        q = q_ref[pl.ds(q0, tq), :]
        fq = _row_to_col(f_ref[pb][:, 0:tqp])[0:tq]
        m_scr[...] = jnp.full((tq, 1), -jnp.inf, F32)
        l_scr[...] = jnp.zeros((tq, 1), F32)
        acc_scr[...] = jnp.zeros((tq, HEAD_DIM), F32)
        block(q, fq, pb, tkd, causal)

        def full(jj, c):
            block(q, fq, jj, tk, None)
            return c

        lax.fori_loop(0, pb, full, 0)
        o_ref[pl.ds(q0, tq), :] = (acc_scr[...] / l_scr[...]).astype(o_ref.dtype)
        return carry

    lax.fori_loop(0, nq, qblock, 0)


def _stick_kernel(q_ref, k_ref, v_ref, o_ref, k_scr, v_scr, acc_scr, a_scr, msd_scr, msf_scr,
                  *, tq, tkd, tk, nq, qoff):
    tkv = k_ref.shape[0]
    _copy_cast(k_ref, k_scr, tkv)
    _copy_cast(v_ref, v_scr, tkv)
    for ref, w in ((msd_scr, tkd), (msf_scr, tk)):
        j0 = lax.broadcasted_iota(jnp.int32, (w, w), 0)
        j1 = lax.broadcasted_iota(jnp.int32, (w, w), 1)
        ref[...] = (j0 > j1).astype(BF16)
    row = lax.broadcasted_iota(jnp.int32, (tq, tkd), 0)
    col = lax.broadcasted_iota(jnp.int32, (tq, tkd), 1)
    causal = col < row

    def block(q, k0, width, mask, ms_ref):
        k = k_scr[pl.ds(k0, width), :]
        v = v_scr[pl.ds(k0, width), :]
        z = _dot_nt(q, k) * ATTN_SCALE
        lsg = _log_sigmoid(z)
        stay = lsg - z
        if mask is not None:
            stay = jnp.where(mask, stay, 0.0)
        hi = stay.astype(BF16)
        lo = (stay - hi.astype(F32)).astype(BF16)
        ms = ms_ref[...]
        within = _dot(hi, ms) + _dot(lo, ms)
        a_old = a_scr[...]
        w = jnp.exp(lsg + within + a_old)
        if mask is not None:
            w = jnp.where(mask, w, 0.0)
        acc_scr[...] += _dot(w.astype(BF16), v)
        a_scr[...] = a_old + within[:, 0:1] + stay[:, 0:1]

    def qblock(i, carry):
        q0 = pl.multiple_of(i * tq, tq)
        p0 = pl.multiple_of(qoff + i * tq, tq)
        q = q_ref[pl.ds(q0, tq), :]
        a_scr[...] = jnp.zeros((tq, 1), F32)
        acc_scr[...] = jnp.zeros((tq, HEAD_DIM), F32)
        block(q, p0, tkd, causal, msd_scr)
        nfull = p0 // tk

        def full(jj, c):
            block(q, pl.multiple_of((nfull - 1 - jj) * tk, tk), tk, None, msf_scr)
            return c

        lax.fori_loop(0, nfull, full, 0)
        o_ref[pl.ds(q0, tq), :] = acc_scr[...].astype(o_ref.dtype)
        return carry

    lax.fori_loop(0, nq, qblock, 0)


def _causal_attn(kind, q, k, v, f, head0, tq, tkd, tk, qoff):
    b, t, _ = q.shape
    tkv = k.shape[1]
    nh = k.shape[2] // HEAD_DIM
    assert qoff % tk == 0 and (tq % tk == 0 or t == tq) and tkd <= tk and tq <= tkd
    common = dict(tq=tq, tkd=tkd, tk=tk, nq=t // tq, qoff=qoff)
    kv_spec = pl.BlockSpec((None, tkv, HEAD_DIM), lambda i, h: (i, 0, h))
    in_specs = [pl.BlockSpec((None, t, HEAD_DIM), lambda i, h: (i, 0, head0 + h)), kv_spec, kv_spec]
    scratch = [pltpu.VMEM((tkv, HEAD_DIM), BF16), pltpu.VMEM((tkv, HEAD_DIM), BF16),
               pltpu.VMEM((tq, HEAD_DIM), F32)]
    if kind == "fox":
        kern = functools.partial(_fox_kernel, **common)
        nblk = f.shape[2] // tk
        f = f.reshape(b, nh, nblk, 1, tk)
        in_specs.append(pl.BlockSpec((None, None, nblk, 1, tk), lambda i, h: (i, h, 0, 0, 0)))
        scratch += [pltpu.VMEM((tq, 1), F32), pltpu.VMEM((tq, 1), F32)]
        args = (q, k, v, f)
    else:
        kern = functools.partial(_stick_kernel, **common)
        scratch += [pltpu.VMEM((tq, 1), F32), pltpu.VMEM((tkd, tkd), BF16), pltpu.VMEM((tk, tk), BF16)]
        args = (q, k, v)
    return pl.pallas_call(
        kern,
        grid=(b, nh),
        in_specs=in_specs,
        out_specs=pl.BlockSpec((None, t, HEAD_DIM), lambda i, h: (i, 0, h)),
        out_shape=jax.ShapeDtypeStruct((b, t, nh * HEAD_DIM), BF16),
        scratch_shapes=scratch,
        name=kind + "_attn",
        compiler_params=_cparams(("arbitrary", "arbitrary"), 52),
    )(*args)


def _outproj_kernel(oa_ref, ob_ref, oc_ref, wa_ref, wb_ref, wc_ref, x_ref, gt_ref, y_ref):
    acc = _dot(oa_ref[...], wa_ref[...]) + _dot(ob_ref[...], wb_ref[...]) + _dot(oc_ref[...], wc_ref[...])
    y_ref[...] = x_ref[...] + gt_ref[...] * acc


def _outproj(oa, ob, oc, wa, wb, wc, x2, gt, tm, tiles_per_batch):
    m = x2.shape[0]
    tn = 512
    row = lambda a: pl.BlockSpec((tm, a.shape[1]), lambda i, j: (i, 0))
    wsp = lambda a: pl.BlockSpec((a.shape[0], tn), lambda i, j: (0, j))
    return pl.pallas_call(
        _outproj_kernel,
        grid=(m // tm, D_MODEL // tn),
        in_specs=[row(oa), row(ob), row(oc), wsp(wa), wsp(wb), wsp(wc),
                  pl.BlockSpec((tm, tn), lambda i, j: (i, j)),
                  _mod_spec(gt, tm, tn, tiles_per_batch)],
        out_specs=pl.BlockSpec((tm, tn), lambda i, j: (i, j)),
        out_shape=jax.ShapeDtypeStruct((m, D_MODEL), F32),
        name="out_proj",
        compiler_params=_cparams(("arbitrary", "arbitrary"), 40),
    )(oa, ob, oc, wa, wb, wc, x2, gt)


def _ffn1_kernel(x_ref, sc_ref, sh_ref, g_ref, wg_ref, wu_ref, a_ref, h_scr):
    @pl.when(pl.program_id(1) == 0)
    def _():
        _norm_rows(x_ref, g_ref, sc_ref, sh_ref, h_scr)

    h = h_scr[...]
    gate = _dot(h, wg_ref[...])
    up = _dot(h, wu_ref[...])
    a_ref[...] = (gate * _sigmoid(gate) * up).astype(BF16)


def _ffn1(x2, sc, sh, g, wg, wu, tm, tiles_per_batch):
    m = x2.shape[0]
    dff = wg.shape[1]
    tn = 512
    full_mod = lambda a: _mod_spec(a, tm, D_MODEL, tiles_per_batch)
    return pl.pallas_call(
        _ffn1_kernel,
        grid=(m // tm, dff // tn),
        in_specs=[pl.BlockSpec((tm, D_MODEL), lambda i, j: (i, 0)),
                  full_mod(sc), full_mod(sh),
                  pl.BlockSpec((1, D_MODEL), lambda i, j: (0, 0)),
                  pl.BlockSpec((D_MODEL, tn), lambda i, j: (0, j)),
                  pl.BlockSpec((D_MODEL, tn), lambda i, j: (0, j))],
        out_specs=pl.BlockSpec((tm, tn), lambda i, j: (i, j)),
        out_shape=jax.ShapeDtypeStruct((m, dff), BF16),
        scratch_shapes=[pltpu.VMEM((tm, D_MODEL), BF16)],
        name="ffn_gate_up",
        compiler_params=_cparams(("arbitrary", "arbitrary"), 44),
    )(x2, sc, sh, g, wg, wu)


def _ffn2_kernel(a_ref, wd_ref, x_ref, gt_ref, y_ref):
    y_ref[...] = x_ref[...] + gt_ref[...] * _dot(a_ref[...], wd_ref[...])


def _ffn2(a, wd, x2, gt, tm, tiles_per_batch):
    m, dff = a.shape
    tn = 256
    return pl.pallas_call(
        _ffn2_kernel,
        grid=(m // tm, D_MODEL // tn),
        in_specs=[pl.BlockSpec((tm, dff), lambda i, j: (i, 0)),
                  pl.BlockSpec((dff, tn), lambda i, j: (0, j)),
                  pl.BlockSpec((tm, tn), lambda i, j: (i, j)),
                  _mod_spec(gt, tm, tn, tiles_per_batch)],
        out_specs=pl.BlockSpec((tm, tn), lambda i, j: (i, j)),
        out_shape=jax.ShapeDtypeStruct((m, D_MODEL), F32),
        name="ffn_down",
        compiler_params=_cparams(("arbitrary", "arbitrary"), 48),
    )(a, wd, x2, gt)


def _final_norm_kernel(x_ref, g_ref, y_ref, *, rows_chunk=64):
    g = g_ref[...]

    def body(r, c):
        r0 = pl.multiple_of(r * rows_chunk, rows_chunk)
        x = x_ref[pl.ds(r0, rows_chunk), :]
        ms = jnp.mean(x * x, axis=-1, keepdims=True)
        y_ref[pl.ds(r0, rows_chunk), :] = x * lax.rsqrt(ms + EPS) * g
        return c

    lax.fori_loop(0, x_ref.shape[0] // rows_chunk, body, 0)


def _final_norm(x2, g, tm):
    m = x2.shape[0]
    return pl.pallas_call(
        _final_norm_kernel,
        grid=(m // tm,),
        in_specs=[pl.BlockSpec((tm, D_MODEL), lambda i: (i, 0)),
                  pl.BlockSpec((1, D_MODEL), lambda i: (0, 0))],
        out_specs=pl.BlockSpec((tm, D_MODEL), lambda i: (i, 0)),
        out_shape=jax.ShapeDtypeStruct((m, D_MODEL), F32),
        name="final_norm",
        compiler_params=_cparams(("arbitrary",), 40),
    )(x2, g.reshape(1, D_MODEL))


def _pad_time(a, total):
    pad = total - a.shape[1]
    if pad == 0:
        return a
    return jnp.pad(a, ((0, 0), (0, pad)) + ((0, 0),) * (a.ndim - 2))


def _layer(x2, batch, mods, wts, bias, caches, tm, prompt):
    m = x2.shape[0]
    t = m // batch
    sh1, sc1, gt1, sh2, sc2, gt2 = mods
    tiles_per_batch = max(t // tm, 1)
    q, ka, kb, kc, va, vb, vc, lf = _inproj(x2, sc1, sh1, wts["g_attn"], wts["w_qkv"], wts["w_f"], wts["b_f"],
                                            tm, tiles_per_batch)
    r3 = lambda a: a.reshape(batch, t, a.shape[-1])
    q, ka, kb, kc, va, vb, vc = map(r3, (q, ka, kb, kc, va, vb, vc))
    logf = r3(lf)[:, :, :H_B]

    if prompt:
        keep = min(A_LEFT, t)
        state = (ka[:, t - keep:], va[:, t - keep:], kb, vb, logf, kc, vc)
        f = _cumsum_time(jnp.swapaxes(logf, 1, 2))
        oa = _band_attn(q, ka, va, bias, CHUNK, A_BAND - CHUNK, A_BAND - CHUNK - A_LEFT)
        blk = 256
        ob = _causal_attn("fox", q, kb, vb, f, H_A, blk, blk, blk, 0)
        oc = _causal_attn("stick", q, kc, vc, None, H_A + H_B, blk, blk, blk, 0)
    else:
        cak, cav, cbk, cbv, cblogf, cck, ccv = caches
        state = (ka, va, kb, vb, logf, kc, vc)
        flat = lambda c: c.reshape(c.shape[0], c.shape[1], -1)
        past = cbk.shape[1]
        tkd = LANES
        tkv = past + tkd
        cat = lambda c, n: _pad_time(jnp.concatenate([flat(c), n], axis=1), tkv)
        kka = jnp.concatenate([flat(cak), ka], axis=1)
        vva = jnp.concatenate([flat(cav), va], axis=1)
        wa = kka.shape[1]
        oa = _band_attn(q, kka, vva, bias, t, A_BAND - wa, A_BAND - wa)
        lcat = jnp.concatenate([cblogf.astype(F32), logf], axis=1)
        fpad = -(-tkv // (8 * LANES)) * (8 * LANES)
        f = _cumsum_time(jnp.swapaxes(_pad_time(lcat, fpad), 1, 2))
        ob = _causal_attn("fox", q, cat(cbk, kb), cat(cbv, vb), f, H_A, t, tkd, 256, past)
        oc = _causal_attn("stick", q, cat(cck, kc), cat(ccv, vc), None, H_A + H_B, t, tkd, 256, past)

    r2 = lambda a: a.reshape(m, a.shape[-1])
    x2 = _outproj(r2(oa), r2(ob), r2(oc), wts["wo_a"], wts["wo_b"], wts["wo_c"], x2, gt1, tm, tiles_per_batch)
    a = _ffn1(x2, sc2, sh2, wts["g_ffn"], wts["w_gate"], wts["w_up"], tm, tiles_per_batch)
    x2 = _ffn2(a, wts["w_down"], x2, gt2, tm, tiles_per_batch)
    return x2, state


def kernel(x_prompt, x_sample, cache_a_k, cache_a_v, cache_b_k, cache_b_v, cache_b_logf, cache_c_k, cache_c_v,
           c_prompt, c_sample, w_ada, b_ada, g_attn, g_ffn, w_in, b_f, rel_bias, w_o, w_gate, w_up, w_down, g_final):
    depth = w_ada.shape[0]
    bp, tp, _ = x_prompt.shape
    bs, ts, _ = x_sample.shape
    n_qkv = 3 * D_MODEL

    rows = -(-(bp + bs) // 8) * 8
    c_all = jnp.concatenate([c_prompt, c_sample, jnp.zeros((rows - bp - bs, D_MODEL), F32)], axis=0)
    mod = _ada(c_all, w_ada, b_ada)

    bias_p = _relbias(rel_bias, CHUNK, A_BAND)
    bias_s = _relbias(rel_bias, ts, A_BAND)

    xp = x_prompt.reshape(bp * tp, D_MODEL)
    xs = x_sample.reshape(bs * ts, D_MODEL)
    tm_p, tm_s = 1024, bs * ts
    states_p, states_s = [], []
    ea = H_A * HEAD_DIM
    eb = (H_A + H_B) * HEAD_DIM
    for l in range(depth):
        wo = w_o[l].astype(BF16)
        wts = dict(
            g_attn=g_attn[l].reshape(1, D_MODEL), g_ffn=g_ffn[l].reshape(1, D_MODEL),
            w_qkv=w_in[l, :, :n_qkv].astype(BF16),
            w_f=jnp.pad(w_in[l, :, n_qkv:], ((0, 0), (0, LANES - H_B))).astype(BF16),
            b_f=jnp.pad(b_f[l], (0, LANES - H_B)).reshape(1, LANES),
            wo_a=wo[:ea], wo_b=wo[ea:eb], wo_c=wo[eb:],
            w_gate=w_gate[l].astype(BF16), w_up=w_up[l].astype(BF16), w_down=w_down[l].astype(BF16))
        chunks = jnp.split(mod[l], 6, axis=-1)
        mods_p = [c[:bp].reshape(bp, 1, D_MODEL) for c in chunks]
        mods_s = [jnp.repeat(c[bp:bp + bs], ts, axis=0).reshape(1, bs * ts, D_MODEL) for c in chunks]
        xp, st_p = _layer(xp, bp, mods_p, wts, bias_p[l], None, tm_p, True)
        caches = (cache_a_k[l], cache_a_v[l], cache_b_k[l], cache_b_v[l], cache_b_logf[l],
                  cache_c_k[l], cache_c_v[l])
        xs, st_s = _layer(xs, bs, mods_s, wts, bias_s[l], caches, tm_s, False)
        states_p.append(st_p)
        states_s.append(st_s)

    def stack(states, heads):
        outs = []
        for idx, arrs in enumerate(zip(*states)):
            a = jnp.stack(arrs)
            if idx != 4:
                a = a.reshape(a.shape[:3] + (heads[idx], HEAD_DIM))
            outs.append(a)
        return outs

    heads = (H_A, H_A, H_B, H_B, None, H_C, H_C)
    y_prompt = _final_norm(xp, g_final, 512).reshape(bp, tp, D_MODEL)
    y_sample = _final_norm(xs, g_final, bs * ts).reshape(bs, ts, D_MODEL)
    return tuple([y_prompt, y_sample] + stack(states_p, heads) + stack(states_s, heads))
```

```python
import functools

import jax
import jax.numpy as jnp
from jax import lax
from jax.experimental import pallas as pl
from jax.experimental.pallas import tpu as pltpu

F32 = jnp.float32
BF16 = jnp.bfloat16

D_MODEL = 2048
HEAD_DIM = 128
N_HEADS = D_MODEL // HEAD_DIM
H_A = N_HEADS // 4
H_B = (N_HEADS - H_A) // 2
H_C = N_HEADS - H_A - H_B
CHUNK = 64
A_LEFT = 8 * CHUNK
REL_CLIP = 128
N_REL = 2 * REL_CLIP + 1
EPS = 1e-6
ATTN_SCALE = HEAD_DIM ** -0.5

V7X_VMEM_BYTES = 64 * 1024 * 1024
LANES = 128
HEAD_PAIR = 2 * HEAD_DIM
A_BAND = 640


def _cparams(sem, vmem_mb):
    return pltpu.CompilerParams(dimension_semantics=sem,
                                vmem_limit_bytes=min(vmem_mb * 1024 * 1024, V7X_VMEM_BYTES - (4 << 20)))


def _dot(a, b):
    return jnp.dot(a, b, preferred_element_type=F32)


def _dot_nt(a, b):
    return lax.dot_general(a, b, (((1,), (1,)), ((), ())), preferred_element_type=F32)


def _sigmoid(x):
    return 1.0 / (1.0 + jnp.exp(-x))


def _log_sigmoid(x):
    return jnp.minimum(x, 0.0) - jnp.log1p(jnp.exp(-jnp.abs(x)))


def _modulated_norm(x, g, sc, sh):
    ms = jnp.mean(x * x, axis=-1, keepdims=True)
    y = x * lax.rsqrt(ms + EPS) * g
    return y * (1.0 + sc) + sh


def _norm_rows(x_ref, g_ref, sc_ref, sh_ref, h_ref, rows_chunk=64):
    tm = x_ref.shape[0]
    g = g_ref[...]
    per_row = sc_ref.shape[0] != 1

    def body(r, c):
        r0 = pl.multiple_of(r * rows_chunk, rows_chunk)
        sc = sc_ref[pl.ds(r0, rows_chunk), :] if per_row else sc_ref[...]
        sh = sh_ref[pl.ds(r0, rows_chunk), :] if per_row else sh_ref[...]
        h = _modulated_norm(x_ref[pl.ds(r0, rows_chunk), :], g, sc, sh)
        h_ref[pl.ds(r0, rows_chunk), :] = h.astype(h_ref.dtype)
        return c

    lax.fori_loop(0, tm // rows_chunk, body, 0)


def _ada_kernel(c_ref, w_ref, b_ref, o_ref):
    c = c_ref[...]
    a = (c * _sigmoid(c)).astype(BF16)
    o_ref[...] = _dot(a, w_ref[...].astype(BF16)) + b_ref[...]


def _ada(c_all, w_ada, b_ada):
    depth, _, n = w_ada.shape
    rows = c_all.shape[0]
    tn = 1024
    return pl.pallas_call(
        _ada_kernel,
        grid=(depth, n // tn),
        in_specs=[pl.BlockSpec((rows, D_MODEL), lambda l, j: (0, 0)),
                  pl.BlockSpec((None, D_MODEL, tn), lambda l, j: (l, 0, j)),
                  pl.BlockSpec((None, 1, tn), lambda l, j: (l, 0, j))],
        out_specs=pl.BlockSpec((None, rows, tn), lambda l, j: (l, 0, j)),
        out_shape=jax.ShapeDtypeStruct((depth, rows, n), F32),
        name="ada_mod",
        compiler_params=_cparams(("arbitrary", "arbitrary"), 40),
    )(c_all, w_ada, b_ada.reshape(depth, 1, n))


def _relbias_kernel(tab_ref, o_ref, *, cq, band):
    l = pl.program_id(0)
    h = pl.program_id(1)
    qi = lax.broadcasted_iota(jnp.int32, (cq, band), 0)
    s = lax.broadcasted_iota(jnp.int32, (cq, band), 1)
    idx = jnp.clip(qi + (band - cq) - s, -REL_CLIP, REL_CLIP) + REL_CLIP
    base = l * (N_REL * H_A) + h

    def body(r, acc):
        return jnp.where(idx == r, tab_ref[base + r * H_A], acc)

    o_ref[...] = lax.fori_loop(0, N_REL, body, jnp.zeros((cq, band), F32))


def _relbias(rel_bias, cq, band):
    depth = rel_bias.shape[0]
    return pl.pallas_call(
        functools.partial(_relbias_kernel, cq=cq, band=band),
        grid=(depth, H_A),
        in_specs=[pl.BlockSpec(memory_space=pltpu.SMEM)],
        out_specs=pl.BlockSpec((None, None, cq, band), lambda l, h: (l, h, 0, 0)),
        out_shape=jax.ShapeDtypeStruct((depth, H_A, cq, band), F32),
        name="rel_bias",
    )(rel_bias.reshape(-1))


def _inproj_kernel(x_ref, sc_ref, sh_ref, g_ref, wq_ref, wk_ref, wv_ref, wf_ref, bf_ref,
                   q_ref, ka_ref, kb_ref, kc_ref, va_ref, vb_ref, vc_ref, lf_ref, h_scr):
    j = pl.program_id(1)

    @pl.when(j == 0)
    def _():
        _norm_rows(x_ref, g_ref, sc_ref, sh_ref, h_scr)
        lf_ref[...] = _log_sigmoid(_dot(h_scr[...], wf_ref[...]) + bf_ref[...])

    h = h_scr[...]
    q_ref[...] = _dot(h, wq_ref[...]).astype(BF16)
    kt = _dot(h, wk_ref[...])
    vt = _dot(h, wv_ref[...])
    pa = H_A // 2
    pb = (H_A + H_B) // 2

    @pl.when(j < pa)
    def _():
        ka_ref[...] = kt
        va_ref[...] = vt

    @pl.when((j >= pa) & (j < pb))
    def _():
        kb_ref[...] = kt
        vb_ref[...] = vt

    @pl.when(j >= pb)
    def _():
        kc_ref[...] = kt
        vc_ref[...] = vt


def _mod_spec(mod, tm, tn, tiles_per_batch):
    r = mod.shape[1]
    col = (lambda j: 0) if tn == mod.shape[2] else (lambda j: j)
    if r == 1:
        return pl.BlockSpec((None, 1, tn), lambda i, j: (i // tiles_per_batch, 0, col(j)))
    return pl.BlockSpec((None, tm, tn), lambda i, j: (0, i, col(j)))


def _inproj(x2, sc, sh, g, w_qkv, w_f, b_f, tm, tiles_per_batch):
    m = x2.shape[0]
    tn = HEAD_PAIR
    npair = N_HEADS // 2
    pa, pb = H_A // 2, (H_A + H_B) // 2

    def clamp(lo, n):
        return lambda i, j: (i, jnp.clip(j - lo, 0, n - 1))

    def full_mod(a):
        return _mod_spec(a, tm, D_MODEL, tiles_per_batch)

    out_shape = [jax.ShapeDtypeStruct((m, D_MODEL), BF16)]
    out_specs = [pl.BlockSpec((tm, tn), lambda i, j: (i, j))]
    for _ in range(2):
        for lo, n in ((0, pa), (pa, pb - pa), (pb, npair - pb)):
            out_shape.append(jax.ShapeDtypeStruct((m, n * tn), F32))
            out_specs.append(pl.BlockSpec((tm, tn), clamp(lo, n)))
    out_shape.append(jax.ShapeDtypeStruct((m, LANES), F32))
    out_specs.append(pl.BlockSpec((tm, LANES), lambda i, j: (i, 0)))
    outs = pl.pallas_call(
        _inproj_kernel,
        grid=(m // tm, npair),
        in_specs=[pl.BlockSpec((tm, D_MODEL), lambda i, j: (i, 0)),
                  full_mod(sc), full_mod(sh),
                  pl.BlockSpec((1, D_MODEL), lambda i, j: (0, 0)),
                  pl.BlockSpec((D_MODEL, tn), lambda i, j: (0, j)),
                  pl.BlockSpec((D_MODEL, tn), lambda i, j: (0, npair + j)),
                  pl.BlockSpec((D_MODEL, tn), lambda i, j: (0, 2 * npair + j)),
                  pl.BlockSpec((D_MODEL, LANES), lambda i, j: (0, 0)),
                  pl.BlockSpec((1, LANES), lambda i, j: (0, 0))],
        out_specs=out_specs,
        out_shape=out_shape,
        scratch_shapes=[pltpu.VMEM((tm, D_MODEL), BF16)],
        name="in_proj",
        compiler_params=_cparams(("arbitrary", "arbitrary"), 52),
    )(x2, sc, sh, g, w_qkv, w_qkv, w_qkv, w_f, b_f)
    q, ka, kb, kc, va, vb, vc, lf = outs
    return q, ka, kb, kc, va, vb, vc, lf


def _split3(x):
    hi = x.astype(BF16)
    r1 = x - hi.astype(F32)
    mid = r1.astype(BF16)
    lo = (r1 - mid.astype(F32)).astype(BF16)
    return hi, mid, lo


def _dot_f32_by_01(x, ones01):
    hi, mid, lo = _split3(x)
    return _dot(hi, ones01) + _dot(mid, ones01) + _dot(lo, ones01)


def _cumsum_kernel(x_ref, o_ref):
    x = x_ref[...]
    rows = x.shape[0]
    i0 = lax.broadcasted_iota(jnp.int32, (LANES, LANES), 0)
    i1 = lax.broadcasted_iota(jnp.int32, (LANES, LANES), 1)
    upper = (i0 <= i1).astype(BF16)
    c = _dot_f32_by_01(x, upper)
    tot = jnp.broadcast_to(c[:, LANES - 1:LANES], (rows, LANES))
    r0 = lax.broadcasted_iota(jnp.int32, (rows, rows), 0)
    r1 = lax.broadcasted_iota(jnp.int32, (rows, rows), 1)
    lower = (r1 < r0).astype(BF16)
    hi, mid, lo = _split3(tot)
    off = _dot(lower, hi) + _dot(lower, mid) + _dot(lower, lo)
    o_ref[...] = c + off


def _cumsum_time(logf_hm):
    b, h, t = logf_hm.shape
    rows = t // LANES
    out = pl.pallas_call(
        _cumsum_kernel,
        grid=(b * h,),
        in_specs=[pl.BlockSpec((None, rows, LANES), lambda i: (i, 0, 0))],
        out_specs=pl.BlockSpec((None, rows, LANES), lambda i: (i, 0, 0)),
        out_shape=jax.ShapeDtypeStruct((b * h, rows, LANES), F32),
        name="forget_cumsum",
    )(logf_hm.reshape(b * h, rows, LANES))
    return out.reshape(b, h, t)


def _band_kernel(q_ref, k_ref, v_ref, bias_ref, o_ref, kp_scr, vp_scr, *, cq, npad, nmask, nchunks):
    tk = k_ref.shape[0]
    kp_scr[0:npad, :] = jnp.zeros((npad, HEAD_DIM), BF16)
    vp_scr[0:npad, :] = jnp.zeros((npad, HEAD_DIM), BF16)
    _copy_cast(k_ref, kp_scr, tk, dst_off=npad)
    _copy_cast(v_ref, vp_scr, tk, dst_off=npad)
    bias = bias_ref[...]
    slot = lax.broadcasted_iota(jnp.int32, (cq, A_BAND), 1)

    def chunk(n, carry):
        r0 = pl.multiple_of(n * cq, cq)
        q = q_ref[pl.ds(r0, cq), :]
        k = kp_scr[pl.ds(r0, A_BAND), :]
        v = vp_scr[pl.ds(r0, A_BAND), :]
        s = _dot_nt(q, k) * ATTN_SCALE + bias
        first_valid = jnp.maximum(nmask, npad - n * cq)
        s = jnp.where(slot >= first_valid, s, -jnp.inf)
        m = jnp.max(s, axis=-1, keepdims=True)
        p = jnp.exp(s - m)
        l = jnp.sum(p, axis=-1, keepdims=True)
        o = _dot(p.astype(BF16), v) / l
        o_ref[pl.ds(r0, cq), :] = o.astype(o_ref.dtype)
        return carry

    lax.fori_loop(0, nchunks, chunk, 0)


def _band_attn(q, k, v, bias, cq, npad, nmask):
    b, t, _ = q.shape
    tk = k.shape[1]
    kern = functools.partial(_band_kernel, cq=cq, npad=npad, nmask=nmask, nchunks=t // cq)
    return pl.pallas_call(
        kern,
        grid=(b, H_A),
        in_specs=[pl.BlockSpec((None, t, HEAD_DIM), lambda i, h: (i, 0, h)),
                  pl.BlockSpec((None, tk, HEAD_DIM), lambda i, h: (i, 0, h)),
                  pl.BlockSpec((None, tk, HEAD_DIM), lambda i, h: (i, 0, h)),
                  pl.BlockSpec((None, cq, A_BAND), lambda i, h: (h, 0, 0))],
        out_specs=pl.BlockSpec((None, t, HEAD_DIM), lambda i, h: (i, 0, h)),
        out_shape=jax.ShapeDtypeStruct((b, t, H_A * HEAD_DIM), BF16),
        scratch_shapes=[pltpu.VMEM((npad + tk, HEAD_DIM), BF16),
                        pltpu.VMEM((npad + tk, HEAD_DIM), BF16)],
        name="band_attn",
        compiler_params=_cparams(("arbitrary", "arbitrary"), 48),
    )(q, k, v, bias)


def _copy_cast(src_ref, dst_ref, rows, dst_off=0, step=256):
    n = rows // step

    def body(i, c):
        r = pl.multiple_of(i * step, step)
        d = pl.multiple_of(dst_off + i * step, 16)
        dst_ref[pl.ds(d, step), :] = src_ref[pl.ds(r, step), :].astype(dst_ref.dtype)
        return c

    lax.fori_loop(0, n, body, 0)
    if rows % step:
        dst_ref[dst_off + n * step:dst_off + rows, :] = src_ref[n * step:rows, :].astype(dst_ref.dtype)


def _row_to_col(row):
    n = row.shape[1]
    cols = []
    for c in range(n // LANES):
        blk = jnp.broadcast_to(row[:, c * LANES:(c + 1) * LANES], (LANES, LANES))
        cols.append(blk.T[:, 0:1])
    return cols[0] if len(cols) == 1 else jnp.concatenate(cols, axis=0)


def _fox_kernel(q_ref, k_ref, v_ref, f_ref, o_ref, k_scr, v_scr, acc_scr, m_scr, l_scr,
                *, tq, tkd, tk, nq, qoff):
    tkv = k_ref.shape[0]
    _copy_cast(k_ref, k_scr, tkv)
    _copy_cast(v_ref, v_scr, tkv)
    row = lax.broadcasted_iota(jnp.int32, (tq, tkd), 0)
    col = lax.broadcasted_iota(jnp.int32, (tq, tkd), 1)
    causal = col <= row
    tqp = max(tq, LANES)

    def block(q, fq, kb, width, mask):
        k0 = pl.multiple_of(kb * tk, tk)
        k = k_scr[pl.ds(k0, width), :]
        v = v_scr[pl.ds(k0, width), :]
        fk = f_ref[kb][:, 0:width]
        s = _dot_nt(q, k) * ATTN_SCALE + fq - fk
        if mask is not None:
            s = jnp.where(mask, s, -jnp.inf)
        m_old = m_scr[...]
        m_new = jnp.maximum(m_old, jnp.max(s, axis=-1, keepdims=True))
        alpha = jnp.exp(m_old - m_new)
        p = jnp.exp(s - m_new)
        l_scr[...] = alpha * l_scr[...] + jnp.sum(p, axis=-1, keepdims=True)
        acc_scr[...] = alpha * acc_scr[...] + _dot(p.astype(BF16), v)
        m_scr[...] = m_new

    def qblock(i, carry):
        q0 = pl.multiple_of(i * tq, tq)
        pb = (qoff + i * tq) // tk
        q = q_ref[pl.ds(q0, tq), :]
        fq = _row_to_col(f_ref[pb][:, 0:tqp])[0:tq]
        m_scr[...] = jnp.full((tq, 1), -jnp.inf, F32)
        l_scr[...] = jnp.zeros((tq, 1), F32)
        acc_scr[...] = jnp.zeros((tq, HEAD_DIM), F32)
        block(q, fq, pb, tkd, causal)

        def full(jj, c):
            block(q, fq, jj, tk, None)
            return c

        lax.fori_loop(0, pb, full, 0)
        o_ref[pl.ds(q0, tq), :] = (acc_scr[...] / l_scr[...]).astype(o_ref.dtype)
        return carry

    lax.fori_loop(0, nq, qblock, 0)


def _stick_kernel(q_ref, k_ref, v_ref, o_ref, k_scr, v_scr, acc_scr, a_scr, msd_scr, msf_scr,
                  *, tq, tkd, tk, nq, qoff):
    tkv = k_ref.shape[0]
    _copy_cast(k_ref, k_scr, tkv)
    _copy_cast(v_ref, v_scr, tkv)
    for ref, w in ((msd_scr, tkd), (msf_scr, tk)):
        j0 = lax.broadcasted_iota(jnp.int32, (w, w), 0)
        j1 = lax.broadcasted_iota(jnp.int32, (w, w), 1)
        ref[...] = (j0 > j1).astype(BF16)
    row = lax.broadcasted_iota(jnp.int32, (tq, tkd), 0)
    col = lax.broadcasted_iota(jnp.int32, (tq, tkd), 1)
    causal = col < row

    def block(q, k0, width, mask, ms_ref):
        k = k_scr[pl.ds(k0, width), :]
        v = v_scr[pl.ds(k0, width), :]
        z = _dot_nt(q, k) * ATTN_SCALE
        lsg = _log_sigmoid(z)
        stay = lsg - z
        if mask is not None:
            stay = jnp.where(mask, stay, 0.0)
        hi = stay.astype(BF16)
        lo = (stay - hi.astype(F32)).astype(BF16)
        ms = ms_ref[...]
        within = _dot(hi, ms) + _dot(lo, ms)
        a_old = a_scr[...]
        w = jnp.exp(lsg + within + a_old)
        if mask is not None:
            w = jnp.where(mask, w, 0.0)
        acc_scr[...] += _dot(w.astype(BF16), v)
        a_scr[...] = a_old + within[:, 0:1] + stay[:, 0:1]

    def qblock(i, carry):
        q0 = pl.multiple_of(i * tq, tq)
        p0 = pl.multiple_of(qoff + i * tq, tq)
        q = q_ref[pl.ds(q0, tq), :]
        a_scr[...] = jnp.zeros((tq, 1), F32)
        acc_scr[...] = jnp.zeros((tq, HEAD_DIM), F32)
        block(q, p0, tkd, causal, msd_scr)
        nfull = p0 // tk

        def full(jj, c):
            block(q, pl.multiple_of((nfull - 1 - jj) * tk, tk), tk, None, msf_scr)
            return c

        lax.fori_loop(0, nfull, full, 0)
        o_ref[pl.ds(q0, tq), :] = acc_scr[...].astype(o_ref.dtype)
        return carry

    lax.fori_loop(0, nq, qblock, 0)


def _causal_attn(kind, q, k, v, f, head0, tq, tkd, tk, qoff):
    b, t, _ = q.shape
    tkv = k.shape[1]
    nh = k.shape[2] // HEAD_DIM
    assert qoff % tk == 0 and (tq % tk == 0 or t == tq) and tkd <= tk and tq <= tkd
    common = dict(tq=tq, tkd=tkd, tk=tk, nq=t // tq, qoff=qoff)
    kv_spec = pl.BlockSpec((None, tkv, HEAD_DIM), lambda i, h: (i, 0, h))
    in_specs = [pl.BlockSpec((None, t, HEAD_DIM), lambda i, h: (i, 0, head0 + h)), kv_spec, kv_spec]
    scratch = [pltpu.VMEM((tkv, HEAD_DIM), BF16), pltpu.VMEM((tkv, HEAD_DIM), BF16),
               pltpu.VMEM((tq, HEAD_DIM), F32)]
    if kind == "fox":
        kern = functools.partial(_fox_kernel, **common)
        nblk = f.shape[2] // tk
        f = f.reshape(b, nh, nblk, 1, tk)
        in_specs.append(pl.BlockSpec((None, None, nblk, 1, tk), lambda i, h: (i, h, 0, 0, 0)))
        scratch += [pltpu.VMEM((tq, 1), F32), pltpu.VMEM((tq, 1), F32)]
        args = (q, k, v, f)
    else:
        kern = functools.partial(_stick_kernel, **common)
        scratch += [pltpu.VMEM((tq, 1), F32), pltpu.VMEM((tkd, tkd), BF16), pltpu.VMEM((tk, tk), BF16)]
        args = (q, k, v)
    return pl.pallas_call(
        kern,
        grid=(b, nh),
        in_specs=in_specs,
        out_specs=pl.BlockSpec((None, t, HEAD_DIM), lambda i, h: (i, 0, h)),
        out_shape=jax.ShapeDtypeStruct((b, t, nh * HEAD_DIM), BF16),
        scratch_shapes=scratch,
        name=kind + "_attn",
        compiler_params=_cparams(("arbitrary", "arbitrary"), 52),
    )(*args)


SUBLANES = 8
LOG2E = 1.4426950408889634


def _col_replicate(row):
    n = row.shape[1]
    parts = [jnp.broadcast_to(row[:, c * LANES:(c + 1) * LANES], (LANES, LANES)).T for c in range(n // LANES)]
    return parts[0] if len(parts) == 1 else jnp.concatenate(parts, axis=0)


def _aug_columns(col_rep, first):
    hi, mid, lo = _split3(col_rep)
    lane = lax.broadcasted_iota(jnp.int32, col_rep.shape, 1)
    base = 0 if first else 3
    one = jnp.where((lane >= 3 - base) & (lane < 6 - base), 1.0, 0.0)
    x = jnp.where(lane == base, hi.astype(F32),
                  jnp.where(lane == base + 1, mid.astype(F32),
                            jnp.where(lane == base + 2, lo.astype(F32), one)))
    return x.astype(BF16)


def _fox2_kernel(q_ref, k_ref, v_ref, g_ref, o_ref, kp_scr, vt_scr, qp_scr, acc_scr, m_scr, l_scr,
                 *, tq, tk, nq, qoff, ng):
    nblk = kp_scr.shape[0]
    c2 = ATTN_SCALE * LOG2E

    def prep(kb, c):
        r0 = pl.multiple_of(kb * tk, tk)
        kp_scr[kb, :, 0:HEAD_DIM] = k_ref[pl.ds(r0, tk), :].astype(BF16)
        kp_scr[kb, :, HEAD_DIM:2 * HEAD_DIM] = _aug_columns(_col_replicate(-g_ref[kb]), True)
        vt_scr[kb] = v_ref[pl.ds(r0, tk), :].T.astype(BF16)
        return c

    lax.fori_loop(0, nblk, prep, 0)
    off = qoff % tk
    row = lax.broadcasted_iota(jnp.int32, (tk, tq), 0)
    col = lax.broadcasted_iota(jnp.int32, (tk, tq), 1)

    def group(g, p0):
        qp = qp_scr[...]
        xs = []
        for u in range(ng):
            x = _dot_nt(kp_scr[g * ng + u], qp)
            if p0 is not None:
                x = jnp.where(row + ((g * ng + u) * tk - p0) <= col, x, -jnp.inf)
            xs.append(x)
        m_blk = jnp.max(xs[0], axis=0, keepdims=True)
        for x in xs[1:]:
            m_blk = jnp.maximum(m_blk, jnp.max(x, axis=0, keepdims=True))
        m_old = m_scr[...]
        m_new = jnp.maximum(m_old, m_blk)
        alpha = jnp.exp2((m_old - m_new) * c2)
        l_new = alpha * l_scr[...]
        acc = alpha * acc_scr[...]
        for u in range(ng):
            p = jnp.exp2((xs[u] - m_new) * c2)
            l_new = l_new + jnp.sum(p, axis=0, keepdims=True)
            acc = acc + _dot(vt_scr[g * ng + u], p.astype(BF16))
        l_scr[...] = l_new
        acc_scr[...] = acc
        m_scr[...] = m_new

    def qblock(i, carry):
        q0 = pl.multiple_of(i * tq, tq)
        p0 = qoff + i * tq
        pb = p0 // tk
        qp_scr[:, 0:HEAD_DIM] = q_ref[pl.ds(q0, tq), :]
        qp_scr[:, HEAD_DIM:2 * HEAD_DIM] = _aug_columns(_col_replicate(g_ref[pb][:, off:off + tq]), False)
        m_scr[...] = jnp.full((1, tq), -jnp.inf, F32)
        l_scr[...] = jnp.zeros((1, tq), F32)
        acc_scr[...] = jnp.zeros((HEAD_DIM, tq), F32)
        gd = pb // ng
        group(gd, p0)

        def full(g, c):
            group(g, None)
            return c

        lax.fori_loop(0, gd, full, 0)
        o_ref[pl.ds(q0, tq), :] = (acc_scr[...] / l_scr[...]).T.astype(o_ref.dtype)
        return carry

    lax.fori_loop(0, nq, qblock, 0)


def _stick2_kernel(q_ref, k_ref, v_ref, o_ref, kp_scr, vt_scr, acc_scr, *, tq, tk, nq, qoff, ng):
    nblk = kp_scr.shape[0]
    nv = tk // SUBLANES
    c2 = ATTN_SCALE * LOG2E

    r = lax.broadcasted_iota(jnp.int32, (tk, tk), 0)
    kk = lax.broadcasted_iota(jnp.int32, (tk, tk), 1)
    perm = ((r % SUBLANES) * nv + r // SUBLANES == kk).astype(BF16)
    perm_t = ((kk % SUBLANES) * nv + kk // SUBLANES == r).astype(BF16)

    def prep(kb, c):
        r0 = pl.multiple_of(kb * tk, tk)
        kp_scr[kb] = _dot(perm, k_ref[pl.ds(r0, tk), :].astype(BF16)).astype(BF16)
        vt_scr[kb] = _dot(v_ref[pl.ds(r0, tk), :].T.astype(BF16), perm_t).astype(BF16)
        return c

    lax.fori_loop(0, nblk, prep, 0)
    row = lax.broadcasted_iota(jnp.int32, (tk, tq), 0)
    col = lax.broadcasted_iota(jnp.int32, (tk, tq), 1)
    key_in_block = (row % SUBLANES) * nv + row // SUBLANES
    sub = lax.broadcasted_iota(jnp.int32, (SUBLANES, tq), 0)

    def tile_terms(q, kb, p0):
        x = _dot_nt(kp_scr[kb], q)
        e = jnp.exp2(jnp.minimum(x * (-c2), 126.0))
        beta = 1.0 / (1.0 + e)
        stay = e * beta
        valid = None
        if p0 is not None:
            valid = key_in_block + (kb * tk - p0) < col
            stay = jnp.where(valid, stay, 1.0)
        after = [None] * nv
        run = jnp.ones((SUBLANES, tq), F32)
        for v in range(nv - 1, -1, -1):
            after[v] = run
            run = run * stay[v * SUBLANES:(v + 1) * SUBLANES, :]
        incl = run
        for k in (1, 2, 4):
            incl = incl * jnp.where(sub + k < SUBLANES, pltpu.roll(incl, SUBLANES - k, axis=0), 1.0)
        later = jnp.where(sub + 1 < SUBLANES, pltpu.roll(incl, SUBLANES - 1, axis=0), 1.0)
        return after, later, incl[0:1, :], beta, valid

    def group(q, g, carry, p0):
        terms = [tile_terms(q, g * ng + u, p0) for u in range(ng)]
        acc = acc_scr[...]
        for u in range(ng - 1, -1, -1):
            after, later, total, beta, valid = terms[u]
            scale_u = later * carry
            w = jnp.concatenate([after[v] * scale_u for v in range(nv)], axis=0) * beta
            if valid is not None:
                w = jnp.where(valid, w, 0.0)
            acc = acc + _dot(vt_scr[g * ng + u], w.astype(BF16))
            carry = carry * total
        acc_scr[...] = acc
        return carry

    def qblock(i, c0):
        q0 = pl.multiple_of(i * tq, tq)
        p0 = qoff + i * tq
        gd = (p0 // tk) // ng
        q = q_ref[pl.ds(q0, tq), :]
        acc_scr[...] = jnp.zeros((HEAD_DIM, tq), F32)
        carry = group(q, gd, jnp.ones((1, tq), F32), p0)
        lax.fori_loop(0, gd, lambda jj, c: group(q, gd - 1 - jj, c, None), carry)
        o_ref[pl.ds(q0, tq), :] = acc_scr[...].T.astype(o_ref.dtype)
        return c0

    lax.fori_loop(0, nq, qblock, 0)


def _causal_attn2(kind, q, k, v, f, head0, tq, tk, qoff, ng):
    b, t, _ = q.shape
    tkv = k.shape[1]
    nh = k.shape[2] // HEAD_DIM
    nblk = tkv // tk
    assert tkv % tk == 0 and nblk % ng == 0 and tq <= tk and tq % LANES == 0 and (tq == tk or t == tq)
    assert (qoff % tk) + tq <= tk
    common = dict(tq=tq, tk=tk, nq=t // tq, qoff=qoff, ng=ng)
    kv_spec = pl.BlockSpec((None, tkv, HEAD_DIM), lambda i, h: (i, 0, h))
    in_specs = [pl.BlockSpec((None, t, HEAD_DIM), lambda i, h: (i, 0, head0 + h)), kv_spec, kv_spec]
    if kind == "fox":
        kern = functools.partial(_fox2_kernel, **common)
        g = (f * (1.0 / ATTN_SCALE)).reshape(b, nh, nblk, 1, tk)
        in_specs.append(pl.BlockSpec((None, None, nblk, 1, tk), lambda i, h: (i, h, 0, 0, 0)))
        scratch = [pltpu.VMEM((nblk, tk, 2 * HEAD_DIM), BF16), pltpu.VMEM((nblk, HEAD_DIM, tk), BF16),
                   pltpu.VMEM((tq, 2 * HEAD_DIM), BF16), pltpu.VMEM((HEAD_DIM, tq), F32),
                   pltpu.VMEM((1, tq), F32), pltpu.VMEM((1, tq), F32)]
        args = (q, k, v, g)
    else:
        kern = functools.partial(_stick2_kernel, **common)
        scratch = [pltpu.VMEM((nblk, tk, HEAD_DIM), BF16), pltpu.VMEM((nblk, HEAD_DIM, tk), BF16),
                   pltpu.VMEM((HEAD_DIM, tq), F32)]
        args = (q, k, v)
    return pl.pallas_call(
        kern,
        grid=(b, nh),
        in_specs=in_specs,
        out_specs=pl.BlockSpec((None, t, HEAD_DIM), lambda i, h: (i, 0, h)),
        out_shape=jax.ShapeDtypeStruct((b, t, nh * HEAD_DIM), BF16),
        scratch_shapes=scratch,
        name=kind + "_attn",
        compiler_params=_cparams(("arbitrary", "arbitrary"), 52),
    )(*args)


def _outproj_kernel(oa_ref, ob_ref, oc_ref, wa_ref, wb_ref, wc_ref, x_ref, gt_ref, y_ref):
    acc = _dot(oa_ref[...], wa_ref[...]) + _dot(ob_ref[...], wb_ref[...]) + _dot(oc_ref[...], wc_ref[...])
    y_ref[...] = x_ref[...] + gt_ref[...] * acc


def _outproj(oa, ob, oc, wa, wb, wc, x2, gt, tm, tiles_per_batch):
    m = x2.shape[0]
    tn = 512
    row = lambda a: pl.BlockSpec((tm, a.shape[1]), lambda i, j: (i, 0))
    wsp = lambda a: pl.BlockSpec((a.shape[0], tn), lambda i, j: (0, j))
    return pl.pallas_call(
        _outproj_kernel,
        grid=(m // tm, D_MODEL // tn),
        in_specs=[row(oa), row(ob), row(oc), wsp(wa), wsp(wb), wsp(wc),
                  pl.BlockSpec((tm, tn), lambda i, j: (i, j)),
                  _mod_spec(gt, tm, tn, tiles_per_batch)],
        out_specs=pl.BlockSpec((tm, tn), lambda i, j: (i, j)),
        out_shape=jax.ShapeDtypeStruct((m, D_MODEL), F32),
        name="out_proj",
        compiler_params=_cparams(("arbitrary", "arbitrary"), 40),
    )(oa, ob, oc, wa, wb, wc, x2, gt)


def _ffn1_kernel(x_ref, sc_ref, sh_ref, g_ref, wg_ref, wu_ref, a_ref, h_scr):
    @pl.when(pl.program_id(1) == 0)
    def _():
        _norm_rows(x_ref, g_ref, sc_ref, sh_ref, h_scr)

    h = h_scr[...]
    gate = _dot(h, wg_ref[...])
    up = _dot(h, wu_ref[...])
    a_ref[...] = (gate * _sigmoid(gate) * up).astype(BF16)


def _ffn1(x2, sc, sh, g, wg, wu, tm, tiles_per_batch):
    m = x2.shape[0]
    dff = wg.shape[1]
    tn = 512
    full_mod = lambda a: _mod_spec(a, tm, D_MODEL, tiles_per_batch)
    return pl.pallas_call(
        _ffn1_kernel,
        grid=(m // tm, dff // tn),
        in_specs=[pl.BlockSpec((tm, D_MODEL), lambda i, j: (i, 0)),
                  full_mod(sc), full_mod(sh),
                  pl.BlockSpec((1, D_MODEL), lambda i, j: (0, 0)),
                  pl.BlockSpec((D_MODEL, tn), lambda i, j: (0, j)),
                  pl.BlockSpec((D_MODEL, tn), lambda i, j: (0, j))],
        out_specs=pl.BlockSpec((tm, tn), lambda i, j: (i, j)),
        out_shape=jax.ShapeDtypeStruct((m, dff), BF16),
        scratch_shapes=[pltpu.VMEM((tm, D_MODEL), BF16)],
        name="ffn_gate_up",
        compiler_params=_cparams(("arbitrary", "arbitrary"), 44),
    )(x2, sc, sh, g, wg, wu)


def _ffn2_kernel(a_ref, wd_ref, x_ref, gt_ref, y_ref):
    y_ref[...] = x_ref[...] + gt_ref[...] * _dot(a_ref[...], wd_ref[...])


def _ffn2(a, wd, x2, gt, tm, tiles_per_batch):
    m, dff = a.shape
    tn = 256
    return pl.pallas_call(
        _ffn2_kernel,
        grid=(m // tm, D_MODEL // tn),
        in_specs=[pl.BlockSpec((tm, dff), lambda i, j: (i, 0)),
                  pl.BlockSpec((dff, tn), lambda i, j: (0, j)),
                  pl.BlockSpec((tm, tn), lambda i, j: (i, j)),
                  _mod_spec(gt, tm, tn, tiles_per_batch)],
        out_specs=pl.BlockSpec((tm, tn), lambda i, j: (i, j)),
        out_shape=jax.ShapeDtypeStruct((m, D_MODEL), F32),
        name="ffn_down",
        compiler_params=_cparams(("arbitrary", "arbitrary"), 48),
    )(a, wd, x2, gt)


def _final_norm_kernel(x_ref, g_ref, y_ref, *, rows_chunk=64):
    g = g_ref[...]

    def body(r, c):
        r0 = pl.multiple_of(r * rows_chunk, rows_chunk)
        x = x_ref[pl.ds(r0, rows_chunk), :]
        ms = jnp.mean(x * x, axis=-1, keepdims=True)
        y_ref[pl.ds(r0, rows_chunk), :] = x * lax.rsqrt(ms + EPS) * g
        return c

    lax.fori_loop(0, x_ref.shape[0] // rows_chunk, body, 0)


def _final_norm(x2, g, tm):
    m = x2.shape[0]
    return pl.pallas_call(
        _final_norm_kernel,
        grid=(m // tm,),
        in_specs=[pl.BlockSpec((tm, D_MODEL), lambda i: (i, 0)),
                  pl.BlockSpec((1, D_MODEL), lambda i: (0, 0))],
        out_specs=pl.BlockSpec((tm, D_MODEL), lambda i: (i, 0)),
        out_shape=jax.ShapeDtypeStruct((m, D_MODEL), F32),
        name="final_norm",
        compiler_params=_cparams(("arbitrary",), 40),
    )(x2, g.reshape(1, D_MODEL))


def _pad_time(a, total):
    pad = total - a.shape[1]
    if pad == 0:
        return a
    return jnp.pad(a, ((0, 0), (0, pad)) + ((0, 0),) * (a.ndim - 2))


def _layer(x2, batch, mods, wts, bias, caches, tm, prompt):
    m = x2.shape[0]
    t = m // batch
    sh1, sc1, gt1, sh2, sc2, gt2 = mods
    tiles_per_batch = max(t // tm, 1)
    q, ka, kb, kc, va, vb, vc, lf = _inproj(x2, sc1, sh1, wts["g_attn"], wts["w_qkv"], wts["w_f"], wts["b_f"],
                                            tm, tiles_per_batch)
    r3 = lambda a: a.reshape(batch, t, a.shape[-1])
    q, ka, kb, kc, va, vb, vc = map(r3, (q, ka, kb, kc, va, vb, vc))
    logf = r3(lf)[:, :, :H_B]

    if prompt:
        keep = min(A_LEFT, t)
        state = (ka[:, t - keep:], va[:, t - keep:], kb, vb, logf, kc, vc)
        f = _cumsum_time(jnp.swapaxes(logf, 1, 2))
        oa = _band_attn(q, ka, va, bias, CHUNK, A_BAND - CHUNK, A_BAND - CHUNK - A_LEFT)
        blk = 256
        ob = _causal_attn2("fox", q, kb, vb, f, H_A, blk, blk, 0, 4)
        oc = _causal_attn2("stick", q, kc, vc, None, H_A + H_B, blk, blk, 0, 4)
    else:
        cak, cav, cbk, cbv, cblogf, cck, ccv = caches
        state = (ka, va, kb, vb, logf, kc, vc)
        flat = lambda c: c.reshape(c.shape[0], c.shape[1], -1)
        past = cbk.shape[1]
        tk = 256
        tkv = past + tk
        cat = lambda c, n: _pad_time(jnp.concatenate([flat(c), n], axis=1), tkv)
        kka = jnp.concatenate([flat(cak), ka], axis=1)
        vva = jnp.concatenate([flat(cav), va], axis=1)
        wa = kka.shape[1]
        oa = _band_attn(q, kka, vva, bias, t, A_BAND - wa, A_BAND - wa)
        lcat = jnp.concatenate([cblogf.astype(F32), logf], axis=1)
        fpad = -(-tkv // (8 * LANES)) * (8 * LANES)
        f = _cumsum_time(jnp.swapaxes(_pad_time(lcat, fpad), 1, 2))[:, :, :tkv]
        qpad = _pad_time(q, LANES)
        ng = tkv // tk
        ob = _causal_attn2("fox", qpad, cat(cbk, kb), cat(cbv, vb), f, H_A, LANES, tk, past, ng)[:, :t]
        oc = _causal_attn2("stick", qpad, cat(cck, kc), cat(ccv, vc), None, H_A + H_B, LANES, tk, past, ng)[:, :t]

    r2 = lambda a: a.reshape(m, a.shape[-1])
    x2 = _outproj(r2(oa), r2(ob), r2(oc), wts["wo_a"], wts["wo_b"], wts["wo_c"], x2, gt1, tm, tiles_per_batch)
    a = _ffn1(x2, sc2, sh2, wts["g_ffn"], wts["w_gate"], wts["w_up"], tm, tiles_per_batch)
    x2 = _ffn2(a, wts["w_down"], x2, gt2, tm, tiles_per_batch)
    return x2, state


def kernel(x_prompt, x_sample, cache_a_k, cache_a_v, cache_b_k, cache_b_v, cache_b_logf, cache_c_k, cache_c_v,
           c_prompt, c_sample, w_ada, b_ada, g_attn, g_ffn, w_in, b_f, rel_bias, w_o, w_gate, w_up, w_down, g_final):
    depth = w_ada.shape[0]
    bp, tp, _ = x_prompt.shape
    bs, ts, _ = x_sample.shape
    n_qkv = 3 * D_MODEL

    rows = -(-(bp + bs) // 8) * 8
    c_all = jnp.concatenate([c_prompt, c_sample, jnp.zeros((rows - bp - bs, D_MODEL), F32)], axis=0)
    mod = _ada(c_all, w_ada, b_ada)

    bias_p = _relbias(rel_bias, CHUNK, A_BAND)
    bias_s = _relbias(rel_bias, ts, A_BAND)

    xp = x_prompt.reshape(bp * tp, D_MODEL)
    xs = x_sample.reshape(bs * ts, D_MODEL)
    tm_p, tm_s = 1024, bs * ts
    states_p, states_s = [], []
    ea = H_A * HEAD_DIM
    eb = (H_A + H_B) * HEAD_DIM
    for l in range(depth):
        wo = w_o[l].astype(BF16)
        wts = dict(
            g_attn=g_attn[l].reshape(1, D_MODEL), g_ffn=g_ffn[l].reshape(1, D_MODEL),
            w_qkv=w_in[l, :, :n_qkv].astype(BF16),
            w_f=jnp.pad(w_in[l, :, n_qkv:], ((0, 0), (0, LANES - H_B))).astype(BF16),
            b_f=jnp.pad(b_f[l], (0, LANES - H_B)).reshape(1, LANES),
            wo_a=wo[:ea], wo_b=wo[ea:eb], wo_c=wo[eb:],
            w_gate=w_gate[l].astype(BF16), w_up=w_up[l].astype(BF16), w_down=w_down[l].astype(BF16))
        chunks = jnp.split(mod[l], 6, axis=-1)
        mods_p = [c[:bp].reshape(bp, 1, D_MODEL) for c in chunks]
        mods_s = [jnp.repeat(c[bp:bp + bs], ts, axis=0).reshape(1, bs * ts, D_MODEL) for c in chunks]
        xp, st_p = _layer(xp, bp, mods_p, wts, bias_p[l], None, tm_p, True)
        caches = (cache_a_k[l], cache_a_v[l], cache_b_k[l], cache_b_v[l], cache_b_logf[l],
                  cache_c_k[l], cache_c_v[l])
        xs, st_s = _layer(xs, bs, mods_s, wts, bias_s[l], caches, tm_s, False)
        states_p.append(st_p)
        states_s.append(st_s)

    def stack(states, heads):
        outs = []
        for idx, arrs in enumerate(zip(*states)):
            a = jnp.stack(arrs)
            if idx != 4:
                a = a.reshape(a.shape[:3] + (heads[idx], HEAD_DIM))
            outs.append(a)
        return outs

    heads = (H_A, H_A, H_B, H_B, None, H_C, H_C)
    y_prompt = _final_norm(xp, g_final, 512).reshape(bp, tp, D_MODEL)
    y_sample = _final_norm(xs, g_final, bs * ts).reshape(bs, ts, D_MODEL)
    return tuple([y_prompt, y_sample] + stack(states_p, heads) + stack(states_s, heads))
```

```python
import functools

import jax
import jax.numpy as jnp
from jax import lax
from jax.experimental import pallas as pl
from jax.experimental.pallas import tpu as pltpu

F32 = jnp.float32
BF16 = jnp.bfloat16

D_MODEL = 2048
HEAD_DIM = 128
N_HEADS = D_MODEL // HEAD_DIM
H_A = N_HEADS // 4
H_B = (N_HEADS - H_A) // 2
H_C = N_HEADS - H_A - H_B
CHUNK = 64
A_LEFT = 8 * CHUNK
REL_CLIP = 128
N_REL = 2 * REL_CLIP + 1
EPS = 1e-6
ATTN_SCALE = HEAD_DIM ** -0.5
LOG2E = 1.4426950408889634
Q_SCALE = ATTN_SCALE * LOG2E

V7X_VMEM_BYTES = 64 * 1024 * 1024
LANES = 128
HEAD_PAIR = 2 * HEAD_DIM
A_BAND = 640


def _cparams(sem, vmem_mb):
    return pltpu.CompilerParams(dimension_semantics=sem,
                                vmem_limit_bytes=min(vmem_mb * 1024 * 1024, V7X_VMEM_BYTES - (4 << 20)))


def _dot(a, b):
    return jnp.dot(a, b, preferred_element_type=F32)


def _dot_nt(a, b):
    return lax.dot_general(a, b, (((1,), (1,)), ((), ())), preferred_element_type=F32)


def _sigmoid(x):
    return 1.0 / (1.0 + jnp.exp(-x))


def _log_sigmoid(x):
    return jnp.minimum(x, 0.0) - jnp.log1p(jnp.exp(-jnp.abs(x)))


def _modulated_norm(x, g, sc, sh):
    ms = jnp.mean(x * x, axis=-1, keepdims=True)
    y = x * lax.rsqrt(ms + EPS) * g
    return y * (1.0 + sc) + sh


def _norm_rows(x_ref, g_ref, sc_ref, sh_ref, h_ref, rows_chunk=64):
    tm = x_ref.shape[0]
    g = g_ref[...]
    per_row = sc_ref.shape[0] != 1

    def body(r, c):
        r0 = pl.multiple_of(r * rows_chunk, rows_chunk)
        sc = sc_ref[pl.ds(r0, rows_chunk), :] if per_row else sc_ref[...]
        sh = sh_ref[pl.ds(r0, rows_chunk), :] if per_row else sh_ref[...]
        h = _modulated_norm(x_ref[pl.ds(r0, rows_chunk), :], g, sc, sh)
        h_ref[pl.ds(r0, rows_chunk), :] = h.astype(h_ref.dtype)
        return c

    lax.fori_loop(0, tm // rows_chunk, body, 0)


def _ada_kernel(c_ref, w_ref, b_ref, o_ref):
    c = c_ref[...]
    a = (c * _sigmoid(c)).astype(BF16)
    o_ref[...] = _dot(a, w_ref[...].astype(BF16)) + b_ref[...]


def _ada(c_all, w_ada, b_ada):
    depth, _, n = w_ada.shape
    rows = c_all.shape[0]
    tn = 1024
    return pl.pallas_call(
        _ada_kernel,
        grid=(depth, n // tn),
        in_specs=[pl.BlockSpec((rows, D_MODEL), lambda l, j: (0, 0)),
                  pl.BlockSpec((None, D_MODEL, tn), lambda l, j: (l, 0, j)),
                  pl.BlockSpec((None, 1, tn), lambda l, j: (l, 0, j))],
        out_specs=pl.BlockSpec((None, rows, tn), lambda l, j: (l, 0, j)),
        out_shape=jax.ShapeDtypeStruct((depth, rows, n), F32),
        name="ada_mod",
        compiler_params=_cparams(("arbitrary", "arbitrary"), 40),
    )(c_all, w_ada, b_ada.reshape(depth, 1, n))


def _relbias_kernel(tab_ref, o_ref, *, cq, band):
    l = pl.program_id(0)
    h = pl.program_id(1)
    qi = lax.broadcasted_iota(jnp.int32, (cq, band), 0)
    s = lax.broadcasted_iota(jnp.int32, (cq, band), 1)
    idx = jnp.clip(qi + (band - cq) - s, -REL_CLIP, REL_CLIP) + REL_CLIP
    base = l * (N_REL * H_A) + h

    def body(r, acc):
        return jnp.where(idx == r, tab_ref[base + r * H_A], acc)

    o_ref[...] = lax.fori_loop(0, N_REL, body, jnp.zeros((cq, band), F32)) * LOG2E


def _relbias(rel_bias, cq, band):
    depth = rel_bias.shape[0]
    return pl.pallas_call(
        functools.partial(_relbias_kernel, cq=cq, band=band),
        grid=(depth, H_A),
        in_specs=[pl.BlockSpec(memory_space=pltpu.SMEM)],
        out_specs=pl.BlockSpec((None, None, cq, band), lambda l, h: (l, h, 0, 0)),
        out_shape=jax.ShapeDtypeStruct((depth, H_A, cq, band), F32),
        name="rel_bias",
    )(rel_bias.reshape(-1))


def _inproj_kernel(x_ref, sc_ref, sh_ref, g_ref, wq_ref, wk_ref, wv_ref, wf_ref, bf_ref,
                   q_ref, ka_ref, kb_ref, kc_ref, va_ref, vb_ref, vc_ref, lf_ref, h_scr):
    j = pl.program_id(1)

    @pl.when(j == 0)
    def _():
        _norm_rows(x_ref, g_ref, sc_ref, sh_ref, h_scr)
        lf_ref[...] = _log_sigmoid(_dot(h_scr[...], wf_ref[...]) + bf_ref[...])

    h = h_scr[...]
    pa = H_A // 2
    pb = (H_A + H_B) // 2
    q_scale = jnp.where(j >= pb, -Q_SCALE, Q_SCALE)
    q_ref[...] = (_dot(h, wq_ref[...]) * q_scale).astype(BF16)
    kt = _dot(h, wk_ref[...])
    vt = _dot(h, wv_ref[...])

    @pl.when(j < pa)
    def _():
        ka_ref[...] = kt
        va_ref[...] = vt

    @pl.when((j >= pa) & (j < pb))
    def _():
        kb_ref[...] = kt
        vb_ref[...] = vt

    @pl.when(j >= pb)
    def _():
        kc_ref[...] = kt
        vc_ref[...] = vt


def _mod_spec(mod, tm, tn, tiles_per_batch):
    r = mod.shape[1]
    col = (lambda j: 0) if tn == mod.shape[2] else (lambda j: j)
    if r == 1:
        return pl.BlockSpec((None, 1, tn), lambda i, j: (i // tiles_per_batch, 0, col(j)))
    return pl.BlockSpec((None, tm, tn), lambda i, j: (0, i, col(j)))


def _inproj(x2, sc, sh, g, w_qkv, w_f, b_f, tm, tiles_per_batch):
    m = x2.shape[0]
    tn = HEAD_PAIR
    npair = N_HEADS // 2
    pa, pb = H_A // 2, (H_A + H_B) // 2

    def clamp(lo, n):
        return lambda i, j: (i, jnp.clip(j - lo, 0, n - 1))

    def full_mod(a):
        return _mod_spec(a, tm, D_MODEL, tiles_per_batch)

    out_shape = [jax.ShapeDtypeStruct((m, D_MODEL), BF16)]
    out_specs = [pl.BlockSpec((tm, tn), lambda i, j: (i, j))]
    for _ in range(2):
        for lo, n in ((0, pa), (pa, pb - pa), (pb, npair - pb)):
            out_shape.append(jax.ShapeDtypeStruct((m, n * tn), F32))
            out_specs.append(pl.BlockSpec((tm, tn), clamp(lo, n)))
    out_shape.append(jax.ShapeDtypeStruct((m, LANES), F32))
    out_specs.append(pl.BlockSpec((tm, LANES), lambda i, j: (i, 0)))
    outs = pl.pallas_call(
        _inproj_kernel,
        grid=(m // tm, npair),
        in_specs=[pl.BlockSpec((tm, D_MODEL), lambda i, j: (i, 0)),
                  full_mod(sc), full_mod(sh),
                  pl.BlockSpec((1, D_MODEL), lambda i, j: (0, 0)),
                  pl.BlockSpec((D_MODEL, tn), lambda i, j: (0, j)),
                  pl.BlockSpec((D_MODEL, tn), lambda i, j: (0, npair + j)),
                  pl.BlockSpec((D_MODEL, tn), lambda i, j: (0, 2 * npair + j)),
                  pl.BlockSpec((D_MODEL, LANES), lambda i, j: (0, 0)),
                  pl.BlockSpec((1, LANES), lambda i, j: (0, 0))],
        out_specs=out_specs,
        out_shape=out_shape,
        scratch_shapes=[pltpu.VMEM((tm, D_MODEL), BF16)],
        name="in_proj",
        compiler_params=_cparams(("arbitrary", "arbitrary"), 52),
    )(x2, sc, sh, g, w_qkv, w_qkv, w_qkv, w_f, b_f)
    q, ka, kb, kc, va, vb, vc, lf = outs
    return q, ka, kb, kc, va, vb, vc, lf


def _split3(x):
    hi = x.astype(BF16)
    r1 = x - hi.astype(F32)
    mid = r1.astype(BF16)
    lo = (r1 - mid.astype(F32)).astype(BF16)
    return hi, mid, lo


def _dot_f32_by_01(x, ones01):
    hi, mid, lo = _split3(x)
    return _dot(hi, ones01) + _dot(mid, ones01) + _dot(lo, ones01)


def _cumsum_kernel(x_ref, o_ref):
    x = x_ref[...]
    rows = x.shape[0]
    i0 = lax.broadcasted_iota(jnp.int32, (LANES, LANES), 0)
    i1 = lax.broadcasted_iota(jnp.int32, (LANES, LANES), 1)
    upper = (i0 <= i1).astype(BF16)
    c = _dot_f32_by_01(x, upper)
    tot = jnp.broadcast_to(c[:, LANES - 1:LANES], (rows, LANES))
    r0 = lax.broadcasted_iota(jnp.int32, (rows, rows), 0)
    r1 = lax.broadcasted_iota(jnp.int32, (rows, rows), 1)
    lower = (r1 < r0).astype(BF16)
    hi, mid, lo = _split3(tot)
    off = _dot(lower, hi) + _dot(lower, mid) + _dot(lower, lo)
    o_ref[...] = (c + off) * LOG2E


def _cumsum_time(logf_hm):
    b, h, t = logf_hm.shape
    rows = t // LANES
    out = pl.pallas_call(
        _cumsum_kernel,
        grid=(b * h,),
        in_specs=[pl.BlockSpec((None, rows, LANES), lambda i: (i, 0, 0))],
        out_specs=pl.BlockSpec((None, rows, LANES), lambda i: (i, 0, 0)),
        out_shape=jax.ShapeDtypeStruct((b * h, rows, LANES), F32),
        name="forget_cumsum",
    )(logf_hm.reshape(b * h, rows, LANES))
    return out.reshape(b, h, t)


def _band_kernel(q_ref, k_ref, v_ref, bias_ref, o_ref, kp_scr, vp_scr, *, cq, npad, nmask, nchunks):
    tk = k_ref.shape[0]
    kp_scr[0:npad, :] = jnp.zeros((npad, HEAD_DIM), BF16)
    vp_scr[0:npad, :] = jnp.zeros((npad, HEAD_DIM), BF16)
    _copy_cast(k_ref, kp_scr, tk, dst_off=npad)
    _copy_cast(v_ref, vp_scr, tk, dst_off=npad)
    bias = bias_ref[...]
    slot = lax.broadcasted_iota(jnp.int32, (cq, A_BAND), 1)

    def chunk(n, carry):
        r0 = pl.multiple_of(n * cq, cq)
        q = q_ref[pl.ds(r0, cq), :]
        k = kp_scr[pl.ds(r0, A_BAND), :]
        v = vp_scr[pl.ds(r0, A_BAND), :]
        s = _dot_nt(q, k) + bias
        first_valid = jnp.maximum(nmask, npad - n * cq)
        s = jnp.where(slot >= first_valid, s, -jnp.inf)
        m = jnp.max(s, axis=-1, keepdims=True)
        p = jnp.exp2(s - m)
        l = jnp.sum(p, axis=-1, keepdims=True)
        o = _dot(p.astype(BF16), v) / l
        o_ref[pl.ds(r0, cq), :] = o.astype(o_ref.dtype)
        return carry

    lax.fori_loop(0, nchunks, chunk, 0, unroll=min(nchunks, 4))


def _band_attn(q, k, v, bias, cq, npad, nmask):
    b, t, _ = q.shape
    tk = k.shape[1]
    kern = functools.partial(_band_kernel, cq=cq, npad=npad, nmask=nmask, nchunks=t // cq)
    return pl.pallas_call(
        kern,
        grid=(b, H_A),
        in_specs=[pl.BlockSpec((None, t, HEAD_DIM), lambda i, h: (i, 0, h)),
                  pl.BlockSpec((None, tk, HEAD_DIM), lambda i, h: (i, 0, h)),
                  pl.BlockSpec((None, tk, HEAD_DIM), lambda i, h: (i, 0, h)),
                  pl.BlockSpec((None, cq, A_BAND), lambda i, h: (h, 0, 0))],
        out_specs=pl.BlockSpec((None, t, HEAD_DIM), lambda i, h: (i, 0, h)),
        out_shape=jax.ShapeDtypeStruct((b, t, H_A * HEAD_DIM), BF16),
        scratch_shapes=[pltpu.VMEM((npad + tk, HEAD_DIM), BF16),
                        pltpu.VMEM((npad + tk, HEAD_DIM), BF16)],
        name="band_attn",
        compiler_params=_cparams(("arbitrary", "arbitrary"), 48),
    )(q, k, v, bias)


def _copy_cast(src_ref, dst_ref, rows, dst_off=0, step=256):
    n = rows // step

    def body(i, c):
        r = pl.multiple_of(i * step, step)
        d = pl.multiple_of(dst_off + i * step, 16)
        dst_ref[pl.ds(d, step), :] = src_ref[pl.ds(r, step), :].astype(dst_ref.dtype)
        return c

    lax.fori_loop(0, n, body, 0)
    if rows % step:
        dst_ref[dst_off + n * step:dst_off + rows, :] = src_ref[n * step:rows, :].astype(dst_ref.dtype)


def _row_to_col(row):
    n = row.shape[1]
    cols = []
    for c in range(n // LANES):
        blk = jnp.broadcast_to(row[:, c * LANES:(c + 1) * LANES], (LANES, LANES))
        cols.append(blk.T[:, 0:1])
    return cols[0] if len(cols) == 1 else jnp.concatenate(cols, axis=0)


def _fox_kernel(q_ref, k_ref, v_ref, f_ref, o_ref, k_scr, v_scr, acc_scr, m_scr, l_scr,
                *, tq, tkd, tk, nq, qoff):
    tkv = k_ref.shape[0]
    _copy_cast(k_ref, k_scr, tkv)
    _copy_cast(v_ref, v_scr, tkv)
    row = lax.broadcasted_iota(jnp.int32, (tq, tkd), 0)
    col = lax.broadcasted_iota(jnp.int32, (tq, tkd), 1)
    causal = col <= row
    tqp = max(tq, LANES)

    def block(q, fq, kb, width, mask):
        k0 = pl.multiple_of(kb * tk, tk)
        k = k_scr[pl.ds(k0, width), :]
        v = v_scr[pl.ds(k0, width), :]
        fk = f_ref[kb][:, 0:width]
        s = _dot_nt(q, k) * ATTN_SCALE + fq - fk
        if mask is not None:
            s = jnp.where(mask, s, -jnp.inf)
        m_old = m_scr[...]
        m_new = jnp.maximum(m_old, jnp.max(s, axis=-1, keepdims=True))
        alpha = jnp.exp(m_old - m_new)
        p = jnp.exp(s - m_new)
        l_scr[...] = alpha * l_scr[...] + jnp.sum(p, axis=-1, keepdims=True)
        acc_scr[...] = alpha * acc_scr[...] + _dot(p.astype(BF16), v)
        m_scr[...] = m_new

    def qblock(i, carry):
        q0 = pl.multiple_of(i * tq, tq)
        pb = (qoff + i * tq) // tk
        q = q_ref[pl.ds(q0, tq), :]
        fq = _row_to_col(f_ref[pb][:, 0:tqp])[0:tq]
        m_scr[...] = jnp.full((tq, 1), -jnp.inf, F32)
        l_scr[...] = jnp.zeros((tq, 1), F32)
        acc_scr[...] = jnp.zeros((tq, HEAD_DIM), F32)
        block(q, fq, pb, tkd, causal)

        def full(jj, c):
            block(q, fq, jj, tk, None)
            return c

        lax.fori_loop(0, pb, full, 0)
        o_ref[pl.ds(q0, tq), :] = (acc_scr[...] / l_scr[...]).astype(o_ref.dtype)
        return carry

    lax.fori_loop(0, nq, qblock, 0)


def _stick_kernel(q_ref, k_ref, v_ref, o_ref, k_scr, v_scr, acc_scr, a_scr, msd_scr, msf_scr,
                  *, tq, tkd, tk, nq, qoff):
    tkv = k_ref.shape[0]
    _copy_cast(k_ref, k_scr, tkv)
    _copy_cast(v_ref, v_scr, tkv)
    for ref, w in ((msd_scr, tkd), (msf_scr, tk)):
        j0 = lax.broadcasted_iota(jnp.int32, (w, w), 0)
        j1 = lax.broadcasted_iota(jnp.int32, (w, w), 1)
        ref[...] = (j0 > j1).astype(BF16)
    row = lax.broadcasted_iota(jnp.int32, (tq, tkd), 0)
    col = lax.broadcasted_iota(jnp.int32, (tq, tkd), 1)
    causal = col < row

    def block(q, k0, width, mask, ms_ref):
        k = k_scr[pl.ds(k0, width), :]
        v = v_scr[pl.ds(k0, width), :]
        z = _dot_nt(q, k) * ATTN_SCALE
        lsg = _log_sigmoid(z)
        stay = lsg - z
        if mask is not None:
            stay = jnp.where(mask, stay, 0.0)
        hi = stay.astype(BF16)
        lo = (stay - hi.astype(F32)).astype(BF16)
        ms = ms_ref[...]
        within = _dot(hi, ms) + _dot(lo, ms)
        a_old = a_scr[...]
        w = jnp.exp(lsg + within + a_old)
        if mask is not None:
            w = jnp.where(mask, w, 0.0)
        acc_scr[...] += _dot(w.astype(BF16), v)
        a_scr[...] = a_old + within[:, 0:1] + stay[:, 0:1]

    def qblock(i, carry):
        q0 = pl.multiple_of(i * tq, tq)
        p0 = pl.multiple_of(qoff + i * tq, tq)
        q = q_ref[pl.ds(q0, tq), :]
        a_scr[...] = jnp.zeros((tq, 1), F32)
        acc_scr[...] = jnp.zeros((tq, HEAD_DIM), F32)
        block(q, p0, tkd, causal, msd_scr)
        nfull = p0 // tk

        def full(jj, c):
            block(q, pl.multiple_of((nfull - 1 - jj) * tk, tk), tk, None, msf_scr)
            return c

        lax.fori_loop(0, nfull, full, 0)
        o_ref[pl.ds(q0, tq), :] = acc_scr[...].astype(o_ref.dtype)
        return carry

    lax.fori_loop(0, nq, qblock, 0)


def _causal_attn(kind, q, k, v, f, head0, tq, tkd, tk, qoff):
    b, t, _ = q.shape
    tkv = k.shape[1]
    nh = k.shape[2] // HEAD_DIM
    assert qoff % tk == 0 and (tq % tk == 0 or t == tq) and tkd <= tk and tq <= tkd
    common = dict(tq=tq, tkd=tkd, tk=tk, nq=t // tq, qoff=qoff)
    kv_spec = pl.BlockSpec((None, tkv, HEAD_DIM), lambda i, h: (i, 0, h))
    in_specs = [pl.BlockSpec((None, t, HEAD_DIM), lambda i, h: (i, 0, head0 + h)), kv_spec, kv_spec]
    scratch = [pltpu.VMEM((tkv, HEAD_DIM), BF16), pltpu.VMEM((tkv, HEAD_DIM), BF16),
               pltpu.VMEM((tq, HEAD_DIM), F32)]
    if kind == "fox":
        kern = functools.partial(_fox_kernel, **common)
        nblk = f.shape[2] // tk
        f = f.reshape(b, nh, nblk, 1, tk)
        in_specs.append(pl.BlockSpec((None, None, nblk, 1, tk), lambda i, h: (i, h, 0, 0, 0)))
        scratch += [pltpu.VMEM((tq, 1), F32), pltpu.VMEM((tq, 1), F32)]
        args = (q, k, v, f)
    else:
        kern = functools.partial(_stick_kernel, **common)
        scratch += [pltpu.VMEM((tq, 1), F32), pltpu.VMEM((tkd, tkd), BF16), pltpu.VMEM((tk, tk), BF16)]
        args = (q, k, v)
    return pl.pallas_call(
        kern,
        grid=(b, nh),
        in_specs=in_specs,
        out_specs=pl.BlockSpec((None, t, HEAD_DIM), lambda i, h: (i, 0, h)),
        out_shape=jax.ShapeDtypeStruct((b, t, nh * HEAD_DIM), BF16),
        scratch_shapes=scratch,
        name=kind + "_attn",
        compiler_params=_cparams(("arbitrary", "arbitrary"), 52),
    )(*args)


SUBLANES = 8
VT_EXTRA = 16


def _col_replicate(row):
    n = row.shape[1]
    parts = [jnp.broadcast_to(row[:, c * LANES:(c + 1) * LANES], (LANES, LANES)).T for c in range(n // LANES)]
    return parts[0] if len(parts) == 1 else jnp.concatenate(parts, axis=0)


def _aug_columns(col_rep, first):
    hi, mid, lo = _split3(col_rep)
    lane = lax.broadcasted_iota(jnp.int32, col_rep.shape, 1)
    base = 0 if first else 3
    one = jnp.where((lane >= 3 - base) & (lane < 6 - base), 1.0, 0.0)
    x = jnp.where(lane == base, hi.astype(F32),
                  jnp.where(lane == base + 1, mid.astype(F32),
                            jnp.where(lane == base + 2, lo.astype(F32), one)))
    return x.astype(BF16)


def _fox2_kernel(q_ref, k_ref, v_ref, g_ref, o_ref, kp_scr, vt_scr, qp_scr, acc_scr, m_scr,
                 *, tq, tk, nq, qoff, ng):
    nblk = kp_scr.shape[0]
    ones_rows = (lax.broadcasted_iota(jnp.int32, (VT_EXTRA, tk), 0) == 0).astype(BF16)

    def prep(kb, c):
        r0 = pl.multiple_of(kb * tk, tk)
        kp_scr[kb, :, 0:HEAD_DIM] = k_ref[pl.ds(r0, tk), :].astype(BF16)
        kp_scr[kb, :, HEAD_DIM:2 * HEAD_DIM] = _aug_columns(_col_replicate(-g_ref[kb]), True)
        vt_scr[kb, 0:HEAD_DIM, :] = v_ref[pl.ds(r0, tk), :].T.astype(BF16)
        vt_scr[kb, HEAD_DIM:HEAD_DIM + VT_EXTRA, :] = ones_rows
        return c

    lax.fori_loop(0, nblk, prep, 0)
    off = qoff % tk
    row = lax.broadcasted_iota(jnp.int32, (tk, tq), 0)
    col = lax.broadcasted_iota(jnp.int32, (tk, tq), 1)

    def group(g, p0):
        qp = qp_scr[...]
        xs = []
        for u in range(ng):
            x = _dot_nt(kp_scr[g * ng + u], qp)
            if p0 is not None:
                x = jnp.where(row + ((g * ng + u) * tk - p0) <= col, x, -jnp.inf)
            xs.append(x)
        m_blk = jnp.max(xs[0], axis=0, keepdims=True)
        for x in xs[1:]:
            m_blk = jnp.maximum(m_blk, jnp.max(x, axis=0, keepdims=True))
        m_old = m_scr[...]
        m_new = jnp.maximum(m_old, m_blk)
        acc = jnp.exp2(m_old - m_new) * acc_scr[...]
        for u in range(ng):
            p = jnp.exp2(xs[u] - m_new)
            acc = acc + _dot(vt_scr[g * ng + u], p.astype(BF16))
        acc_scr[...] = acc
        m_scr[...] = m_new

    def qblock(i, carry):
        q0 = pl.multiple_of(i * tq, tq)
        p0 = qoff + i * tq
        pb = p0 // tk
        qp_scr[:, 0:HEAD_DIM] = q_ref[pl.ds(q0, tq), :]
        if tq <= tk:
            gq = g_ref[pb][:, off:off + tq]
        else:
            gq = jnp.concatenate([g_ref[pb + c] for c in range(tq // tk)], axis=1)
        qp_scr[:, HEAD_DIM:2 * HEAD_DIM] = _aug_columns(_col_replicate(gq), False)
        m_scr[...] = jnp.full((1, tq), -jnp.inf, F32)
        acc_scr[...] = jnp.zeros((HEAD_DIM + VT_EXTRA, tq), F32)
        gd = pb // ng
        group(gd, p0)

        def full(g, c):
            group(g, None)
            return c

        lax.fori_loop(0, gd, full, 0)
        denom = acc_scr[HEAD_DIM:HEAD_DIM + 1, :]
        o_ref[pl.ds(q0, tq), :] = (acc_scr[0:HEAD_DIM, :] / denom).T.astype(o_ref.dtype)
        return carry

    lax.fori_loop(0, nq, qblock, 0)


def _stick2_kernel(q_ref, k_ref, v_ref, o_ref, kp_scr, vt_scr, acc_scr, *, tq, tk, nq, qoff, ng):
    nblk = kp_scr.shape[0]
    nv = tk // SUBLANES

    r = lax.broadcasted_iota(jnp.int32, (tk, tk), 0)
    kk = lax.broadcasted_iota(jnp.int32, (tk, tk), 1)
    perm = ((r % SUBLANES) * nv + r // SUBLANES == kk).astype(BF16)
    perm_t = ((kk % SUBLANES) * nv + kk // SUBLANES == r).astype(BF16)

    def prep(kb, c):
        r0 = pl.multiple_of(kb * tk, tk)
        kp_scr[kb] = _dot(perm, k_ref[pl.ds(r0, tk), :].astype(BF16)).astype(BF16)
        vt_scr[kb] = _dot(v_ref[pl.ds(r0, tk), :].T.astype(BF16), perm_t).astype(BF16)
        return c

    lax.fori_loop(0, nblk, prep, 0)
    row = lax.broadcasted_iota(jnp.int32, (tk, tq), 0)
    col = lax.broadcasted_iota(jnp.int32, (tk, tq), 1)
    key_in_block = (row % SUBLANES) * nv + row // SUBLANES
    sub = lax.broadcasted_iota(jnp.int32, (SUBLANES, tq), 0)

    def tile_terms(q, kb, p0):
        x = _dot_nt(kp_scr[kb], q)
        beta = 1.0 / (1.0 + jnp.exp2(x))
        stay = 1.0 - beta
        valid = None
        if p0 is not None:
            valid = key_in_block + (kb * tk - p0) < col
            stay = jnp.where(valid, stay, 1.0)
        after = [None] * nv
        run = jnp.ones((SUBLANES, tq), F32)
        for v in range(nv - 1, -1, -1):
            after[v] = run
            run = run * stay[v * SUBLANES:(v + 1) * SUBLANES, :]
        incl = run
        for k in (1, 2, 4):
            incl = incl * jnp.where(sub + k < SUBLANES, pltpu.roll(incl, SUBLANES - k, axis=0), 1.0)
        later = jnp.where(sub + 1 < SUBLANES, pltpu.roll(incl, SUBLANES - 1, axis=0), 1.0)
        return after, later, incl[0:1, :], beta, valid

    def group(q, g, carry, p0):
        terms = [tile_terms(q, g * ng + u, p0) for u in range(ng)]
        acc = acc_scr[...]
        for u in range(ng - 1, -1, -1):
            after, later, total, beta, valid = terms[u]
            scale_u = later * carry
            w = jnp.concatenate([after[v] * scale_u for v in range(nv)], axis=0) * beta
            if valid is not None:
                w = jnp.where(valid, w, 0.0)
            acc = acc + _dot(vt_scr[g * ng + u], w.astype(BF16))
            carry = carry * total
        acc_scr[...] = acc
        return carry

    def qblock(i, c0):
        q0 = pl.multiple_of(i * tq, tq)
        p0 = qoff + i * tq
        gd = (p0 // tk) // ng
        q = q_ref[pl.ds(q0, tq), :]
        acc_scr[...] = jnp.zeros((HEAD_DIM, tq), F32)
        carry = group(q, gd, jnp.ones((1, tq), F32), p0)
        lax.fori_loop(0, gd, lambda jj, c: group(q, gd - 1 - jj, c, None), carry)
        o_ref[pl.ds(q0, tq), :] = acc_scr[...].T.astype(o_ref.dtype)
        return c0

    lax.fori_loop(0, nq, qblock, 0)


def _causal_attn2(kind, q, k, v, f, head0, tq, tk, qoff, ng):
    b, t, _ = q.shape
    tkv = k.shape[1]
    nh = k.shape[2] // HEAD_DIM
    nblk = tkv // tk
    assert tkv % tk == 0 and nblk % ng == 0 and tq % LANES == 0 and t % tq == 0
    assert (ng * tk) % tq == 0 and qoff % tq == 0 and (tq % tk == 0 or (qoff % tk) + tq <= tk and t == tq)
    common = dict(tq=tq, tk=tk, nq=t // tq, qoff=qoff, ng=ng)
    kv_spec = pl.BlockSpec((None, tkv, HEAD_DIM), lambda i, h: (i, 0, h))
    in_specs = [pl.BlockSpec((None, t, HEAD_DIM), lambda i, h: (i, 0, head0 + h)), kv_spec, kv_spec]
    if kind == "fox":
        kern = functools.partial(_fox2_kernel, **common)
        g = f.reshape(b, nh, nblk, 1, tk)
        in_specs.append(pl.BlockSpec((None, None, nblk, 1, tk), lambda i, h: (i, h, 0, 0, 0)))
        scratch = [pltpu.VMEM((nblk, tk, 2 * HEAD_DIM), BF16), pltpu.VMEM((nblk, HEAD_DIM + VT_EXTRA, tk), BF16),
                   pltpu.VMEM((tq, 2 * HEAD_DIM), BF16), pltpu.VMEM((HEAD_DIM + VT_EXTRA, tq), F32),
                   pltpu.VMEM((1, tq), F32)]
        args = (q, k, v, g)
    else:
        kern = functools.partial(_stick2_kernel, **common)
        scratch = [pltpu.VMEM((nblk, tk, HEAD_DIM), BF16), pltpu.VMEM((nblk, HEAD_DIM, tk), BF16),
                   pltpu.VMEM((HEAD_DIM, tq), F32)]
        args = (q, k, v)
    return pl.pallas_call(
        kern,
        grid=(b, nh),
        in_specs=in_specs,
        out_specs=pl.BlockSpec((None, t, HEAD_DIM), lambda i, h: (i, 0, h)),
        out_shape=jax.ShapeDtypeStruct((b, t, nh * HEAD_DIM), BF16),
        scratch_shapes=scratch,
        name=kind + "_attn",
        compiler_params=_cparams(("arbitrary", "arbitrary"), 52),
    )(*args)


def _outproj_kernel(oa_ref, ob_ref, oc_ref, wa_ref, wb_ref, wc_ref, x_ref, gt_ref, y_ref):
    acc = _dot(oa_ref[...], wa_ref[...]) + _dot(ob_ref[...], wb_ref[...]) + _dot(oc_ref[...], wc_ref[...])
    y_ref[...] = x_ref[...] + gt_ref[...] * acc


def _outproj(oa, ob, oc, wa, wb, wc, x2, gt, tm, tiles_per_batch):
    m = x2.shape[0]
    tn = 512
    row = lambda a: pl.BlockSpec((tm, a.shape[1]), lambda i, j: (i, 0))
    wsp = lambda a: pl.BlockSpec((a.shape[0], tn), lambda i, j: (0, j))
    return pl.pallas_call(
        _outproj_kernel,
        grid=(m // tm, D_MODEL // tn),
        in_specs=[row(oa), row(ob), row(oc), wsp(wa), wsp(wb), wsp(wc),
                  pl.BlockSpec((tm, tn), lambda i, j: (i, j)),
                  _mod_spec(gt, tm, tn, tiles_per_batch)],
        out_specs=pl.BlockSpec((tm, tn), lambda i, j: (i, j)),
        out_shape=jax.ShapeDtypeStruct((m, D_MODEL), F32),
        name="out_proj",
        compiler_params=_cparams(("arbitrary", "arbitrary"), 40),
    )(oa, ob, oc, wa, wb, wc, x2, gt)


def _ffn1_kernel(x_ref, sc_ref, sh_ref, g_ref, wg_ref, wu_ref, a_ref, h_scr):
    @pl.when(pl.program_id(1) == 0)
    def _():
        _norm_rows(x_ref, g_ref, sc_ref, sh_ref, h_scr)

    h = h_scr[...]
    gate = _dot(h, wg_ref[...])
    up = _dot(h, wu_ref[...])
    a_ref[...] = (gate * _sigmoid(gate) * up).astype(BF16)


def _ffn1(x2, sc, sh, g, wg, wu, tm, tiles_per_batch):
    m = x2.shape[0]
    dff = wg.shape[1]
    tn = 512
    full_mod = lambda a: _mod_spec(a, tm, D_MODEL, tiles_per_batch)
    return pl.pallas_call(
        _ffn1_kernel,
        grid=(m // tm, dff // tn),
        in_specs=[pl.BlockSpec((tm, D_MODEL), lambda i, j: (i, 0)),
                  full_mod(sc), full_mod(sh),
                  pl.BlockSpec((1, D_MODEL), lambda i, j: (0, 0)),
                  pl.BlockSpec((D_MODEL, tn), lambda i, j: (0, j)),
                  pl.BlockSpec((D_MODEL, tn), lambda i, j: (0, j))],
        out_specs=pl.BlockSpec((tm, tn), lambda i, j: (i, j)),
        out_shape=jax.ShapeDtypeStruct((m, dff), BF16),
        scratch_shapes=[pltpu.VMEM((tm, D_MODEL), BF16)],
        name="ffn_gate_up",
        compiler_params=_cparams(("arbitrary", "arbitrary"), 44),
    )(x2, sc, sh, g, wg, wu)


def _ffn2_kernel(a_ref, wd_ref, x_ref, gt_ref, y_ref):
    y_ref[...] = x_ref[...] + gt_ref[...] * _dot(a_ref[...], wd_ref[...])


def _ffn2(a, wd, x2, gt, tm, tiles_per_batch):
    m, dff = a.shape
    tn = 256
    return pl.pallas_call(
        _ffn2_kernel,
        grid=(m // tm, D_MODEL // tn),
        in_specs=[pl.BlockSpec((tm, dff), lambda i, j: (i, 0)),
                  pl.BlockSpec((dff, tn), lambda i, j: (0, j)),
                  pl.BlockSpec((tm, tn), lambda i, j: (i, j)),
                  _mod_spec(gt, tm, tn, tiles_per_batch)],
        out_specs=pl.BlockSpec((tm, tn), lambda i, j: (i, j)),
        out_shape=jax.ShapeDtypeStruct((m, D_MODEL), F32),
        name="ffn_down",
        compiler_params=_cparams(("arbitrary", "arbitrary"), 48),
    )(a, wd, x2, gt)


def _final_norm_kernel(x_ref, g_ref, y_ref, *, rows_chunk=64):
    g = g_ref[...]

    def body(r, c):
        r0 = pl.multiple_of(r * rows_chunk, rows_chunk)
        x = x_ref[pl.ds(r0, rows_chunk), :]
        ms = jnp.mean(x * x, axis=-1, keepdims=True)
        y_ref[pl.ds(r0, rows_chunk), :] = x * lax.rsqrt(ms + EPS) * g
        return c

    lax.fori_loop(0, x_ref.shape[0] // rows_chunk, body, 0)


def _final_norm(x2, g, tm):
    m = x2.shape[0]
    return pl.pallas_call(
        _final_norm_kernel,
        grid=(m // tm,),
        in_specs=[pl.BlockSpec((tm, D_MODEL), lambda i: (i, 0)),
                  pl.BlockSpec((1, D_MODEL), lambda i: (0, 0))],
        out_specs=pl.BlockSpec((tm, D_MODEL), lambda i: (i, 0)),
        out_shape=jax.ShapeDtypeStruct((m, D_MODEL), F32),
        name="final_norm",
        compiler_params=_cparams(("arbitrary",), 40),
    )(x2, g.reshape(1, D_MODEL))


def _pad_time(a, total):
    pad = total - a.shape[1]
    if pad == 0:
        return a
    return jnp.pad(a, ((0, 0), (0, pad)) + ((0, 0),) * (a.ndim - 2))


def _layer(x2, batch, mods, wts, bias, caches, tm, prompt):
    m = x2.shape[0]
    t = m // batch
    sh1, sc1, gt1, sh2, sc2, gt2 = mods
    tiles_per_batch = max(t // tm, 1)
    q, ka, kb, kc, va, vb, vc, lf = _inproj(x2, sc1, sh1, wts["g_attn"], wts["w_qkv"], wts["w_f"], wts["b_f"],
                                            tm, tiles_per_batch)
    r3 = lambda a: a.reshape(batch, t, a.shape[-1])
    q, ka, kb, kc, va, vb, vc = map(r3, (q, ka, kb, kc, va, vb, vc))
    logf = r3(lf)[:, :, :H_B]

    if prompt:
        keep = min(A_LEFT, t)
        state = (ka[:, t - keep:], va[:, t - keep:], kb, vb, logf, kc, vc)
        f = _cumsum_time(jnp.swapaxes(logf, 1, 2))
        oa = _band_attn(q, ka, va, bias, CHUNK, A_BAND - CHUNK, A_BAND - CHUNK - A_LEFT)
        blk = 256
        ob = _causal_attn2("fox", q, kb, vb, f, H_A, 4 * blk, blk, 0, 4)
        oc = _causal_attn2("stick", q, kc, vc, None, H_A + H_B, 4 * blk, blk, 0, 4)
    else:
        cak, cav, cbk, cbv, cblogf, cck, ccv = caches
        state = (ka, va, kb, vb, logf, kc, vc)
        flat = lambda c: c.reshape(c.shape[0], c.shape[1], -1)
        past = cbk.shape[1]
        tk = 256
        tkv = past + tk
        cat = lambda c, n: _pad_time(jnp.concatenate([flat(c), n], axis=1), tkv)
        kka = jnp.concatenate([flat(cak), ka], axis=1)
        vva = jnp.concatenate([flat(cav), va], axis=1)
        wa = kka.shape[1]
        oa = _band_attn(q, kka, vva, bias, t, A_BAND - wa, A_BAND - wa)
        lcat = jnp.concatenate([cblogf.astype(F32), logf], axis=1)
        fpad = -(-tkv // (8 * LANES)) * (8 * LANES)
        f = _cumsum_time(jnp.swapaxes(_pad_time(lcat, fpad), 1, 2))[:, :, :tkv]
        qpad = _pad_time(q, LANES)
        ng = tkv // tk
        ob = _causal_attn2("fox", qpad, cat(cbk, kb), cat(cbv, vb), f, H_A, LANES, tk, past, ng)[:, :t]
        oc = _causal_attn2("stick", qpad, cat(cck, kc), cat(ccv, vc), None, H_A + H_B, LANES, tk, past, ng)[:, :t]

    r2 = lambda a: a.reshape(m, a.shape[-1])
    x2 = _outproj(r2(oa), r2(ob), r2(oc), wts["wo_a"], wts["wo_b"], wts["wo_c"], x2, gt1, tm, tiles_per_batch)
    a = _ffn1(x2, sc2, sh2, wts["g_ffn"], wts["w_gate"], wts["w_up"], tm, tiles_per_batch)
    x2 = _ffn2(a, wts["w_down"], x2, gt2, tm, tiles_per_batch)
    return x2, state


def kernel(x_prompt, x_sample, cache_a_k, cache_a_v, cache_b_k, cache_b_v, cache_b_logf, cache_c_k, cache_c_v,
           c_prompt, c_sample, w_ada, b_ada, g_attn, g_ffn, w_in, b_f, rel_bias, w_o, w_gate, w_up, w_down, g_final):
    depth = w_ada.shape[0]
    bp, tp, _ = x_prompt.shape
    bs, ts, _ = x_sample.shape
    n_qkv = 3 * D_MODEL

    rows = -(-(bp + bs) // 8) * 8
    c_all = jnp.concatenate([c_prompt, c_sample, jnp.zeros((rows - bp - bs, D_MODEL), F32)], axis=0)
    mod = _ada(c_all, w_ada, b_ada)

    bias_p = _relbias(rel_bias, CHUNK, A_BAND)
    bias_s = _relbias(rel_bias, ts, A_BAND)

    xp = x_prompt.reshape(bp * tp, D_MODEL)
    xs = x_sample.reshape(bs * ts, D_MODEL)
    tm_p, tm_s = 1024, bs * ts
    states_p, states_s = [], []
    ea = H_A * HEAD_DIM
    eb = (H_A + H_B) * HEAD_DIM
    for l in range(depth):
        wo = w_o[l].astype(BF16)
        wts = dict(
            g_attn=g_attn[l].reshape(1, D_MODEL), g_ffn=g_ffn[l].reshape(1, D_MODEL),
            w_qkv=w_in[l, :, :n_qkv].astype(BF16),
            w_f=jnp.pad(w_in[l, :, n_qkv:], ((0, 0), (0, LANES - H_B))).astype(BF16),
            b_f=jnp.pad(b_f[l], (0, LANES - H_B)).reshape(1, LANES),
            wo_a=wo[:ea], wo_b=wo[ea:eb], wo_c=wo[eb:],
            w_gate=w_gate[l].astype(BF16), w_up=w_up[l].astype(BF16), w_down=w_down[l].astype(BF16))
        chunks = jnp.split(mod[l], 6, axis=-1)
        mods_p = [c[:bp].reshape(bp, 1, D_MODEL) for c in chunks]
        mods_s = [jnp.repeat(c[bp:bp + bs], ts, axis=0).reshape(1, bs * ts, D_MODEL) for c in chunks]
        xp, st_p = _layer(xp, bp, mods_p, wts, bias_p[l], None, tm_p, True)
        caches = (cache_a_k[l], cache_a_v[l], cache_b_k[l], cache_b_v[l], cache_b_logf[l],
                  cache_c_k[l], cache_c_v[l])
        xs, st_s = _layer(xs, bs, mods_s, wts, bias_s[l], caches, tm_s, False)
        states_p.append(st_p)
        states_s.append(st_s)

    def stack(states, heads):
        outs = []
        for idx, arrs in enumerate(zip(*states)):
            a = jnp.stack(arrs)
            if idx != 4:
                a = a.reshape(a.shape[:3] + (heads[idx], HEAD_DIM))
            outs.append(a)
        return outs

    heads = (H_A, H_A, H_B, H_B, None, H_C, H_C)
    y_prompt = _final_norm(xp, g_final, 512).reshape(bp, tp, D_MODEL)
    y_sample = _final_norm(xs, g_final, bs * ts).reshape(bs, ts, D_MODEL)
    return tuple([y_prompt, y_sample] + stack(states_p, heads) + stack(states_s, heads))
```

```python
import functools

import jax
import jax.numpy as jnp
from jax import lax
from jax.experimental import pallas as pl
from jax.experimental.pallas import tpu as pltpu

F32 = jnp.float32
BF16 = jnp.bfloat16

D_MODEL = 2048
HEAD_DIM = 128
N_HEADS = D_MODEL // HEAD_DIM
H_A = N_HEADS // 4
H_B = (N_HEADS - H_A) // 2
H_C = N_HEADS - H_A - H_B
CHUNK = 64
A_LEFT = 8 * CHUNK
REL_CLIP = 128
N_REL = 2 * REL_CLIP + 1
EPS = 1e-6
ATTN_SCALE = HEAD_DIM ** -0.5
LOG2E = 1.4426950408889634
Q_SCALE = ATTN_SCALE * LOG2E

V7X_VMEM_BYTES = 64 * 1024 * 1024
LANES = 128
SUBLANES = 8
HEAD_PAIR = 2 * HEAD_DIM
A_BAND = 640
VT_EXTRA = 16

PAIRS_A = H_A // 2
PAIRS_AB = (H_A + H_B) // 2
N_PAIRS = N_HEADS // 2


def _cparams(sem, vmem_mb):
    return pltpu.CompilerParams(dimension_semantics=sem,
                                vmem_limit_bytes=min(vmem_mb * 1024 * 1024, V7X_VMEM_BYTES - (4 << 20)))


def _dot(a, b):
    return jnp.dot(a, b, preferred_element_type=F32)


def _dot_nt(a, b):
    return lax.dot_general(a, b, (((1,), (1,)), ((), ())), preferred_element_type=F32)


def _sigmoid(x):
    return 1.0 / (1.0 + jnp.exp(-x))


def _log_sigmoid(x):
    return jnp.minimum(x, 0.0) - jnp.log1p(jnp.exp(-jnp.abs(x)))


def _modulated_norm(x, g, sc, sh):
    ms = jnp.mean(x * x, axis=-1, keepdims=True)
    y = x * lax.rsqrt(ms + EPS) * g
    return y * (1.0 + sc) + sh


def _norm_rows(x_ref, g_ref, sc_ref, sh_ref, h_ref, rows_chunk=64):
    tm = x_ref.shape[0]
    g = g_ref[...]
    per_row = sc_ref.shape[0] != 1

    def body(r, c):
        r0 = pl.multiple_of(r * rows_chunk, rows_chunk)
        sc = sc_ref[pl.ds(r0, rows_chunk), :] if per_row else sc_ref[...]
        sh = sh_ref[pl.ds(r0, rows_chunk), :] if per_row else sh_ref[...]
        h = _modulated_norm(x_ref[pl.ds(r0, rows_chunk), :], g, sc, sh)
        h_ref[pl.ds(r0, rows_chunk), :] = h.astype(h_ref.dtype)
        return c

    lax.fori_loop(0, tm // rows_chunk, body, 0)


def _copy_cast(src_ref, dst_ref, rows, dst_off=0, step=256):
    n = rows // step

    def body(i, c):
        r = pl.multiple_of(i * step, step)
        d = pl.multiple_of(dst_off + i * step, 16)
        dst_ref[pl.ds(d, step), :] = src_ref[pl.ds(r, step), :].astype(dst_ref.dtype)
        return c

    lax.fori_loop(0, n, body, 0)
    if rows % step:
        dst_ref[dst_off + n * step:dst_off + rows, :] = src_ref[n * step:rows, :].astype(dst_ref.dtype)


def _split3(x):
    hi = x.astype(BF16)
    r1 = x - hi.astype(F32)
    mid = r1.astype(BF16)
    lo = (r1 - mid.astype(F32)).astype(BF16)
    return hi, mid, lo


def _ada_kernel(c_ref, w_ref, b_ref, o_ref):
    c = c_ref[...]
    a = (c * _sigmoid(c)).astype(BF16)
    o_ref[...] = _dot(a, w_ref[...].astype(BF16)) + b_ref[...]


def _ada(c_all, w_ada, b_ada):
    depth, _, n = w_ada.shape
    rows = c_all.shape[0]
    tn = 1024
    return pl.pallas_call(
        _ada_kernel,
        grid=(depth, n // tn),
        in_specs=[pl.BlockSpec((rows, D_MODEL), lambda l, j: (0, 0)),
                  pl.BlockSpec((None, D_MODEL, tn), lambda l, j: (l, 0, j)),
                  pl.BlockSpec((None, 1, tn), lambda l, j: (l, 0, j))],
        out_specs=pl.BlockSpec((None, rows, tn), lambda l, j: (l, 0, j)),
        out_shape=jax.ShapeDtypeStruct((depth, rows, n), F32),
        name="ada_mod",
        compiler_params=_cparams(("arbitrary", "arbitrary"), 40),
    )(c_all, w_ada, b_ada.reshape(depth, 1, n))


def _relbias_kernel(tab_ref, o_ref, *, cq, band):
    l = pl.program_id(0)
    h = pl.program_id(1)
    qi = lax.broadcasted_iota(jnp.int32, (cq, band), 0)
    s = lax.broadcasted_iota(jnp.int32, (cq, band), 1)
    idx = jnp.clip(qi + (band - cq) - s, -REL_CLIP, REL_CLIP) + REL_CLIP
    base = l * (N_REL * H_A) + h

    def body(r, acc):
        return jnp.where(idx == r, tab_ref[base + r * H_A], acc)

    o_ref[...] = lax.fori_loop(0, N_REL, body, jnp.zeros((cq, band), F32)) * LOG2E


def _relbias(rel_bias, cq, band):
    depth = rel_bias.shape[0]
    return pl.pallas_call(
        functools.partial(_relbias_kernel, cq=cq, band=band),
        grid=(depth, H_A),
        in_specs=[pl.BlockSpec(memory_space=pltpu.SMEM)],
        out_specs=pl.BlockSpec((None, None, cq, band), lambda l, h: (l, h, 0, 0)),
        out_shape=jax.ShapeDtypeStruct((depth, H_A, cq, band), F32),
        name="rel_bias",
    )(rel_bias.reshape(-1))


def _inproj_kernel(*refs, n_prev, with_state):
    x_ref, sc_ref, sh_ref, g_ref, wq_ref, wk_ref, wv_ref, wf_ref, bf_ref = refs[:9]
    outs = refs[9 + n_prev:-1]
    h_scr = refs[-1]
    q_ref, ka_ref, va_ref, kb_ref, vb_ref, kc_ref, vc_ref, lf_ref = outs[:8]
    j = pl.program_id(1)

    @pl.when(j == 0)
    def _():
        _norm_rows(x_ref, g_ref, sc_ref, sh_ref, h_scr)
        lf_ref[...] = _log_sigmoid(_dot(h_scr[...], wf_ref[...]) + bf_ref[...])

    h = h_scr[...]
    q_scale = jnp.where(j >= PAIRS_AB, -Q_SCALE, Q_SCALE)
    q_ref[...] = (_dot(h, wq_ref[...]) * q_scale).astype(BF16)
    kt = _dot(h, wk_ref[...])
    vt = _dot(h, wv_ref[...])

    @pl.when(j < PAIRS_A)
    def _():
        ka_ref[...] = kt
        va_ref[...] = vt

    @pl.when((j >= PAIRS_A) & (j < PAIRS_AB))
    def _():
        kb_ref[...] = kt.astype(kb_ref.dtype)
        vb_ref[...] = vt.astype(vb_ref.dtype)

    @pl.when(j >= PAIRS_AB)
    def _():
        kc_ref[...] = kt.astype(kc_ref.dtype)
        vc_ref[...] = vt.astype(vc_ref.dtype)

    if with_state:
        kbs_ref, vbs_ref, kcs_ref, vcs_ref = outs[8:12]
        for pair in range(PAIRS_A, N_PAIRS):
            in_b = pair < PAIRS_AB
            ks_ref, vs_ref = (kbs_ref, vbs_ref) if in_b else (kcs_ref, vcs_ref)
            head0 = 2 * (pair - (PAIRS_A if in_b else PAIRS_AB))

            @pl.when(j == pair)
            def _(ks_ref=ks_ref, vs_ref=vs_ref, head0=head0):
                for c in range(2):
                    ks_ref[:, head0 + c, :] = kt[:, c * HEAD_DIM:(c + 1) * HEAD_DIM]
                    vs_ref[:, head0 + c, :] = vt[:, c * HEAD_DIM:(c + 1) * HEAD_DIM]


def _mod_spec(mod, tm, tn, tiles_per_batch):
    r = mod.shape[1]
    col = (lambda j: 0) if tn == mod.shape[2] else (lambda j: j)
    if r == 1:
        return pl.BlockSpec((None, 1, tn), lambda i, j: (i // tiles_per_batch, 0, col(j)))
    return pl.BlockSpec((None, tm, tn), lambda i, j: (0, i, col(j)))


def _inproj(x2, sc, sh, g, w_qkv, w_f, b_f, tm, tiles_per_batch, state=None):
    m = x2.shape[0]
    tn = HEAD_PAIR

    def clamp(lo, n):
        return lambda i, j: (i, jnp.clip(j - lo, 0, n - 1))

    def full_mod(a):
        return _mod_spec(a, tm, D_MODEL, tiles_per_batch)

    kv_dtype = F32 if state is None else BF16
    groups = ((0, PAIRS_A, F32), (PAIRS_A, PAIRS_AB - PAIRS_A, kv_dtype), (PAIRS_AB, N_PAIRS - PAIRS_AB, kv_dtype))
    out_shape = [jax.ShapeDtypeStruct((m, D_MODEL), BF16)]
    out_specs = [pl.BlockSpec((tm, tn), lambda i, j: (i, j))]
    for lo, n, dt in groups:
        for _ in range(2):
            out_shape.append(jax.ShapeDtypeStruct((m, n * tn), dt))
            out_specs.append(pl.BlockSpec((tm, tn), clamp(lo, n)))
    out_shape.append(jax.ShapeDtypeStruct((m, LANES), F32))
    out_specs.append(pl.BlockSpec((tm, LANES), lambda i, j: (i, 0)))
    in_specs = [pl.BlockSpec((tm, D_MODEL), lambda i, j: (i, 0)),
                full_mod(sc), full_mod(sh),
                pl.BlockSpec((1, D_MODEL), lambda i, j: (0, 0)),
                pl.BlockSpec((D_MODEL, tn), lambda i, j: (0, j)),
                pl.BlockSpec((D_MODEL, tn), lambda i, j: (0, N_PAIRS + j)),
                pl.BlockSpec((D_MODEL, tn), lambda i, j: (0, 2 * N_PAIRS + j)),
                pl.BlockSpec((D_MODEL, LANES), lambda i, j: (0, 0)),
                pl.BlockSpec((1, LANES), lambda i, j: (0, 0))]
    args = [x2, sc, sh, g, w_qkv, w_qkv, w_qkv, w_f, b_f]
    n_in, n_plain_out = len(args), len(out_shape)
    aliases = {}
    n_prev = 0
    if state is not None:
        layer, depth, batch, prev = state
        t = m // batch
        for idx, heads in enumerate((H_B, H_B, H_C, H_C)):
            out_shape.append(jax.ShapeDtypeStruct((depth, batch, t, heads, HEAD_DIM), F32))
            out_specs.append(pl.BlockSpec(
                (None, None, tm, heads, HEAD_DIM),
                lambda i, j: (layer, i // tiles_per_batch, i % tiles_per_batch, 0, 0)))
            if prev is not None:
                in_specs.append(pl.BlockSpec(memory_space=pl.ANY))
                args.append(prev[idx])
                aliases[n_in + idx] = n_plain_out + idx
                n_prev += 1
    outs = pl.pallas_call(
        functools.partial(_inproj_kernel, n_prev=n_prev, with_state=state is not None),
        grid=(m // tm, N_PAIRS),
        in_specs=in_specs,
        out_specs=out_specs,
        out_shape=out_shape,
        input_output_aliases=aliases,
        scratch_shapes=[pltpu.VMEM((tm, D_MODEL), BF16)],
        name="in_proj",
        compiler_params=_cparams(("arbitrary", "arbitrary"), 52),
    )(*args)
    q, ka, va, kb, vb, kc, vc, lf = outs[:8]
    return q, ka, kb, kc, va, vb, vc, lf, tuple(outs[8:])


def _dot_f32_by_01(x, ones01):
    hi, mid, lo = _split3(x)
    return _dot(hi, ones01) + _dot(mid, ones01) + _dot(lo, ones01)


def _cumsum_kernel(x_ref, o_ref):
    x = x_ref[...]
    rows = x.shape[0]
    i0 = lax.broadcasted_iota(jnp.int32, (LANES, LANES), 0)
    i1 = lax.broadcasted_iota(jnp.int32, (LANES, LANES), 1)
    upper = (i0 <= i1).astype(BF16)
    c = _dot_f32_by_01(x, upper)
    tot = jnp.broadcast_to(c[:, LANES - 1:LANES], (rows, LANES))
    r0 = lax.broadcasted_iota(jnp.int32, (rows, rows), 0)
    r1 = lax.broadcasted_iota(jnp.int32, (rows, rows), 1)
    lower = (r1 < r0).astype(BF16)
    hi, mid, lo = _split3(tot)
    off = _dot(lower, hi) + _dot(lower, mid) + _dot(lower, lo)
    o_ref[...] = (c + off) * LOG2E


def _cumsum_time(logf_hm):
    b, h, t = logf_hm.shape
    rows = t // LANES
    out = pl.pallas_call(
        _cumsum_kernel,
        grid=(b * h,),
        in_specs=[pl.BlockSpec((None, rows, LANES), lambda i: (i, 0, 0))],
        out_specs=pl.BlockSpec((None, rows, LANES), lambda i: (i, 0, 0)),
        out_shape=jax.ShapeDtypeStruct((b * h, rows, LANES), F32),
        name="forget_cumsum",
    )(logf_hm.reshape(b * h, rows, LANES))
    return out.reshape(b, h, t)


def _band_kernel(q_ref, k_ref, v_ref, bias_ref, o_ref, kp_scr, vp_scr, *, cq, npad, nmask, nchunks):
    tk = k_ref.shape[0]
    kp_scr[0:npad, :] = jnp.zeros((npad, HEAD_DIM), BF16)
    vp_scr[0:npad, :] = jnp.zeros((npad, HEAD_DIM), BF16)
    _copy_cast(k_ref, kp_scr, tk, dst_off=npad)
    _copy_cast(v_ref, vp_scr, tk, dst_off=npad)
    bias = bias_ref[...]
    slot = lax.broadcasted_iota(jnp.int32, (cq, A_BAND), 1)
    per_step = min(nchunks, 8)

    def step(i, carry):
        rows = [pl.multiple_of((i * per_step + u) * cq, cq) for u in range(per_step)]
        scores = [_dot_nt(q_ref[pl.ds(r0, cq), :], kp_scr[pl.ds(r0, A_BAND), :]) for r0 in rows]
        probs, denoms = [], []
        for u in range(per_step):
            first_valid = jnp.maximum(nmask, npad - (i * per_step + u) * cq)
            s = jnp.where(slot >= first_valid, scores[u] + bias, -jnp.inf)
            p = jnp.exp2(s - jnp.max(s, axis=-1, keepdims=True))
            denoms.append(jnp.sum(p, axis=-1, keepdims=True))
            probs.append(p.astype(BF16))
        for u in range(per_step):
            o = _dot(probs[u], vp_scr[pl.ds(rows[u], A_BAND), :]) / denoms[u]
            o_ref[pl.ds(rows[u], cq), :] = o.astype(o_ref.dtype)
        return carry

    lax.fori_loop(0, nchunks // per_step, step, 0)


def _band_attn(q, k, v, bias, cq, npad, nmask):
    b, t, _ = q.shape
    tk = k.shape[1]
    kern = functools.partial(_band_kernel, cq=cq, npad=npad, nmask=nmask, nchunks=t // cq)
    return pl.pallas_call(
        kern,
        grid=(b, H_A),
        in_specs=[pl.BlockSpec((None, t, HEAD_DIM), lambda i, h: (i, 0, h)),
                  pl.BlockSpec((None, tk, HEAD_DIM), lambda i, h: (i, 0, h)),
                  pl.BlockSpec((None, tk, HEAD_DIM), lambda i, h: (i, 0, h)),
                  pl.BlockSpec((None, cq, A_BAND), lambda i, h: (h, 0, 0))],
        out_specs=pl.BlockSpec((None, t, HEAD_DIM), lambda i, h: (i, 0, h)),
        out_shape=jax.ShapeDtypeStruct((b, t, H_A * HEAD_DIM), BF16),
        scratch_shapes=[pltpu.VMEM((npad + tk, HEAD_DIM), BF16),
                        pltpu.VMEM((npad + tk, HEAD_DIM), BF16)],
        name="band_attn",
        compiler_params=_cparams(("arbitrary", "arbitrary"), 48),
    )(q, k, v, bias)


def _col_replicate(row):
    n = row.shape[1]
    parts = [jnp.broadcast_to(row[:, c * LANES:(c + 1) * LANES], (LANES, LANES)).T for c in range(n // LANES)]
    return parts[0] if len(parts) == 1 else jnp.concatenate(parts, axis=0)


def _aug_columns(col_rep, first):
    hi, mid, lo = _split3(col_rep)
    lane = lax.broadcasted_iota(jnp.int32, col_rep.shape, 1)
    base = 0 if first else 3
    one = jnp.where((lane >= 3 - base) & (lane < 6 - base), 1.0, 0.0)
    x = jnp.where(lane == base, hi.astype(F32),
                  jnp.where(lane == base + 1, mid.astype(F32),
                            jnp.where(lane == base + 2, lo.astype(F32), one)))
    return x.astype(BF16)


def _fox_kernel(q_ref, k_ref, v_ref, g_ref, o_ref, kp_scr, vt_scr, qp_scr, acc_scr, m_scr,
                *, tq, tk, nq, qoff, ng):
    nblk = kp_scr.shape[0]
    ones_rows = (lax.broadcasted_iota(jnp.int32, (VT_EXTRA, tk), 0) == 0).astype(BF16)

    def prep(kb, c):
        r0 = pl.multiple_of(kb * tk, tk)
        kp_scr[kb, :, 0:HEAD_DIM] = k_ref[pl.ds(r0, tk), :].astype(BF16)
        kp_scr[kb, :, HEAD_DIM:2 * HEAD_DIM] = _aug_columns(_col_replicate(-g_ref[kb]), True)
        vt_scr[kb, 0:HEAD_DIM, :] = v_ref[pl.ds(r0, tk), :].astype(F32).T.astype(BF16)
        vt_scr[kb, HEAD_DIM:HEAD_DIM + VT_EXTRA, :] = ones_rows
        return c

    lax.fori_loop(0, nblk, prep, 0)
    off = qoff % tk
    row = lax.broadcasted_iota(jnp.int32, (tk, tq), 0)
    col = lax.broadcasted_iota(jnp.int32, (tk, tq), 1)

    def group(g, p0):
        qp = qp_scr[...]
        xs = []
        for u in range(ng):
            x = _dot_nt(kp_scr[g * ng + u], qp)
            if p0 is not None:
                x = jnp.where(row + ((g * ng + u) * tk - p0) <= col, x, -jnp.inf)
            xs.append(x)
        m_blk = jnp.max(xs[0], axis=0, keepdims=True)
        for x in xs[1:]:
            m_blk = jnp.maximum(m_blk, jnp.max(x, axis=0, keepdims=True))
        m_old = m_scr[...]
        m_new = jnp.maximum(m_old, m_blk)
        acc = jnp.exp2(m_old - m_new) * acc_scr[...]
        for u in range(ng):
            p = jnp.exp2(xs[u] - m_new)
            acc = acc + _dot(vt_scr[g * ng + u], p.astype(BF16))
        acc_scr[...] = acc
        m_scr[...] = m_new

    def qblock(i, carry):
        q0 = pl.multiple_of(i * tq, tq)
        p0 = qoff + i * tq
        pb = p0 // tk
        qp_scr[:, 0:HEAD_DIM] = q_ref[pl.ds(q0, tq), :]
        if tq <= tk:
            gq = g_ref[pb][:, off:off + tq]
        else:
            gq = jnp.concatenate([g_ref[pb + c] for c in range(tq // tk)], axis=1)
        qp_scr[:, HEAD_DIM:2 * HEAD_DIM] = _aug_columns(_col_replicate(gq), False)
        m_scr[...] = jnp.full((1, tq), -jnp.inf, F32)
        acc_scr[...] = jnp.zeros((HEAD_DIM + VT_EXTRA, tq), F32)
        gd = pb // ng
        group(gd, p0)

        def full(g, c):
            group(g, None)
            return c

        lax.fori_loop(0, gd, full, 0)
        denom = acc_scr[HEAD_DIM:HEAD_DIM + 1, :]
        o_ref[pl.ds(q0, tq), :] = (acc_scr[0:HEAD_DIM, :] / denom).T.astype(o_ref.dtype)
        return carry

    lax.fori_loop(0, nq, qblock, 0)


def _stick_kernel(q_ref, k_ref, v_ref, o_ref, kp_scr, vt_scr, acc_scr, *, tq, tk, nq, qoff, ng):
    nblk = kp_scr.shape[0]
    nv = tk // SUBLANES

    r = lax.broadcasted_iota(jnp.int32, (tk, tk), 0)
    kk = lax.broadcasted_iota(jnp.int32, (tk, tk), 1)
    perm = ((r % SUBLANES) * nv + r // SUBLANES == kk).astype(BF16)
    perm_t = ((kk % SUBLANES) * nv + kk // SUBLANES == r).astype(BF16)

    def prep(kb, c):
        r0 = pl.multiple_of(kb * tk, tk)
        kp_scr[kb] = _dot(perm, k_ref[pl.ds(r0, tk), :].astype(BF16)).astype(BF16)
        vt_scr[kb] = _dot(v_ref[pl.ds(r0, tk), :].astype(F32).T.astype(BF16), perm_t).astype(BF16)
        return c

    lax.fori_loop(0, nblk, prep, 0)
    row = lax.broadcasted_iota(jnp.int32, (tk, tq), 0)
    col = lax.broadcasted_iota(jnp.int32, (tk, tq), 1)
    key_in_block = (row % SUBLANES) * nv + row // SUBLANES
    sub = lax.broadcasted_iota(jnp.int32, (SUBLANES, tq), 0)

    def tile_terms(q, kb, p0):
        x = _dot_nt(kp_scr[kb], q)
        beta = 1.0 / (1.0 + jnp.exp2(x))
        stay = 1.0 - beta
        valid = None
        if p0 is not None:
            valid = key_in_block + (kb * tk - p0) < col
            stay = jnp.where(valid, stay, 1.0)
        after = [None] * nv
        run = jnp.ones((SUBLANES, tq), F32)
        for v in range(nv - 1, -1, -1):
            after[v] = run
            run = run * stay[v * SUBLANES:(v + 1) * SUBLANES, :]
        incl = run
        for k in (1, 2, 4):
            incl = incl * jnp.where(sub + k < SUBLANES, pltpu.roll(incl, SUBLANES - k, axis=0), 1.0)
        later = jnp.where(sub + 1 < SUBLANES, pltpu.roll(incl, SUBLANES - 1, axis=0), 1.0)
        return after, later, incl[0:1, :], beta, valid

    def group(q, g, carry, p0):
        terms = [tile_terms(q, g * ng + u, p0) for u in range(ng)]
        acc = acc_scr[...]
        for u in range(ng - 1, -1, -1):
            after, later, total, beta, valid = terms[u]
            scale_u = later * carry
            w = jnp.concatenate([after[v] * scale_u for v in range(nv)], axis=0) * beta
            if valid is not None:
                w = jnp.where(valid, w, 0.0)
            acc = acc + _dot(vt_scr[g * ng + u], w.astype(BF16))
            carry = carry * total
        acc_scr[...] = acc
        return carry

    def qblock(i, c0):
        q0 = pl.multiple_of(i * tq, tq)
        p0 = qoff + i * tq
        gd = (p0 // tk) // ng
        q = q_ref[pl.ds(q0, tq), :]
        acc_scr[...] = jnp.zeros((HEAD_DIM, tq), F32)
        carry = group(q, gd, jnp.ones((1, tq), F32), p0)
        lax.fori_loop(0, gd, lambda jj, c: group(q, gd - 1 - jj, c, None), carry)
        o_ref[pl.ds(q0, tq), :] = acc_scr[...].T.astype(o_ref.dtype)
        return c0

    lax.fori_loop(0, nq, qblock, 0)


def _causal_attn(kind, q, k, v, f, head0, tq, tk, qoff, ng):
    b, t, _ = q.shape
    tkv = k.shape[1]
    nh = k.shape[2] // HEAD_DIM
    nblk = tkv // tk
    assert tkv % tk == 0 and nblk % ng == 0 and tq % LANES == 0 and t % tq == 0
    assert (ng * tk) % tq == 0 and qoff % tq == 0 and (tq % tk == 0 or (qoff % tk) + tq <= tk and t == tq)
    common = dict(tq=tq, tk=tk, nq=t // tq, qoff=qoff, ng=ng)
    kv_spec = pl.BlockSpec((None, tkv, HEAD_DIM), lambda i, h: (i, 0, h))
    in_specs = [pl.BlockSpec((None, t, HEAD_DIM), lambda i, h: (i, 0, head0 + h)), kv_spec, kv_spec]
    if kind == "fox":
        kern = functools.partial(_fox_kernel, **common)
        g = f.reshape(b, nh, nblk, 1, tk)
        in_specs.append(pl.BlockSpec((None, None, nblk, 1, tk), lambda i, h: (i, h, 0, 0, 0)))
        scratch = [pltpu.VMEM((nblk, tk, 2 * HEAD_DIM), BF16), pltpu.VMEM((nblk, HEAD_DIM + VT_EXTRA, tk), BF16),
                   pltpu.VMEM((tq, 2 * HEAD_DIM), BF16), pltpu.VMEM((HEAD_DIM + VT_EXTRA, tq), F32),
                   pltpu.VMEM((1, tq), F32)]
        args = (q, k, v, g)
    else:
        kern = functools.partial(_stick_kernel, **common)
        scratch = [pltpu.VMEM((nblk, tk, HEAD_DIM), BF16), pltpu.VMEM((nblk, HEAD_DIM, tk), BF16),
                   pltpu.VMEM((HEAD_DIM, tq), F32)]
        args = (q, k, v)
    return pl.pallas_call(
        kern,
        grid=(b, nh),
        in_specs=in_specs,
        out_specs=pl.BlockSpec((None, t, HEAD_DIM), lambda i, h: (i, 0, h)),
        out_shape=jax.ShapeDtypeStruct((b, t, nh * HEAD_DIM), BF16),
        scratch_shapes=scratch,
        name=kind + "_attn",
        compiler_params=_cparams(("arbitrary", "arbitrary"), 52),
    )(*args)


def _outproj_kernel(oa_ref, ob_ref, oc_ref, wa_ref, wb_ref, wc_ref, x_ref, gt_ref, y_ref):
    acc = _dot(oa_ref[...], wa_ref[...]) + _dot(ob_ref[...], wb_ref[...]) + _dot(oc_ref[...], wc_ref[...])
    y_ref[...] = x_ref[...] + gt_ref[...] * acc


def _outproj(oa, ob, oc, wa, wb, wc, x2, gt, tm, tiles_per_batch):
    m = x2.shape[0]
    tn = 512
    row = lambda a: pl.BlockSpec((tm, a.shape[1]), lambda i, j: (i, 0))
    wsp = lambda a: pl.BlockSpec((a.shape[0], tn), lambda i, j: (0, j))
    return pl.pallas_call(
        _outproj_kernel,
        grid=(m // tm, D_MODEL // tn),
        in_specs=[row(oa), row(ob), row(oc), wsp(wa), wsp(wb), wsp(wc),
                  pl.BlockSpec((tm, tn), lambda i, j: (i, j)),
                  _mod_spec(gt, tm, tn, tiles_per_batch)],
        out_specs=pl.BlockSpec((tm, tn), lambda i, j: (i, j)),
        out_shape=jax.ShapeDtypeStruct((m, D_MODEL), F32),
        name="out_proj",
        compiler_params=_cparams(("arbitrary", "arbitrary"), 40),
    )(oa, ob, oc, wa, wb, wc, x2, gt)


def _ffn1_kernel(x_ref, sc_ref, sh_ref, g_ref, wg_ref, wu_ref, a_ref, h_scr):
    @pl.when(pl.program_id(1) == 0)
    def _():
        _norm_rows(x_ref, g_ref, sc_ref, sh_ref, h_scr)

    h = h_scr[...]
    gate = _dot(h, wg_ref[...])
    up = _dot(h, wu_ref[...])
    a_ref[...] = (gate * _sigmoid(gate) * up).astype(BF16)


def _ffn1(x2, sc, sh, g, wg, wu, tm, tiles_per_batch):
    m = x2.shape[0]
    dff = wg.shape[1]
    tn = 512
    full_mod = lambda a: _mod_spec(a, tm, D_MODEL, tiles_per_batch)
    return pl.pallas_call(
        _ffn1_kernel,
        grid=(m // tm, dff // tn),
        in_specs=[pl.BlockSpec((tm, D_MODEL), lambda i, j: (i, 0)),
                  full_mod(sc), full_mod(sh),
                  pl.BlockSpec((1, D_MODEL), lambda i, j: (0, 0)),
                  pl.BlockSpec((D_MODEL, tn), lambda i, j: (0, j)),
                  pl.BlockSpec((D_MODEL, tn), lambda i, j: (0, j))],
        out_specs=pl.BlockSpec((tm, tn), lambda i, j: (i, j)),
        out_shape=jax.ShapeDtypeStruct((m, dff), BF16),
        scratch_shapes=[pltpu.VMEM((tm, D_MODEL), BF16)],
        name="ffn_gate_up",
        compiler_params=_cparams(("arbitrary", "arbitrary"), 44),
    )(x2, sc, sh, g, wg, wu)


def _ffn2_kernel(a_ref, wd_ref, x_ref, gt_ref, y_ref):
    y_ref[...] = x_ref[...] + gt_ref[...] * _dot(a_ref[...], wd_ref[...])


def _ffn2(a, wd, x2, gt, tm, tiles_per_batch):
    m, dff = a.shape
    tn = 256
    return pl.pallas_call(
        _ffn2_kernel,
        grid=(m // tm, D_MODEL // tn),
        in_specs=[pl.BlockSpec((tm, dff), lambda i, j: (i, 0)),
                  pl.BlockSpec((dff, tn), lambda i, j: (0, j)),
                  pl.BlockSpec((tm, tn), lambda i, j: (i, j)),
                  _mod_spec(gt, tm, tn, tiles_per_batch)],
        out_specs=pl.BlockSpec((tm, tn), lambda i, j: (i, j)),
        out_shape=jax.ShapeDtypeStruct((m, D_MODEL), F32),
        name="ffn_down",
        compiler_params=_cparams(("arbitrary", "arbitrary"), 48),
    )(a, wd, x2, gt)


def _final_norm_kernel(x_ref, g_ref, y_ref, *, rows_chunk=64):
    g = g_ref[...]

    def body(r, c):
        r0 = pl.multiple_of(r * rows_chunk, rows_chunk)
        x = x_ref[pl.ds(r0, rows_chunk), :]
        ms = jnp.mean(x * x, axis=-1, keepdims=True)
        y_ref[pl.ds(r0, rows_chunk), :] = x * lax.rsqrt(ms + EPS) * g
        return c

    lax.fori_loop(0, x_ref.shape[0] // rows_chunk, body, 0)


def _final_norm(x2, g, tm):
    m = x2.shape[0]
    return pl.pallas_call(
        _final_norm_kernel,
        grid=(m // tm,),
        in_specs=[pl.BlockSpec((tm, D_MODEL), lambda i: (i, 0)),
                  pl.BlockSpec((1, D_MODEL), lambda i: (0, 0))],
        out_specs=pl.BlockSpec((tm, D_MODEL), lambda i: (i, 0)),
        out_shape=jax.ShapeDtypeStruct((m, D_MODEL), F32),
        name="final_norm",
        compiler_params=_cparams(("arbitrary",), 40),
    )(x2, g.reshape(1, D_MODEL))


def _pad_time(a, total):
    pad = total - a.shape[1]
    if pad == 0:
        return a
    return jnp.pad(a, ((0, 0), (0, pad)) + ((0, 0),) * (a.ndim - 2))


def _mix_and_ffn(x2, oa, ob, oc, mods, wts, tm, tiles_per_batch):
    _, _, gt1, sh2, sc2, gt2 = mods
    m = x2.shape[0]
    r2 = lambda a: a.reshape(m, a.shape[-1])
    x2 = _outproj(r2(oa), r2(ob), r2(oc), wts["wo_a"], wts["wo_b"], wts["wo_c"], x2, gt1, tm, tiles_per_batch)
    a = _ffn1(x2, sc2, sh2, wts["g_ffn"], wts["w_gate"], wts["w_up"], tm, tiles_per_batch)
    return _ffn2(a, wts["w_down"], x2, gt2, tm, tiles_per_batch)


def _prompt_layer(x2, batch, mods, wts, bias, tm_in, tm, layer, depth, prev_state):
    m = x2.shape[0]
    t = m // batch
    tiles_per_batch = t // tm
    sh1, sc1 = mods[0], mods[1]
    q, ka, kb, kc, va, vb, vc, lf, big_state = _inproj(
        x2, sc1, sh1, wts["g_attn"], wts["w_qkv"], wts["w_f"], wts["b_f"], tm_in, t // tm_in,
        state=(layer, depth, batch, prev_state))
    r3 = lambda a: a.reshape(batch, t, a.shape[-1])
    q, ka, kb, kc, va, vb, vc = map(r3, (q, ka, kb, kc, va, vb, vc))
    logf = r3(lf)[:, :, :H_B]
    keep = min(A_LEFT, t)
    small_state = (ka[:, t - keep:], va[:, t - keep:], logf)
    f = _cumsum_time(jnp.swapaxes(logf, 1, 2))
    oa = _band_attn(q, ka, va, bias, CHUNK, A_BAND - CHUNK, A_BAND - CHUNK - A_LEFT)
    tq, tk, ng = 1024, 256, 4
    ob = _causal_attn("fox", q, kb, vb, f, H_A, tq, tk, 0, ng)
    oc = _causal_attn("stick", q, kc, vc, None, H_A + H_B, tq, tk, 0, ng)
    return _mix_and_ffn(x2, oa, ob, oc, mods, wts, tm, tiles_per_batch), small_state, big_state


def _sample_layer(x2, batch, mods, wts, bias, caches, tm):
    m = x2.shape[0]
    t = m // batch
    sh1, sc1 = mods[0], mods[1]
    q, ka, kb, kc, va, vb, vc, lf, _ = _inproj(x2, sc1, sh1, wts["g_attn"], wts["w_qkv"], wts["w_f"], wts["b_f"],
                                               tm, 1)
    r3 = lambda a: a.reshape(batch, t, a.shape[-1])
    q, ka, kb, kc, va, vb, vc = map(r3, (q, ka, kb, kc, va, vb, vc))
    logf = r3(lf)[:, :, :H_B]
    cak, cav, cbk, cbv, cblogf, cck, ccv = caches
    state = (ka, va, kb, vb, logf, kc, vc)
    flat = lambda c: c.reshape(c.shape[0], c.shape[1], -1)
    past = cbk.shape[1]
    tk = 256
    tkv = past + tk
    cat = lambda c, n: _pad_time(jnp.concatenate([flat(c), n], axis=1), tkv)
    kka = jnp.concatenate([flat(cak), ka], axis=1)
    vva = jnp.concatenate([flat(cav), va], axis=1)
    wa = kka.shape[1]
    oa = _band_attn(q, kka, vva, bias, t, A_BAND - wa, A_BAND - wa)
    lcat = jnp.concatenate([cblogf.astype(F32), logf], axis=1)
    fpad = -(-tkv // (SUBLANES * LANES)) * (SUBLANES * LANES)
    f = _cumsum_time(jnp.swapaxes(_pad_time(lcat, fpad), 1, 2))[:, :, :tkv]
    qpad = _pad_time(q, LANES)
    ng = tkv // tk
    ob = _causal_attn("fox", qpad, cat(cbk, kb), cat(cbv, vb), f, H_A, LANES, tk, past, ng)[:, :t]
    oc = _causal_attn("stick", qpad, cat(cck, kc), cat(ccv, vc), None, H_A + H_B, LANES, tk, past, ng)[:, :t]
    return _mix_and_ffn(x2, oa, ob, oc, mods, wts, tm, 1), state


def kernel(x_prompt, x_sample, cache_a_k, cache_a_v, cache_b_k, cache_b_v, cache_b_logf, cache_c_k, cache_c_v,
           c_prompt, c_sample, w_ada, b_ada, g_attn, g_ffn, w_in, b_f, rel_bias, w_o, w_gate, w_up, w_down, g_final):
    depth = w_ada.shape[0]
    bp, tp, _ = x_prompt.shape
    bs, ts, _ = x_sample.shape
    n_qkv = 3 * D_MODEL

    rows = -(-(bp + bs) // SUBLANES) * SUBLANES
    c_all = jnp.concatenate([c_prompt, c_sample, jnp.zeros((rows - bp - bs, D_MODEL), F32)], axis=0)
    mod = _ada(c_all, w_ada, b_ada)

    bias_p = _relbias(rel_bias, CHUNK, A_BAND)
    bias_s = _relbias(rel_bias, ts, A_BAND)

    xp = x_prompt.reshape(bp * tp, D_MODEL)
    xs = x_sample.reshape(bs * ts, D_MODEL)
    tm_in_p, tm_p, tm_s = 512, 1024, bs * ts
    small_p, states_s = [], []
    big_p = None
    ea = H_A * HEAD_DIM
    eb = (H_A + H_B) * HEAD_DIM
    for l in range(depth):
        wo = w_o[l].astype(BF16)
        wts = dict(
            g_attn=g_attn[l].reshape(1, D_MODEL), g_ffn=g_ffn[l].reshape(1, D_MODEL),
            w_qkv=w_in[l, :, :n_qkv].astype(BF16),
            w_f=jnp.pad(w_in[l, :, n_qkv:], ((0, 0), (0, LANES - H_B))).astype(BF16),
            b_f=jnp.pad(b_f[l], (0, LANES - H_B)).reshape(1, LANES),
            wo_a=wo[:ea], wo_b=wo[ea:eb], wo_c=wo[eb:],
            w_gate=w_gate[l].astype(BF16), w_up=w_up[l].astype(BF16), w_down=w_down[l].astype(BF16))
        chunks = jnp.split(mod[l], 6, axis=-1)
        mods_p = [c[:bp].reshape(bp, 1, D_MODEL) for c in chunks]
        mods_s = [jnp.repeat(c[bp:bp + bs], ts, axis=0).reshape(1, bs * ts, D_MODEL) for c in chunks]
        xp, st_small, big_p = _prompt_layer(xp, bp, mods_p, wts, bias_p[l], tm_in_p, tm_p, l, depth, big_p)
        caches = (cache_a_k[l], cache_a_v[l], cache_b_k[l], cache_b_v[l], cache_b_logf[l],
                  cache_c_k[l], cache_c_v[l])
        xs, st_s = _sample_layer(xs, bs, mods_s, wts, bias_s[l], caches, tm_s)
        small_p.append(st_small)
        states_s.append(st_s)

    def stack_heads(arrs, heads):
        a = jnp.stack(arrs)
        return a if heads is None else a.reshape(a.shape[:3] + (heads, HEAD_DIM))

    a_k_p, a_v_p, b_logf_p = [stack_heads(arrs, h) for arrs, h in zip(zip(*small_p), (H_A, H_A, None))]
    b_k_p, b_v_p, c_k_p, c_v_p = big_p
    sample_states = [stack_heads(arrs, h)
                     for arrs, h in zip(zip(*states_s), (H_A, H_A, H_B, H_B, None, H_C, H_C))]
    y_prompt = _final_norm(xp, g_final, 512).reshape(bp, tp, D_MODEL)
    y_sample = _final_norm(xs, g_final, bs * ts).reshape(bs, ts, D_MODEL)
    return tuple([y_prompt, y_sample, a_k_p, a_v_p, b_k_p, b_v_p, b_logf_p, c_k_p, c_v_p] + sample_states)
```

```python
import functools

import jax
import jax.numpy as jnp
from jax import lax
from jax.experimental import pallas as pl
from jax.experimental.pallas import tpu as pltpu

F32 = jnp.float32
BF16 = jnp.bfloat16

D_MODEL = 2048
HEAD_DIM = 128
N_HEADS = D_MODEL // HEAD_DIM
H_A = N_HEADS // 4
H_B = (N_HEADS - H_A) // 2
H_C = N_HEADS - H_A - H_B
CHUNK = 64
A_LEFT = 8 * CHUNK
REL_CLIP = 128
N_REL = 2 * REL_CLIP + 1
EPS = 1e-6
ATTN_SCALE = HEAD_DIM ** -0.5
LOG2E = 1.4426950408889634
Q_SCALE = ATTN_SCALE * LOG2E

V7X_VMEM_BYTES = 64 * 1024 * 1024
LANES = 128
SUBLANES = 8
HEAD_PAIR = 2 * HEAD_DIM
A_BAND = 640
VT_EXTRA = 16

PAIRS_A = H_A // 2
PAIRS_AB = (H_A + H_B) // 2
N_PAIRS = N_HEADS // 2


def _cparams(sem, vmem_mb):
    return pltpu.CompilerParams(dimension_semantics=sem,
                                vmem_limit_bytes=min(vmem_mb * 1024 * 1024, V7X_VMEM_BYTES - (4 << 20)))


def _dot(a, b):
    return jnp.dot(a, b, preferred_element_type=F32)


def _dot_nt(a, b):
    return lax.dot_general(a, b, (((1,), (1,)), ((), ())), preferred_element_type=F32)


def _sigmoid(x):
    return 1.0 / (1.0 + jnp.exp(-x))


def _log_sigmoid(x):
    return jnp.minimum(x, 0.0) - jnp.log1p(jnp.exp(-jnp.abs(x)))


def _modulated_norm(x, g, sc, sh):
    ms = jnp.mean(x * x, axis=-1, keepdims=True)
    y = x * lax.rsqrt(ms + EPS) * g
    return y * (1.0 + sc) + sh


def _norm_rows(x_ref, g_ref, sc_ref, sh_ref, h_ref, rows_chunk=64):
    tm = x_ref.shape[0]
    g = g_ref[...]
    per_row = sc_ref.shape[0] != 1

    def body(r, c):
        r0 = pl.multiple_of(r * rows_chunk, rows_chunk)
        sc = sc_ref[pl.ds(r0, rows_chunk), :] if per_row else sc_ref[...]
        sh = sh_ref[pl.ds(r0, rows_chunk), :] if per_row else sh_ref[...]
        h = _modulated_norm(x_ref[pl.ds(r0, rows_chunk), :], g, sc, sh)
        h_ref[pl.ds(r0, rows_chunk), :] = h.astype(h_ref.dtype)
        return c

    lax.fori_loop(0, tm // rows_chunk, body, 0)


def _copy_cast(src_ref, dst_ref, rows, dst_off=0, step=256):
    n = rows // step

    def body(i, c):
        r = pl.multiple_of(i * step, step)
        d = pl.multiple_of(dst_off + i * step, 16)
        dst_ref[pl.ds(d, step), :] = src_ref[pl.ds(r, step), :].astype(dst_ref.dtype)
        return c

    lax.fori_loop(0, n, body, 0)
    if rows % step:
        dst_ref[dst_off + n * step:dst_off + rows, :] = src_ref[n * step:rows, :].astype(dst_ref.dtype)


def _split3(x):
    hi = x.astype(BF16)
    r1 = x - hi.astype(F32)
    mid = r1.astype(BF16)
    lo = (r1 - mid.astype(F32)).astype(BF16)
    return hi, mid, lo


def _ada_kernel(c_ref, w_ref, b_ref, o_ref):
    c = c_ref[...]
    a = (c * _sigmoid(c)).astype(BF16)
    o_ref[...] = _dot(a, w_ref[...].astype(BF16)) + b_ref[...]


def _ada(c_all, w_ada, b_ada):
    depth, _, n = w_ada.shape
    rows = c_all.shape[0]
    tn = 1024
    return pl.pallas_call(
        _ada_kernel,
        grid=(depth, n // tn),
        in_specs=[pl.BlockSpec((rows, D_MODEL), lambda l, j: (0, 0)),
                  pl.BlockSpec((None, D_MODEL, tn), lambda l, j: (l, 0, j)),
                  pl.BlockSpec((None, 1, tn), lambda l, j: (l, 0, j))],
        out_specs=pl.BlockSpec((None, rows, tn), lambda l, j: (l, 0, j)),
        out_shape=jax.ShapeDtypeStruct((depth, rows, n), F32),
        name="ada_mod",
        compiler_params=_cparams(("arbitrary", "arbitrary"), 40),
    )(c_all, w_ada, b_ada.reshape(depth, 1, n))


def _relbias_kernel(tab_ref, o_ref, *, cq, band):
    l = pl.program_id(0)
    h = pl.program_id(1)
    qi = lax.broadcasted_iota(jnp.int32, (cq, band), 0)
    s = lax.broadcasted_iota(jnp.int32, (cq, band), 1)
    idx = jnp.clip(qi + (band - cq) - s, -REL_CLIP, REL_CLIP) + REL_CLIP
    base = l * (N_REL * H_A) + h

    def body(r, acc):
        return jnp.where(idx == r, tab_ref[base + r * H_A], acc)

    o_ref[...] = lax.fori_loop(0, N_REL, body, jnp.zeros((cq, band), F32)) * LOG2E


def _relbias(rel_bias, cq, band):
    depth = rel_bias.shape[0]
    return pl.pallas_call(
        functools.partial(_relbias_kernel, cq=cq, band=band),
        grid=(depth, H_A),
        in_specs=[pl.BlockSpec(memory_space=pltpu.SMEM)],
        out_specs=pl.BlockSpec((None, None, cq, band), lambda l, h: (l, h, 0, 0)),
        out_shape=jax.ShapeDtypeStruct((depth, H_A, cq, band), F32),
        name="rel_bias",
    )(rel_bias.reshape(-1))


def _inproj_kernel(*refs, n_prev, with_state):
    x_ref, sc_ref, sh_ref, g_ref, wq_ref, wk_ref, wv_ref, wf_ref, bf_ref = refs[:9]
    outs = refs[9 + n_prev:-1]
    h_scr = refs[-1]
    q_ref, ka_ref, va_ref, kb_ref, vb_ref, kc_ref, vc_ref, lf_ref = outs[:8]
    j = pl.program_id(1)

    @pl.when(j == 0)
    def _():
        _norm_rows(x_ref, g_ref, sc_ref, sh_ref, h_scr)
        lf_ref[...] = _log_sigmoid(_dot(h_scr[...], wf_ref[...]) + bf_ref[...])

    h = h_scr[...]
    q_scale = jnp.where(j >= PAIRS_AB, -Q_SCALE, Q_SCALE)
    q_ref[...] = (_dot(h, wq_ref[...]) * q_scale).astype(BF16)
    kt = _dot(h, wk_ref[...])
    vt = _dot(h, wv_ref[...])

    @pl.when(j < PAIRS_A)
    def _():
        ka_ref[...] = kt
        va_ref[...] = vt

    @pl.when((j >= PAIRS_A) & (j < PAIRS_AB))
    def _():
        kb_ref[...] = kt.astype(kb_ref.dtype)
        vb_ref[...] = vt.astype(vb_ref.dtype)

    @pl.when(j >= PAIRS_AB)
    def _():
        kc_ref[...] = kt.astype(kc_ref.dtype)
        vc_ref[...] = vt.astype(vc_ref.dtype)

    if with_state:
        kbs_ref, vbs_ref, kcs_ref, vcs_ref = outs[8:12]
        for pair in range(PAIRS_A, N_PAIRS):
            in_b = pair < PAIRS_AB
            ks_ref, vs_ref = (kbs_ref, vbs_ref) if in_b else (kcs_ref, vcs_ref)
            head0 = 2 * (pair - (PAIRS_A if in_b else PAIRS_AB))

            @pl.when(j == pair)
            def _(ks_ref=ks_ref, vs_ref=vs_ref, head0=head0):
                for c in range(2):
                    ks_ref[:, head0 + c, :] = kt[:, c * HEAD_DIM:(c + 1) * HEAD_DIM]
                    vs_ref[:, head0 + c, :] = vt[:, c * HEAD_DIM:(c + 1) * HEAD_DIM]


def _mod_spec(mod, tm, tn, tiles_per_batch):
    r = mod.shape[1]
    col = (lambda j: 0) if tn == mod.shape[2] else (lambda j: j)
    if r == 1:
        return pl.BlockSpec((None, 1, tn), lambda i, j: (i // tiles_per_batch, 0, col(j)))
    return pl.BlockSpec((None, tm, tn), lambda i, j: (0, i, col(j)))


def _inproj(x2, sc, sh, g, w_qkv, w_f, b_f, tm, tiles_per_batch, state=None):
    m = x2.shape[0]
    tn = HEAD_PAIR

    def clamp(lo, n):
        return lambda i, j: (i, jnp.clip(j - lo, 0, n - 1))

    def full_mod(a):
        return _mod_spec(a, tm, D_MODEL, tiles_per_batch)

    kv_dtype = F32 if state is None else BF16
    groups = ((0, PAIRS_A, F32), (PAIRS_A, PAIRS_AB - PAIRS_A, kv_dtype), (PAIRS_AB, N_PAIRS - PAIRS_AB, kv_dtype))
    out_shape = [jax.ShapeDtypeStruct((m, D_MODEL), BF16)]
    out_specs = [pl.BlockSpec((tm, tn), lambda i, j: (i, j))]
    for lo, n, dt in groups:
        for _ in range(2):
            out_shape.append(jax.ShapeDtypeStruct((m, n * tn), dt))
            out_specs.append(pl.BlockSpec((tm, tn), clamp(lo, n)))
    out_shape.append(jax.ShapeDtypeStruct((m, LANES), F32))
    out_specs.append(pl.BlockSpec((tm, LANES), lambda i, j: (i, 0)))
    in_specs = [pl.BlockSpec((tm, D_MODEL), lambda i, j: (i, 0)),
                full_mod(sc), full_mod(sh),
                pl.BlockSpec((1, D_MODEL), lambda i, j: (0, 0)),
                pl.BlockSpec((D_MODEL, tn), lambda i, j: (0, j)),
                pl.BlockSpec((D_MODEL, tn), lambda i, j: (0, N_PAIRS + j)),
                pl.BlockSpec((D_MODEL, tn), lambda i, j: (0, 2 * N_PAIRS + j)),
                pl.BlockSpec((D_MODEL, LANES), lambda i, j: (0, 0)),
                pl.BlockSpec((1, LANES), lambda i, j: (0, 0))]
    args = [x2, sc, sh, g, w_qkv, w_qkv, w_qkv, w_f, b_f]
    n_in, n_plain_out = len(args), len(out_shape)
    aliases = {}
    n_prev = 0
    if state is not None:
        layer, depth, batch, prev = state
        t = m // batch
        for idx, heads in enumerate((H_B, H_B, H_C, H_C)):
            out_shape.append(jax.ShapeDtypeStruct((depth, batch, t, heads, HEAD_DIM), F32))
            out_specs.append(pl.BlockSpec(
                (None, None, tm, heads, HEAD_DIM),
                lambda i, j: (layer, i // tiles_per_batch, i % tiles_per_batch, 0, 0)))
            if prev is not None:
                in_specs.append(pl.BlockSpec(memory_space=pl.ANY))
                args.append(prev[idx])
                aliases[n_in + idx] = n_plain_out + idx
                n_prev += 1
    outs = pl.pallas_call(
        functools.partial(_inproj_kernel, n_prev=n_prev, with_state=state is not None),
        grid=(m // tm, N_PAIRS),
        in_specs=in_specs,
        out_specs=out_specs,
        out_shape=out_shape,
        input_output_aliases=aliases,
        scratch_shapes=[pltpu.VMEM((tm, D_MODEL), BF16)],
        name="in_proj",
        compiler_params=_cparams(("arbitrary", "arbitrary"), 52),
    )(*args)
    q, ka, va, kb, vb, kc, vc, lf = outs[:8]
    return q, ka, kb, kc, va, vb, vc, lf, tuple(outs[8:])


def _dot_f32_by_01(x, ones01):
    hi, mid, lo = _split3(x)
    return _dot(hi, ones01) + _dot(mid, ones01) + _dot(lo, ones01)


def _cumsum_kernel(x_ref, o_ref):
    x = x_ref[...]
    rows = x.shape[0]
    i0 = lax.broadcasted_iota(jnp.int32, (LANES, LANES), 0)
    i1 = lax.broadcasted_iota(jnp.int32, (LANES, LANES), 1)
    upper = (i0 <= i1).astype(BF16)
    c = _dot_f32_by_01(x, upper)
    tot = jnp.broadcast_to(c[:, LANES - 1:LANES], (rows, LANES))
    r0 = lax.broadcasted_iota(jnp.int32, (rows, rows), 0)
    r1 = lax.broadcasted_iota(jnp.int32, (rows, rows), 1)
    lower = (r1 < r0).astype(BF16)
    hi, mid, lo = _split3(tot)
    off = _dot(lower, hi) + _dot(lower, mid) + _dot(lower, lo)
    o_ref[...] = (c + off) * LOG2E


def _cumsum_time(logf_hm):
    b, h, t = logf_hm.shape
    rows = t // LANES
    out = pl.pallas_call(
        _cumsum_kernel,
        grid=(b * h,),
        in_specs=[pl.BlockSpec((None, rows, LANES), lambda i: (i, 0, 0))],
        out_specs=pl.BlockSpec((None, rows, LANES), lambda i: (i, 0, 0)),
        out_shape=jax.ShapeDtypeStruct((b * h, rows, LANES), F32),
        name="forget_cumsum",
    )(logf_hm.reshape(b * h, rows, LANES))
    return out.reshape(b, h, t)


def _band_kernel(q_ref, k_ref, v_ref, bias_ref, o_ref, kp_scr, vp_scr, *, cq, npad, nmask, nchunks):
    tk = k_ref.shape[0]
    kp_scr[0:npad, :] = jnp.zeros((npad, HEAD_DIM), BF16)
    vp_scr[0:npad, :] = jnp.zeros((npad, HEAD_DIM), BF16)
    _copy_cast(k_ref, kp_scr, tk, dst_off=npad)
    _copy_cast(v_ref, vp_scr, tk, dst_off=npad)
    bias = bias_ref[...]
    slot = lax.broadcasted_iota(jnp.int32, (cq, A_BAND), 1)
    per_step = min(nchunks, 8)

    def step(i, carry):
        rows = [pl.multiple_of((i * per_step + u) * cq, cq) for u in range(per_step)]
        scores = [_dot_nt(q_ref[pl.ds(r0, cq), :], kp_scr[pl.ds(r0, A_BAND), :]) for r0 in rows]
        probs, denoms = [], []
        for u in range(per_step):
            first_valid = jnp.maximum(nmask, npad - (i * per_step + u) * cq)
            s = jnp.where(slot >= first_valid, scores[u] + bias, -jnp.inf)
            p = jnp.exp2(s - jnp.max(s, axis=-1, keepdims=True))
            denoms.append(jnp.sum(p, axis=-1, keepdims=True))
            probs.append(p.astype(BF16))
        for u in range(per_step):
            o = _dot(probs[u], vp_scr[pl.ds(rows[u], A_BAND), :]) / denoms[u]
            o_ref[pl.ds(rows[u], cq), :] = o.astype(o_ref.dtype)
        return carry

    lax.fori_loop(0, nchunks // per_step, step, 0)


def _band_attn(q, k, v, bias, cq, npad, nmask):
    b, t, _ = q.shape
    tk = k.shape[1]
    kern = functools.partial(_band_kernel, cq=cq, npad=npad, nmask=nmask, nchunks=t // cq)
    return pl.pallas_call(
        kern,
        grid=(b, H_A),
        in_specs=[pl.BlockSpec((None, t, HEAD_DIM), lambda i, h: (i, 0, h)),
                  pl.BlockSpec((None, tk, HEAD_DIM), lambda i, h: (i, 0, h)),
                  pl.BlockSpec((None, tk, HEAD_DIM), lambda i, h: (i, 0, h)),
                  pl.BlockSpec((None, cq, A_BAND), lambda i, h: (h, 0, 0))],
        out_specs=pl.BlockSpec((None, t, HEAD_DIM), lambda i, h: (i, 0, h)),
        out_shape=jax.ShapeDtypeStruct((b, t, H_A * HEAD_DIM), BF16),
        scratch_shapes=[pltpu.VMEM((npad + tk, HEAD_DIM), BF16),
                        pltpu.VMEM((npad + tk, HEAD_DIM), BF16)],
        name="band_attn",
        compiler_params=_cparams(("arbitrary", "arbitrary"), 48),
    )(q, k, v, bias)


def _col_replicate(row):
    n = row.shape[1]
    parts = [jnp.broadcast_to(row[:, c * LANES:(c + 1) * LANES], (LANES, LANES)).T for c in range(n // LANES)]
    return parts[0] if len(parts) == 1 else jnp.concatenate(parts, axis=0)


def _aug_columns(col_rep, first):
    hi, mid, lo = _split3(col_rep)
    lane = lax.broadcasted_iota(jnp.int32, col_rep.shape, 1)
    base = 0 if first else 3
    one = jnp.where((lane >= 3 - base) & (lane < 6 - base), 1.0, 0.0)
    x = jnp.where(lane == base, hi.astype(F32),
                  jnp.where(lane == base + 1, mid.astype(F32),
                            jnp.where(lane == base + 2, lo.astype(F32), one)))
    return x.astype(BF16)


def _fox_kernel(q_ref, k_ref, v_ref, g_ref, o_ref, kp_scr, vt_scr, qp_scr, acc_scr, x_scr,
                *, tq, tk, nq, qoff, ng):
    nblk = kp_scr.shape[0]
    ones_rows = (lax.broadcasted_iota(jnp.int32, (VT_EXTRA, tk), 0) == 0).astype(BF16)

    def prep(kb, c):
        r0 = pl.multiple_of(kb * tk, tk)
        kp_scr[kb, :, 0:HEAD_DIM] = k_ref[pl.ds(r0, tk), :].astype(BF16)
        kp_scr[kb, :, HEAD_DIM:2 * HEAD_DIM] = _aug_columns(_col_replicate(-g_ref[kb]), True)
        vt_scr[kb, 0:HEAD_DIM, :] = v_ref[pl.ds(r0, tk), :].astype(F32).T.astype(BF16)
        vt_scr[kb, HEAD_DIM:HEAD_DIM + VT_EXTRA, :] = ones_rows
        return c

    lax.fori_loop(0, nblk, prep, 0)
    off = qoff % tk
    row = lax.broadcasted_iota(jnp.int32, (tk, tq), 0)
    col = lax.broadcasted_iota(jnp.int32, (tk, tq), 1)

    def scores(g, slot, p0):
        qp = qp_scr[...]
        m_blk = None
        for u in range(ng):
            x = _dot_nt(kp_scr[g * ng + u], qp)
            if p0 is not None:
                x = jnp.where(row + ((g * ng + u) * tk - p0) <= col, x, -jnp.inf)
            x_scr[slot, u] = x
            m_u = jnp.max(x, axis=0, keepdims=True)
            m_blk = m_u if m_blk is None else jnp.maximum(m_blk, m_u)
        return m_blk

    def process(g, slot, m_cur, m_next):
        acc = acc_scr[...]
        for u in range(ng):
            p = jnp.exp2(x_scr[slot, u] - m_cur)
            acc = acc + _dot(vt_scr[g * ng + u], p.astype(BF16))
        acc_scr[...] = acc * jnp.exp2(m_cur - m_next)

    def qblock(i, carry):
        q0 = pl.multiple_of(i * tq, tq)
        p0 = qoff + i * tq
        pb = p0 // tk
        qp_scr[:, 0:HEAD_DIM] = q_ref[pl.ds(q0, tq), :]
        if tq <= tk:
            gq = g_ref[pb][:, off:off + tq]
        else:
            gq = jnp.concatenate([g_ref[pb + c] for c in range(tq // tk)], axis=1)
        qp_scr[:, HEAD_DIM:2 * HEAD_DIM] = _aug_columns(_col_replicate(gq), False)
        acc_scr[...] = jnp.zeros((HEAD_DIM + VT_EXTRA, tq), F32)
        gd = pb // ng
        m0 = scores(gd, 0, p0)
        m_pre = scores(jnp.maximum(gd - 1, 0), 1, None)
        m1 = jnp.where(gd >= 1, jnp.maximum(m0, m_pre), m0)
        process(gd, 0, m0, m1)

        def pair(pi, m_run):
            g1 = gd - 1 - 2 * pi
            m_a = jnp.maximum(m_run, scores(g1 - 1, 0, None))
            process(g1, 1, m_run, m_a)
            m_pre = scores(jnp.maximum(g1 - 2, 0), 1, None)
            m_b = jnp.where(g1 >= 2, jnp.maximum(m_a, m_pre), m_a)
            process(g1 - 1, 0, m_a, m_b)
            return m_b

        m_last = lax.fori_loop(0, gd // 2, pair, m1)

        @pl.when(gd % 2 == 1)
        def _():
            process(0, 1, m_last, m_last)

        denom = acc_scr[HEAD_DIM:HEAD_DIM + 1, :]
        o_ref[pl.ds(q0, tq), :] = (acc_scr[0:HEAD_DIM, :] / denom).T.astype(o_ref.dtype)
        return carry

    lax.fori_loop(0, nq, qblock, 0)


def _stick_kernel(q_ref, k_ref, v_ref, o_ref, kp_scr, vt_scr, acc_scr, x_scr, *, tq, tk, nq, qoff, ng):
    nblk = kp_scr.shape[0]
    nv = tk // SUBLANES

    r = lax.broadcasted_iota(jnp.int32, (tk, tk), 0)
    kk = lax.broadcasted_iota(jnp.int32, (tk, tk), 1)
    perm = ((r % SUBLANES) * nv + r // SUBLANES == kk).astype(BF16)
    perm_t = ((kk % SUBLANES) * nv + kk // SUBLANES == r).astype(BF16)

    def prep(kb, c):
        r0 = pl.multiple_of(kb * tk, tk)
        kp_scr[kb] = _dot(perm, k_ref[pl.ds(r0, tk), :].astype(BF16)).astype(BF16)
        vt_scr[kb] = _dot(v_ref[pl.ds(r0, tk), :].astype(F32).T.astype(BF16), perm_t).astype(BF16)
        return c

    lax.fori_loop(0, nblk, prep, 0)
    row = lax.broadcasted_iota(jnp.int32, (tk, tq), 0)
    col = lax.broadcasted_iota(jnp.int32, (tk, tq), 1)
    key_in_block = (row % SUBLANES) * nv + row // SUBLANES
    sub = lax.broadcasted_iota(jnp.int32, (SUBLANES, tq), 0)

    def scores(q, g, slot):
        for u in range(ng):
            x_scr[slot, u] = _dot_nt(kp_scr[g * ng + u], q)

    def tile_terms(x, kb, p0):
        beta = 1.0 / (1.0 + jnp.exp2(x))
        stay = 1.0 - beta
        valid = None
        if p0 is not None:
            valid = key_in_block + (kb * tk - p0) < col
            stay = jnp.where(valid, stay, 1.0)
        after = [None] * nv
        run = jnp.ones((SUBLANES, tq), F32)
        for v in range(nv - 1, -1, -1):
            after[v] = run
            run = run * stay[v * SUBLANES:(v + 1) * SUBLANES, :]
        incl = run
        for k in (1, 2, 4):
            incl = incl * jnp.where(sub + k < SUBLANES, pltpu.roll(incl, SUBLANES - k, axis=0), 1.0)
        later = jnp.where(sub + 1 < SUBLANES, pltpu.roll(incl, SUBLANES - 1, axis=0), 1.0)
        return after, later, incl[0:1, :], beta, valid

    def process(g, slot, carry, p0):
        terms = [tile_terms(x_scr[slot, u], g * ng + u, p0) for u in range(ng)]
        acc = acc_scr[...]
        for u in range(ng - 1, -1, -1):
            after, later, total, beta, valid = terms[u]
            scale_u = later * carry
            w = jnp.concatenate([after[v] * scale_u for v in range(nv)], axis=0) * beta
            if valid is not None:
                w = jnp.where(valid, w, 0.0)
            acc = acc + _dot(vt_scr[g * ng + u], w.astype(BF16))
            carry = carry * total
        acc_scr[...] = acc
        return carry

    def qblock(i, c0):
        q0 = pl.multiple_of(i * tq, tq)
        p0 = qoff + i * tq
        gd = (p0 // tk) // ng
        q = q_ref[pl.ds(q0, tq), :]
        acc_scr[...] = jnp.zeros((HEAD_DIM, tq), F32)
        scores(q, gd, 0)
        scores(q, jnp.maximum(gd - 1, 0), 1)
        carry = process(gd, 0, jnp.ones((1, tq), F32), p0)

        def pair(pi, carry):
            g1 = gd - 1 - 2 * pi
            scores(q, g1 - 1, 0)
            carry = process(g1, 1, carry, None)
            scores(q, jnp.maximum(g1 - 2, 0), 1)
            return process(g1 - 1, 0, carry, None)

        carry = lax.fori_loop(0, gd // 2, pair, carry)

        @pl.when(gd % 2 == 1)
        def _():
            process(0, 1, carry, None)

        o_ref[pl.ds(q0, tq), :] = acc_scr[...].T.astype(o_ref.dtype)
        return c0

    lax.fori_loop(0, nq, qblock, 0)


def _causal_attn(kind, q, k, v, f, head0, tq, tk, qoff, ng):
    b, t, _ = q.shape
    tkv = k.shape[1]
    nh = k.shape[2] // HEAD_DIM
    nblk = tkv // tk
    assert tkv % tk == 0 and nblk % ng == 0 and tq % LANES == 0 and t % tq == 0
    assert (ng * tk) % tq == 0 and qoff % tq == 0 and (tq % tk == 0 or (qoff % tk) + tq <= tk and t == tq)
    common = dict(tq=tq, tk=tk, nq=t // tq, qoff=qoff, ng=ng)
    kv_spec = pl.BlockSpec((None, tkv, HEAD_DIM), lambda i, h: (i, 0, h))
    in_specs = [pl.BlockSpec((None, t, HEAD_DIM), lambda i, h: (i, 0, head0 + h)), kv_spec, kv_spec]
    if kind == "fox":
        kern = functools.partial(_fox_kernel, **common)
        g = f.reshape(b, nh, nblk, 1, tk)
        in_specs.append(pl.BlockSpec((None, None, nblk, 1, tk), lambda i, h: (i, h, 0, 0, 0)))
        scratch = [pltpu.VMEM((nblk, tk, 2 * HEAD_DIM), BF16), pltpu.VMEM((nblk, HEAD_DIM + VT_EXTRA, tk), BF16),
                   pltpu.VMEM((tq, 2 * HEAD_DIM), BF16), pltpu.VMEM((HEAD_DIM + VT_EXTRA, tq), F32),
                   pltpu.VMEM((2, ng, tk, tq), F32)]
        args = (q, k, v, g)
    else:
        kern = functools.partial(_stick_kernel, **common)
        scratch = [pltpu.VMEM((nblk, tk, HEAD_DIM), BF16), pltpu.VMEM((nblk, HEAD_DIM, tk), BF16),
                   pltpu.VMEM((HEAD_DIM, tq), F32), pltpu.VMEM((2, ng, tk, tq), F32)]
        args = (q, k, v)
    return pl.pallas_call(
        kern,
        grid=(b, nh),
        in_specs=in_specs,
        out_specs=pl.BlockSpec((None, t, HEAD_DIM), lambda i, h: (i, 0, h)),
        out_shape=jax.ShapeDtypeStruct((b, t, nh * HEAD_DIM), BF16),
        scratch_shapes=scratch,
        name=kind + "_attn",
        compiler_params=_cparams(("arbitrary", "arbitrary"), 52),
    )(*args)


def _outproj_kernel(oa_ref, ob_ref, oc_ref, wa_ref, wb_ref, wc_ref, x_ref, gt_ref, y_ref):
    acc = _dot(oa_ref[...], wa_ref[...]) + _dot(ob_ref[...], wb_ref[...]) + _dot(oc_ref[...], wc_ref[...])
    y_ref[...] = x_ref[...] + gt_ref[...] * acc


def _outproj(oa, ob, oc, wa, wb, wc, x2, gt, tm, tiles_per_batch):
    m = x2.shape[0]
    tn = 512
    row = lambda a: pl.BlockSpec((tm, a.shape[1]), lambda i, j: (i, 0))
    wsp = lambda a: pl.BlockSpec((a.shape[0], tn), lambda i, j: (0, j))
    return pl.pallas_call(
        _outproj_kernel,
        grid=(m // tm, D_MODEL // tn),
        in_specs=[row(oa), row(ob), row(oc), wsp(wa), wsp(wb), wsp(wc),
                  pl.BlockSpec((tm, tn), lambda i, j: (i, j)),
                  _mod_spec(gt, tm, tn, tiles_per_batch)],
        out_specs=pl.BlockSpec((tm, tn), lambda i, j: (i, j)),
        out_shape=jax.ShapeDtypeStruct((m, D_MODEL), F32),
        name="out_proj",
        compiler_params=_cparams(("arbitrary", "arbitrary"), 40),
    )(oa, ob, oc, wa, wb, wc, x2, gt)


def _ffn1_kernel(x_ref, sc_ref, sh_ref, g_ref, wg_ref, wu_ref, a_ref, h_scr):
    @pl.when(pl.program_id(1) == 0)
    def _():
        _norm_rows(x_ref, g_ref, sc_ref, sh_ref, h_scr)

    h = h_scr[...]
    gate = _dot(h, wg_ref[...])
    up = _dot(h, wu_ref[...])
    a_ref[...] = (gate * _sigmoid(gate) * up).astype(BF16)


def _ffn1(x2, sc, sh, g, wg, wu, tm, tiles_per_batch):
    m = x2.shape[0]
    dff = wg.shape[1]
    tn = 512
    full_mod = lambda a: _mod_spec(a, tm, D_MODEL, tiles_per_batch)
    return pl.pallas_call(
        _ffn1_kernel,
        grid=(m // tm, dff // tn),
        in_specs=[pl.BlockSpec((tm, D_MODEL), lambda i, j: (i, 0)),
                  full_mod(sc), full_mod(sh),
                  pl.BlockSpec((1, D_MODEL), lambda i, j: (0, 0)),
                  pl.BlockSpec((D_MODEL, tn), lambda i, j: (0, j)),
                  pl.BlockSpec((D_MODEL, tn), lambda i, j: (0, j))],
        out_specs=pl.BlockSpec((tm, tn), lambda i, j: (i, j)),
        out_shape=jax.ShapeDtypeStruct((m, dff), BF16),
        scratch_shapes=[pltpu.VMEM((tm, D_MODEL), BF16)],
        name="ffn_gate_up",
        compiler_params=_cparams(("arbitrary", "arbitrary"), 44),
    )(x2, sc, sh, g, wg, wu)


def _ffn2_kernel(a_ref, wd_ref, x_ref, gt_ref, y_ref):
    y_ref[...] = x_ref[...] + gt_ref[...] * _dot(a_ref[...], wd_ref[...])


def _ffn2(a, wd, x2, gt, tm, tiles_per_batch):
    m, dff = a.shape
    tn = 256
    return pl.pallas_call(
        _ffn2_kernel,
        grid=(m // tm, D_MODEL // tn),
        in_specs=[pl.BlockSpec((tm, dff), lambda i, j: (i, 0)),
                  pl.BlockSpec((dff, tn), lambda i, j: (0, j)),
                  pl.BlockSpec((tm, tn), lambda i, j: (i, j)),
                  _mod_spec(gt, tm, tn, tiles_per_batch)],
        out_specs=pl.BlockSpec((tm, tn), lambda i, j: (i, j)),
        out_shape=jax.ShapeDtypeStruct((m, D_MODEL), F32),
        name="ffn_down",
        compiler_params=_cparams(("arbitrary", "arbitrary"), 48),
    )(a, wd, x2, gt)


def _final_norm_kernel(x_ref, g_ref, y_ref, *, rows_chunk=64):
    g = g_ref[...]

    def body(r, c):
        r0 = pl.multiple_of(r * rows_chunk, rows_chunk)
        x = x_ref[pl.ds(r0, rows_chunk), :]
        ms = jnp.mean(x * x, axis=-1, keepdims=True)
        y_ref[pl.ds(r0, rows_chunk), :] = x * lax.rsqrt(ms + EPS) * g
        return c

    lax.fori_loop(0, x_ref.shape[0] // rows_chunk, body, 0)


def _final_norm(x2, g, tm):
    m = x2.shape[0]
    return pl.pallas_call(
        _final_norm_kernel,
        grid=(m // tm,),
        in_specs=[pl.BlockSpec((tm, D_MODEL), lambda i: (i, 0)),
                  pl.BlockSpec((1, D_MODEL), lambda i: (0, 0))],
        out_specs=pl.BlockSpec((tm, D_MODEL), lambda i: (i, 0)),
        out_shape=jax.ShapeDtypeStruct((m, D_MODEL), F32),
        name="final_norm",
        compiler_params=_cparams(("arbitrary",), 40),
    )(x2, g.reshape(1, D_MODEL))


def _pad_time(a, total):
    pad = total - a.shape[1]
    if pad == 0:
        return a
    return jnp.pad(a, ((0, 0), (0, pad)) + ((0, 0),) * (a.ndim - 2))


def _mix_and_ffn(x2, oa, ob, oc, mods, wts, tm, tiles_per_batch):
    _, _, gt1, sh2, sc2, gt2 = mods
    m = x2.shape[0]
    r2 = lambda a: a.reshape(m, a.shape[-1])
    x2 = _outproj(r2(oa), r2(ob), r2(oc), wts["wo_a"], wts["wo_b"], wts["wo_c"], x2, gt1, tm, tiles_per_batch)
    a = _ffn1(x2, sc2, sh2, wts["g_ffn"], wts["w_gate"], wts["w_up"], tm, tiles_per_batch)
    return _ffn2(a, wts["w_down"], x2, gt2, tm, tiles_per_batch)


def _prompt_layer(x2, batch, mods, wts, bias, tm_in, tm, layer, depth, prev_state):
    m = x2.shape[0]
    t = m // batch
    tiles_per_batch = t // tm
    sh1, sc1 = mods[0], mods[1]
    q, ka, kb, kc, va, vb, vc, lf, big_state = _inproj(
        x2, sc1, sh1, wts["g_attn"], wts["w_qkv"], wts["w_f"], wts["b_f"], tm_in, t // tm_in,
        state=(layer, depth, batch, prev_state))
    r3 = lambda a: a.reshape(batch, t, a.shape[-1])
    q, ka, kb, kc, va, vb, vc = map(r3, (q, ka, kb, kc, va, vb, vc))
    logf = r3(lf)[:, :, :H_B]
    keep = min(A_LEFT, t)
    small_state = (ka[:, t - keep:], va[:, t - keep:], logf)
    f = _cumsum_time(jnp.swapaxes(logf, 1, 2))
    oa = _band_attn(q, ka, va, bias, CHUNK, A_BAND - CHUNK, A_BAND - CHUNK - A_LEFT)
    tq, tk, ng = 1024, 256, 4
    ob = _causal_attn("fox", q, kb, vb, f, H_A, tq, tk, 0, ng)
    oc = _causal_attn("stick", q, kc, vc, None, H_A + H_B, tq, tk, 0, ng)
    return _mix_and_ffn(x2, oa, ob, oc, mods, wts, tm, tiles_per_batch), small_state, big_state


def _sample_layer(x2, batch, mods, wts, bias, caches, tm):
    m = x2.shape[0]
    t = m // batch
    sh1, sc1 = mods[0], mods[1]
    q, ka, kb, kc, va, vb, vc, lf, _ = _inproj(x2, sc1, sh1, wts["g_attn"], wts["w_qkv"], wts["w_f"], wts["b_f"],
                                               tm, 1)
    r3 = lambda a: a.reshape(batch, t, a.shape[-1])
    q, ka, kb, kc, va, vb, vc = map(r3, (q, ka, kb, kc, va, vb, vc))
    logf = r3(lf)[:, :, :H_B]
    cak, cav, cbk, cbv, cblogf, cck, ccv = caches
    state = (ka, va, kb, vb, logf, kc, vc)
    flat = lambda c: c.reshape(c.shape[0], c.shape[1], -1)
    past = cbk.shape[1]
    tk = 256
    tkv = past + tk
    cat = lambda c, n: _pad_time(jnp.concatenate([flat(c), n], axis=1), tkv)
    kka = jnp.concatenate([flat(cak), ka], axis=1)
    vva = jnp.concatenate([flat(cav), va], axis=1)
    wa = kka.shape[1]
    oa = _band_attn(q, kka, vva, bias, t, A_BAND - wa, A_BAND - wa)
    lcat = jnp.concatenate([cblogf.astype(F32), logf], axis=1)
    fpad = -(-tkv // (SUBLANES * LANES)) * (SUBLANES * LANES)
    f = _cumsum_time(jnp.swapaxes(_pad_time(lcat, fpad), 1, 2))[:, :, :tkv]
    qpad = _pad_time(q, LANES)
    ng = tkv // tk
    ob = _causal_attn("fox", qpad, cat(cbk, kb), cat(cbv, vb), f, H_A, LANES, tk, past, ng)[:, :t]
    oc = _causal_attn("stick", qpad, cat(cck, kc), cat(ccv, vc), None, H_A + H_B, LANES, tk, past, ng)[:, :t]
    return _mix_and_ffn(x2, oa, ob, oc, mods, wts, tm, 1), state


def kernel(x_prompt, x_sample, cache_a_k, cache_a_v, cache_b_k, cache_b_v, cache_b_logf, cache_c_k, cache_c_v,
           c_prompt, c_sample, w_ada, b_ada, g_attn, g_ffn, w_in, b_f, rel_bias, w_o, w_gate, w_up, w_down, g_final):
    depth = w_ada.shape[0]
    bp, tp, _ = x_prompt.shape
    bs, ts, _ = x_sample.shape
    n_qkv = 3 * D_MODEL

    rows = -(-(bp + bs) // SUBLANES) * SUBLANES
    c_all = jnp.concatenate([c_prompt, c_sample, jnp.zeros((rows - bp - bs, D_MODEL), F32)], axis=0)
    mod = _ada(c_all, w_ada, b_ada)

    bias_p = _relbias(rel_bias, CHUNK, A_BAND)
    bias_s = _relbias(rel_bias, ts, A_BAND)

    xp = x_prompt.reshape(bp * tp, D_MODEL)
    xs = x_sample.reshape(bs * ts, D_MODEL)
    tm_in_p, tm_p, tm_s = 512, 1024, bs * ts
    small_p, states_s = [], []
    big_p = None
    ea = H_A * HEAD_DIM
    eb = (H_A + H_B) * HEAD_DIM
    for l in range(depth):
        wo = w_o[l].astype(BF16)
        wts = dict(
            g_attn=g_attn[l].reshape(1, D_MODEL), g_ffn=g_ffn[l].reshape(1, D_MODEL),
            w_qkv=w_in[l, :, :n_qkv].astype(BF16),
            w_f=jnp.pad(w_in[l, :, n_qkv:], ((0, 0), (0, LANES - H_B))).astype(BF16),
            b_f=jnp.pad(b_f[l], (0, LANES - H_B)).reshape(1, LANES),
            wo_a=wo[:ea], wo_b=wo[ea:eb], wo_c=wo[eb:],
            w_gate=w_gate[l].astype(BF16), w_up=w_up[l].astype(BF16), w_down=w_down[l].astype(BF16))
        chunks = jnp.split(mod[l], 6, axis=-1)
        mods_p = [c[:bp].reshape(bp, 1, D_MODEL) for c in chunks]
        mods_s = [jnp.repeat(c[bp:bp + bs], ts, axis=0).reshape(1, bs * ts, D_MODEL) for c in chunks]
        xp, st_small, big_p = _prompt_layer(xp, bp, mods_p, wts, bias_p[l], tm_in_p, tm_p, l, depth, big_p)
        caches = (cache_a_k[l], cache_a_v[l], cache_b_k[l], cache_b_v[l], cache_b_logf[l],
                  cache_c_k[l], cache_c_v[l])
        xs, st_s = _sample_layer(xs, bs, mods_s, wts, bias_s[l], caches, tm_s)
        small_p.append(st_small)
        states_s.append(st_s)

    def stack_heads(arrs, heads):
        a = jnp.stack(arrs)
        return a if heads is None else a.reshape(a.shape[:3] + (heads, HEAD_DIM))

    a_k_p, a_v_p, b_logf_p = [stack_heads(arrs, h) for arrs, h in zip(zip(*small_p), (H_A, H_A, None))]
    b_k_p, b_v_p, c_k_p, c_v_p = big_p
    sample_states = [stack_heads(arrs, h)
                     for arrs, h in zip(zip(*states_s), (H_A, H_A, H_B, H_B, None, H_C, H_C))]
    y_prompt = _final_norm(xp, g_final, 512).reshape(bp, tp, D_MODEL)
    y_sample = _final_norm(xs, g_final, bs * ts).reshape(bs, ts, D_MODEL)
    return tuple([y_prompt, y_sample, a_k_p, a_v_p, b_k_p, b_v_p, b_logf_p, c_k_p, c_v_p] + sample_states)
```

```python
import functools

import jax
import jax.numpy as jnp
from jax import lax
from jax.experimental import pallas as pl
from jax.experimental.pallas import tpu as pltpu

F32 = jnp.float32
BF16 = jnp.bfloat16

D_MODEL = 2048
HEAD_DIM = 128
N_HEADS = D_MODEL // HEAD_DIM
H_A = N_HEADS // 4
H_B = (N_HEADS - H_A) // 2
H_C = N_HEADS - H_A - H_B
CHUNK = 64
A_LEFT = 8 * CHUNK
REL_CLIP = 128
N_REL = 2 * REL_CLIP + 1
EPS = 1e-6
ATTN_SCALE = HEAD_DIM ** -0.5
LOG2E = 1.4426950408889634
Q_SCALE = ATTN_SCALE * LOG2E

V7X_VMEM_BYTES = 64 * 1024 * 1024
LANES = 128
SUBLANES = 8
HEAD_PAIR = 2 * HEAD_DIM
A_BAND = 640
VT_EXTRA = 16

PAIRS_A = H_A // 2
PAIRS_AB = (H_A + H_B) // 2
N_PAIRS = N_HEADS // 2


def _cparams(sem, vmem_mb):
    return pltpu.CompilerParams(dimension_semantics=sem,
                                vmem_limit_bytes=min(vmem_mb * 1024 * 1024, V7X_VMEM_BYTES - (4 << 20)))


def _dot(a, b):
    return jnp.dot(a, b, preferred_element_type=F32)


def _dot_nt(a, b):
    return lax.dot_general(a, b, (((1,), (1,)), ((), ())), preferred_element_type=F32)


def _sigmoid(x):
    return 1.0 / (1.0 + jnp.exp(-x))


def _log_sigmoid(x):
    return jnp.minimum(x, 0.0) - jnp.log1p(jnp.exp(-jnp.abs(x)))


def _modulated_norm(x, g, sc, sh):
    ms = jnp.mean(x * x, axis=-1, keepdims=True)
    y = x * lax.rsqrt(ms + EPS) * g
    return y * (1.0 + sc) + sh


def _norm_rows(x_ref, g_ref, sc_ref, sh_ref, h_ref, rows_chunk=64):
    tm = x_ref.shape[0]
    g = g_ref[...]
    per_row = sc_ref.shape[0] != 1

    def body(r, c):
        r0 = pl.multiple_of(r * rows_chunk, rows_chunk)
        sc = sc_ref[pl.ds(r0, rows_chunk), :] if per_row else sc_ref[...]
        sh = sh_ref[pl.ds(r0, rows_chunk), :] if per_row else sh_ref[...]
        h = _modulated_norm(x_ref[pl.ds(r0, rows_chunk), :], g, sc, sh)
        h_ref[pl.ds(r0, rows_chunk), :] = h.astype(h_ref.dtype)
        return c

    lax.fori_loop(0, tm // rows_chunk, body, 0)


def _copy_cast(src_ref, dst_ref, rows, dst_off=0, step=256):
    n = rows // step

    def body(i, c):
        r = pl.multiple_of(i * step, step)
        d = pl.multiple_of(dst_off + i * step, 16)
        dst_ref[pl.ds(d, step), :] = src_ref[pl.ds(r, step), :].astype(dst_ref.dtype)
        return c

    lax.fori_loop(0, n, body, 0)
    if rows % step:
        dst_ref[dst_off + n * step:dst_off + rows, :] = src_ref[n * step:rows, :].astype(dst_ref.dtype)


def _split3(x):
    hi = x.astype(BF16)
    r1 = x - hi.astype(F32)
    mid = r1.astype(BF16)
    lo = (r1 - mid.astype(F32)).astype(BF16)
    return hi, mid, lo


def _ada_kernel(c_ref, w_ref, b_ref, o_ref):
    c = c_ref[...]
    a = (c * _sigmoid(c)).astype(BF16)
    o_ref[...] = _dot(a, w_ref[...].astype(BF16)) + b_ref[...]


def _ada(c_all, w_ada, b_ada):
    depth, _, n = w_ada.shape
    rows = c_all.shape[0]
    tn = 1024
    return pl.pallas_call(
        _ada_kernel,
        grid=(depth, n // tn),
        in_specs=[pl.BlockSpec((rows, D_MODEL), lambda l, j: (0, 0)),
                  pl.BlockSpec((None, D_MODEL, tn), lambda l, j: (l, 0, j)),
                  pl.BlockSpec((None, 1, tn), lambda l, j: (l, 0, j))],
        out_specs=pl.BlockSpec((None, rows, tn), lambda l, j: (l, 0, j)),
        out_shape=jax.ShapeDtypeStruct((depth, rows, n), F32),
        name="ada_mod",
        compiler_params=_cparams(("arbitrary", "arbitrary"), 40),
    )(c_all, w_ada, b_ada.reshape(depth, 1, n))


def _relbias_kernel(tab_ref, o_ref, *, cq, band):
    l = pl.program_id(0)
    h = pl.program_id(1)
    qi = lax.broadcasted_iota(jnp.int32, (cq, band), 0)
    s = lax.broadcasted_iota(jnp.int32, (cq, band), 1)
    idx = jnp.clip(qi + (band - cq) - s, -REL_CLIP, REL_CLIP) + REL_CLIP
    base = l * (N_REL * H_A) + h

    def body(r, acc):
        return jnp.where(idx == r, tab_ref[base + r * H_A], acc)

    o_ref[...] = lax.fori_loop(0, N_REL, body, jnp.zeros((cq, band), F32)) * LOG2E


def _relbias(rel_bias, cq, band):
    depth = rel_bias.shape[0]
    return pl.pallas_call(
        functools.partial(_relbias_kernel, cq=cq, band=band),
        grid=(depth, H_A),
        in_specs=[pl.BlockSpec(memory_space=pltpu.SMEM)],
        out_specs=pl.BlockSpec((None, None, cq, band), lambda l, h: (l, h, 0, 0)),
        out_shape=jax.ShapeDtypeStruct((depth, H_A, cq, band), F32),
        name="rel_bias",
    )(rel_bias.reshape(-1))


def _inproj_kernel(*refs, n_prev, head_major):
    x_ref, sc_ref, sh_ref, g_ref, wq_ref, wk_ref, wv_ref, wf_ref, bf_ref = refs[:9]
    outs = refs[9 + n_prev:-1]
    h_scr = refs[-1]
    q_ref, ka_ref, va_ref, kb_ref, vb_ref, kc_ref, vc_ref, lf_ref = outs[:8]
    j = pl.program_id(1)

    @pl.when(j == 0)
    def _():
        _norm_rows(x_ref, g_ref, sc_ref, sh_ref, h_scr)
        lf_ref[...] = _log_sigmoid(_dot(h_scr[...], wf_ref[...]) + bf_ref[...])

    h = h_scr[...]
    q_scale = jnp.where(j >= PAIRS_AB, -Q_SCALE, Q_SCALE)
    q_ref[...] = (_dot(h, wq_ref[...]) * q_scale).astype(BF16)
    kt = _dot(h, wk_ref[...])
    vt = _dot(h, wv_ref[...])

    @pl.when(j < PAIRS_A)
    def _():
        ka_ref[...] = kt
        va_ref[...] = vt

    def put(ref, tile):
        if head_major:
            for c in range(2):
                ref[c] = tile[:, c * HEAD_DIM:(c + 1) * HEAD_DIM]
        else:
            ref[...] = tile

    @pl.when((j >= PAIRS_A) & (j < PAIRS_AB))
    def _():
        put(kb_ref, kt)
        put(vb_ref, vt)

    @pl.when(j >= PAIRS_AB)
    def _():
        put(kc_ref, kt)
        put(vc_ref, vt)


def _mod_spec(mod, tm, tn, tiles_per_batch):
    r = mod.shape[1]
    col = (lambda j: 0) if tn == mod.shape[2] else (lambda j: j)
    if r == 1:
        return pl.BlockSpec((None, 1, tn), lambda i, j: (i // tiles_per_batch, 0, col(j)))
    return pl.BlockSpec((None, tm, tn), lambda i, j: (0, i, col(j)))


def _inproj(x2, sc, sh, g, w_qkv, w_f, b_f, tm, tiles_per_batch, state=None):
    m = x2.shape[0]
    tn = HEAD_PAIR

    def clamp(lo, n):
        return lambda i, j: (i, jnp.clip(j - lo, 0, n - 1))

    def full_mod(a):
        return _mod_spec(a, tm, D_MODEL, tiles_per_batch)

    groups = ((0, PAIRS_A), (PAIRS_A, PAIRS_AB - PAIRS_A), (PAIRS_AB, N_PAIRS - PAIRS_AB))
    out_shape = [jax.ShapeDtypeStruct((m, D_MODEL), BF16)]
    out_specs = [pl.BlockSpec((tm, tn), lambda i, j: (i, j))]
    for lo, n in groups:
        for _ in range(2):
            if state is None or lo == 0:
                out_shape.append(jax.ShapeDtypeStruct((m, n * tn), F32))
                out_specs.append(pl.BlockSpec((tm, tn), clamp(lo, n)))
            else:
                layer, depth, batch, _ = state
                out_shape.append(jax.ShapeDtypeStruct((depth, batch, 2 * n, m // batch, HEAD_DIM), F32))
                out_specs.append(pl.BlockSpec(
                    (None, None, 2, tm, HEAD_DIM),
                    lambda i, j, lo=lo, n=n: (layer, i // tiles_per_batch, jnp.clip(j - lo, 0, n - 1),
                                              i % tiles_per_batch, 0)))
    out_shape.append(jax.ShapeDtypeStruct((m, LANES), F32))
    out_specs.append(pl.BlockSpec((tm, LANES), lambda i, j: (i, 0)))
    in_specs = [pl.BlockSpec((tm, D_MODEL), lambda i, j: (i, 0)),
                full_mod(sc), full_mod(sh),
                pl.BlockSpec((1, D_MODEL), lambda i, j: (0, 0)),
                pl.BlockSpec((D_MODEL, tn), lambda i, j: (0, j)),
                pl.BlockSpec((D_MODEL, tn), lambda i, j: (0, N_PAIRS + j)),
                pl.BlockSpec((D_MODEL, tn), lambda i, j: (0, 2 * N_PAIRS + j)),
                pl.BlockSpec((D_MODEL, LANES), lambda i, j: (0, 0)),
                pl.BlockSpec((1, LANES), lambda i, j: (0, 0))]
    args = [x2, sc, sh, g, w_qkv, w_qkv, w_qkv, w_f, b_f]
    aliases = {}
    prev = None if state is None else state[3]
    if prev is not None:
        first_state_out = 3
        for idx, arr in enumerate(prev):
            in_specs.append(pl.BlockSpec(memory_space=pl.ANY))
            aliases[len(args)] = first_state_out + idx
            args.append(arr)
    q, ka, va, kb, vb, kc, vc, lf = pl.pallas_call(
        functools.partial(_inproj_kernel, n_prev=0 if prev is None else len(prev), head_major=state is not None),
        grid=(m // tm, N_PAIRS),
        in_specs=in_specs,
        out_specs=out_specs,
        out_shape=out_shape,
        input_output_aliases=aliases,
        scratch_shapes=[pltpu.VMEM((tm, D_MODEL), BF16)],
        name="in_proj",
        compiler_params=_cparams(("arbitrary", "arbitrary"), 52),
    )(*args)
    return q, ka, kb, kc, va, vb, vc, lf


def _dot_f32_by_01(x, ones01):
    hi, mid, lo = _split3(x)
    return _dot(hi, ones01) + _dot(mid, ones01) + _dot(lo, ones01)


def _cumsum_kernel(x_ref, o_ref):
    x = x_ref[...]
    rows = x.shape[0]
    i0 = lax.broadcasted_iota(jnp.int32, (LANES, LANES), 0)
    i1 = lax.broadcasted_iota(jnp.int32, (LANES, LANES), 1)
    upper = (i0 <= i1).astype(BF16)
    c = _dot_f32_by_01(x, upper)
    tot = jnp.broadcast_to(c[:, LANES - 1:LANES], (rows, LANES))
    r0 = lax.broadcasted_iota(jnp.int32, (rows, rows), 0)
    r1 = lax.broadcasted_iota(jnp.int32, (rows, rows), 1)
    lower = (r1 < r0).astype(BF16)
    hi, mid, lo = _split3(tot)
    off = _dot(lower, hi) + _dot(lower, mid) + _dot(lower, lo)
    o_ref[...] = (c + off) * LOG2E


def _cumsum_time(logf_hm):
    b, h, t = logf_hm.shape
    rows = t // LANES
    out = pl.pallas_call(
        _cumsum_kernel,
        grid=(b * h,),
        in_specs=[pl.BlockSpec((None, rows, LANES), lambda i: (i, 0, 0))],
        out_specs=pl.BlockSpec((None, rows, LANES), lambda i: (i, 0, 0)),
        out_shape=jax.ShapeDtypeStruct((b * h, rows, LANES), F32),
        name="forget_cumsum",
    )(logf_hm.reshape(b * h, rows, LANES))
    return out.reshape(b, h, t)


def _band_kernel(q_ref, k_ref, v_ref, bias_ref, o_ref, kp_scr, vp_scr, *, cq, npad, nmask, nchunks):
    tk = k_ref.shape[0]
    kp_scr[0:npad, :] = jnp.zeros((npad, HEAD_DIM), BF16)
    vp_scr[0:npad, :] = jnp.zeros((npad, HEAD_DIM), BF16)
    _copy_cast(k_ref, kp_scr, tk, dst_off=npad)
    _copy_cast(v_ref, vp_scr, tk, dst_off=npad)
    bias = bias_ref[...]
    slot = lax.broadcasted_iota(jnp.int32, (cq, A_BAND), 1)
    per_step = min(nchunks, 8)

    def step(i, carry):
        rows = [pl.multiple_of((i * per_step + u) * cq, cq) for u in range(per_step)]
        scores = [_dot_nt(q_ref[pl.ds(r0, cq), :], kp_scr[pl.ds(r0, A_BAND), :]) for r0 in rows]
        probs, denoms = [], []
        for u in range(per_step):
            first_valid = jnp.maximum(nmask, npad - (i * per_step + u) * cq)
            s = jnp.where(slot >= first_valid, scores[u] + bias, -jnp.inf)
            p = jnp.exp2(s - jnp.max(s, axis=-1, keepdims=True))
            denoms.append(jnp.sum(p, axis=-1, keepdims=True))
            probs.append(p.astype(BF16))
        for u in range(per_step):
            o = _dot(probs[u], vp_scr[pl.ds(rows[u], A_BAND), :]) / denoms[u]
            o_ref[pl.ds(rows[u], cq), :] = o.astype(o_ref.dtype)
        return carry

    lax.fori_loop(0, nchunks // per_step, step, 0)


def _band_attn(q, k, v, bias, cq, npad, nmask):
    b, t, _ = q.shape
    tk = k.shape[1]
    kern = functools.partial(_band_kernel, cq=cq, npad=npad, nmask=nmask, nchunks=t // cq)
    return pl.pallas_call(
        kern,
        grid=(b, H_A),
        in_specs=[pl.BlockSpec((None, t, HEAD_DIM), lambda i, h: (i, 0, h)),
                  pl.BlockSpec((None, tk, HEAD_DIM), lambda i, h: (i, 0, h)),
                  pl.BlockSpec((None, tk, HEAD_DIM), lambda i, h: (i, 0, h)),
                  pl.BlockSpec((None, cq, A_BAND), lambda i, h: (h, 0, 0))],
        out_specs=pl.BlockSpec((None, t, HEAD_DIM), lambda i, h: (i, 0, h)),
        out_shape=jax.ShapeDtypeStruct((b, t, H_A * HEAD_DIM), BF16),
        scratch_shapes=[pltpu.VMEM((npad + tk, HEAD_DIM), BF16),
                        pltpu.VMEM((npad + tk, HEAD_DIM), BF16)],
        name="band_attn",
        compiler_params=_cparams(("arbitrary", "arbitrary"), 48),
    )(q, k, v, bias)


def _col_replicate(row):
    n = row.shape[1]
    parts = [jnp.broadcast_to(row[:, c * LANES:(c + 1) * LANES], (LANES, LANES)).T for c in range(n // LANES)]
    return parts[0] if len(parts) == 1 else jnp.concatenate(parts, axis=0)


def _aug_columns(col_rep, first):
    hi, mid, lo = _split3(col_rep)
    lane = lax.broadcasted_iota(jnp.int32, col_rep.shape, 1)
    base = 0 if first else 3
    one = jnp.where((lane >= 3 - base) & (lane < 6 - base), 1.0, 0.0)
    x = jnp.where(lane == base, hi.astype(F32),
                  jnp.where(lane == base + 1, mid.astype(F32),
                            jnp.where(lane == base + 2, lo.astype(F32), one)))
    return x.astype(BF16)


def _fox_kernel(q_ref, k_ref, v_ref, g_ref, o_ref, kp_scr, vt_scr, qp_scr, acc_scr, x_scr,
                *, tq, tk, nq, qoff, ng):
    nblk = kp_scr.shape[0]
    ones_rows = (lax.broadcasted_iota(jnp.int32, (VT_EXTRA, tk), 0) == 0).astype(BF16)

    def prep(kb, c):
        r0 = pl.multiple_of(kb * tk, tk)
        kp_scr[kb, :, 0:HEAD_DIM] = k_ref[pl.ds(r0, tk), :].astype(BF16)
        kp_scr[kb, :, HEAD_DIM:2 * HEAD_DIM] = _aug_columns(_col_replicate(-g_ref[kb]), True)
        vt_scr[kb, 0:HEAD_DIM, :] = v_ref[pl.ds(r0, tk), :].astype(F32).T.astype(BF16)
        vt_scr[kb, HEAD_DIM:HEAD_DIM + VT_EXTRA, :] = ones_rows
        return c

    lax.fori_loop(0, nblk, prep, 0)
    off = qoff % tk
    row = lax.broadcasted_iota(jnp.int32, (tk, tq), 0)
    col = lax.broadcasted_iota(jnp.int32, (tk, tq), 1)

    def scores(g, slot, p0):
        qp = qp_scr[...]
        m_blk = None
        for u in range(ng):
            x = _dot_nt(kp_scr[g * ng + u], qp)
            if p0 is not None:
                x = jnp.where(row + ((g * ng + u) * tk - p0) <= col, x, -jnp.inf)
            x_scr[slot, u] = x
            m_u = jnp.max(x, axis=0, keepdims=True)
            m_blk = m_u if m_blk is None else jnp.maximum(m_blk, m_u)
        return m_blk

    def process(g, slot, m_cur, m_next):
        acc = acc_scr[...]
        for u in range(ng):
            p = jnp.exp2(x_scr[slot, u] - m_cur)
            acc = acc + _dot(vt_scr[g * ng + u], p.astype(BF16))
        acc_scr[...] = acc * jnp.exp2(m_cur - m_next)

    def qblock(i, carry):
        q0 = pl.multiple_of(i * tq, tq)
        p0 = qoff + i * tq
        pb = p0 // tk
        qp_scr[:, 0:HEAD_DIM] = q_ref[pl.ds(q0, tq), :]
        if tq <= tk:
            gq = g_ref[pb][:, off:off + tq]
        else:
            gq = jnp.concatenate([g_ref[pb + c] for c in range(tq // tk)], axis=1)
        qp_scr[:, HEAD_DIM:2 * HEAD_DIM] = _aug_columns(_col_replicate(gq), False)
        acc_scr[...] = jnp.zeros((HEAD_DIM + VT_EXTRA, tq), F32)
        gd = pb // ng
        m0 = scores(gd, 0, p0)
        m_pre = scores(jnp.maximum(gd - 1, 0), 1, None)
        m1 = jnp.where(gd >= 1, jnp.maximum(m0, m_pre), m0)
        process(gd, 0, m0, m1)

        def pair(pi, m_run):
            g1 = gd - 1 - 2 * pi
            m_a = jnp.maximum(m_run, scores(g1 - 1, 0, None))
            process(g1, 1, m_run, m_a)
            m_pre = scores(jnp.maximum(g1 - 2, 0), 1, None)
            m_b = jnp.where(g1 >= 2, jnp.maximum(m_a, m_pre), m_a)
            process(g1 - 1, 0, m_a, m_b)
            return m_b

        m_last = lax.fori_loop(0, gd // 2, pair, m1)

        @pl.when(gd % 2 == 1)
        def _():
            process(0, 1, m_last, m_last)

        denom = acc_scr[HEAD_DIM:HEAD_DIM + 1, :]
        o_ref[pl.ds(q0, tq), :] = (acc_scr[0:HEAD_DIM, :] / denom).T.astype(o_ref.dtype)
        return carry

    lax.fori_loop(0, nq, qblock, 0)


def _stick_kernel(q_ref, k_ref, v_ref, o_ref, kp_scr, vt_scr, acc_scr, x_scr, *, tq, tk, nq, qoff, ng):
    nblk = kp_scr.shape[0]
    nv = tk // SUBLANES

    r = lax.broadcasted_iota(jnp.int32, (tk, tk), 0)
    kk = lax.broadcasted_iota(jnp.int32, (tk, tk), 1)
    perm = ((r % SUBLANES) * nv + r // SUBLANES == kk).astype(BF16)
    perm_t = ((kk % SUBLANES) * nv + kk // SUBLANES == r).astype(BF16)

    def prep(kb, c):
        r0 = pl.multiple_of(kb * tk, tk)
        kp_scr[kb] = _dot(perm, k_ref[pl.ds(r0, tk), :].astype(BF16)).astype(BF16)
        vt_scr[kb] = _dot(v_ref[pl.ds(r0, tk), :].astype(F32).T.astype(BF16), perm_t).astype(BF16)
        return c

    lax.fori_loop(0, nblk, prep, 0)
    row = lax.broadcasted_iota(jnp.int32, (tk, tq), 0)
    col = lax.broadcasted_iota(jnp.int32, (tk, tq), 1)
    key_in_block = (row % SUBLANES) * nv + row // SUBLANES
    sub = lax.broadcasted_iota(jnp.int32, (SUBLANES, tq), 0)

    def scores(q, g, slot):
        for u in range(ng):
            x_scr[slot, u] = _dot_nt(kp_scr[g * ng + u], q)

    def tile_terms(x, kb, p0):
        beta = 1.0 / (1.0 + jnp.exp2(x))
        stay = 1.0 - beta
        valid = None
        if p0 is not None:
            valid = key_in_block + (kb * tk - p0) < col
            stay = jnp.where(valid, stay, 1.0)
        after = [None] * nv
        run = jnp.ones((SUBLANES, tq), F32)
        for v in range(nv - 1, -1, -1):
            after[v] = run
            run = run * stay[v * SUBLANES:(v + 1) * SUBLANES, :]
        incl = run
        for k in (1, 2, 4):
            incl = incl * jnp.where(sub + k < SUBLANES, pltpu.roll(incl, SUBLANES - k, axis=0), 1.0)
        later = jnp.where(sub + 1 < SUBLANES, pltpu.roll(incl, SUBLANES - 1, axis=0), 1.0)
        return after, later, incl[0:1, :], beta, valid

    def process(g, slot, carry, p0):
        terms = [tile_terms(x_scr[slot, u], g * ng + u, p0) for u in range(ng)]
        acc = acc_scr[...]
        for u in range(ng - 1, -1, -1):
            after, later, total, beta, valid = terms[u]
            scale_u = later * carry
            w = jnp.concatenate([after[v] * scale_u for v in range(nv)], axis=0) * beta
            if valid is not None:
                w = jnp.where(valid, w, 0.0)
            acc = acc + _dot(vt_scr[g * ng + u], w.astype(BF16))
            carry = carry * total
        acc_scr[...] = acc
        return carry

    def qblock(i, c0):
        q0 = pl.multiple_of(i * tq, tq)
        p0 = qoff + i * tq
        gd = (p0 // tk) // ng
        q = q_ref[pl.ds(q0, tq), :]
        acc_scr[...] = jnp.zeros((HEAD_DIM, tq), F32)
        scores(q, gd, 0)
        scores(q, jnp.maximum(gd - 1, 0), 1)
        carry = process(gd, 0, jnp.ones((1, tq), F32), p0)

        def pair(pi, carry):
            g1 = gd - 1 - 2 * pi
            scores(q, g1 - 1, 0)
            carry = process(g1, 1, carry, None)
            scores(q, jnp.maximum(g1 - 2, 0), 1)
            return process(g1 - 1, 0, carry, None)

        carry = lax.fori_loop(0, gd // 2, pair, carry)

        @pl.when(gd % 2 == 1)
        def _():
            process(0, 1, carry, None)

        o_ref[pl.ds(q0, tq), :] = acc_scr[...].T.astype(o_ref.dtype)
        return c0

    lax.fori_loop(0, nq, qblock, 0)


def _causal_attn(kind, q, k, v, f, head0, tq, tk, qoff, ng, layer=None):
    b, t, _ = q.shape
    if layer is None:
        tkv = k.shape[1]
        nh = k.shape[2] // HEAD_DIM
        kv_spec = pl.BlockSpec((None, tkv, HEAD_DIM), lambda i, h: (i, 0, h))
    else:
        nh, tkv = k.shape[2], k.shape[3]
        kv_spec = pl.BlockSpec((None, None, None, tkv, HEAD_DIM), lambda i, h: (layer, i, h, 0, 0))
    nblk = tkv // tk
    assert tkv % tk == 0 and nblk % ng == 0 and tq % LANES == 0 and t % tq == 0
    assert (ng * tk) % tq == 0 and qoff % tq == 0 and (tq % tk == 0 or (qoff % tk) + tq <= tk and t == tq)
    common = dict(tq=tq, tk=tk, nq=t // tq, qoff=qoff, ng=ng)
    in_specs = [pl.BlockSpec((None, t, HEAD_DIM), lambda i, h: (i, 0, head0 + h)), kv_spec, kv_spec]
    if kind == "fox":
        kern = functools.partial(_fox_kernel, **common)
        g = f.reshape(b, nh, nblk, 1, tk)
        in_specs.append(pl.BlockSpec((None, None, nblk, 1, tk), lambda i, h: (i, h, 0, 0, 0)))
        scratch = [pltpu.VMEM((nblk, tk, 2 * HEAD_DIM), BF16), pltpu.VMEM((nblk, HEAD_DIM + VT_EXTRA, tk), BF16),
                   pltpu.VMEM((tq, 2 * HEAD_DIM), BF16), pltpu.VMEM((HEAD_DIM + VT_EXTRA, tq), F32),
                   pltpu.VMEM((2, ng, tk, tq), F32)]
        args = (q, k, v, g)
    else:
        kern = functools.partial(_stick_kernel, **common)
        scratch = [pltpu.VMEM((nblk, tk, HEAD_DIM), BF16), pltpu.VMEM((nblk, HEAD_DIM, tk), BF16),
                   pltpu.VMEM((HEAD_DIM, tq), F32), pltpu.VMEM((2, ng, tk, tq), F32)]
        args = (q, k, v)
    return pl.pallas_call(
        kern,
        grid=(b, nh),
        in_specs=in_specs,
        out_specs=pl.BlockSpec((None, t, HEAD_DIM), lambda i, h: (i, 0, h)),
        out_shape=jax.ShapeDtypeStruct((b, t, nh * HEAD_DIM), BF16),
        scratch_shapes=scratch,
        name=kind + "_attn",
        compiler_params=_cparams(("arbitrary", "arbitrary"), 52),
    )(*args)


def _outproj_kernel(oa_ref, ob_ref, oc_ref, wa_ref, wb_ref, wc_ref, x_ref, gt_ref, y_ref):
    acc = _dot(oa_ref[...], wa_ref[...]) + _dot(ob_ref[...], wb_ref[...]) + _dot(oc_ref[...], wc_ref[...])
    y_ref[...] = x_ref[...] + gt_ref[...] * acc


def _outproj(oa, ob, oc, wa, wb, wc, x2, gt, tm, tiles_per_batch):
    m = x2.shape[0]
    tn = 512
    row = lambda a: pl.BlockSpec((tm, a.shape[1]), lambda i, j: (i, 0))
    wsp = lambda a: pl.BlockSpec((a.shape[0], tn), lambda i, j: (0, j))
    return pl.pallas_call(
        _outproj_kernel,
        grid=(m // tm, D_MODEL // tn),
        in_specs=[row(oa), row(ob), row(oc), wsp(wa), wsp(wb), wsp(wc),
                  pl.BlockSpec((tm, tn), lambda i, j: (i, j)),
                  _mod_spec(gt, tm, tn, tiles_per_batch)],
        out_specs=pl.BlockSpec((tm, tn), lambda i, j: (i, j)),
        out_shape=jax.ShapeDtypeStruct((m, D_MODEL), F32),
        name="out_proj",
        compiler_params=_cparams(("arbitrary", "arbitrary"), 40),
    )(oa, ob, oc, wa, wb, wc, x2, gt)


def _ffn1_kernel(x_ref, sc_ref, sh_ref, g_ref, wg_ref, wu_ref, a_ref, h_scr):
    @pl.when(pl.program_id(1) == 0)
    def _():
        _norm_rows(x_ref, g_ref, sc_ref, sh_ref, h_scr)

    h = h_scr[...]
    gate = _dot(h, wg_ref[...])
    up = _dot(h, wu_ref[...])
    a_ref[...] = (gate * _sigmoid(gate) * up).astype(BF16)


def _ffn1(x2, sc, sh, g, wg, wu, tm, tiles_per_batch):
    m = x2.shape[0]
    dff = wg.shape[1]
    tn = 512
    full_mod = lambda a: _mod_spec(a, tm, D_MODEL, tiles_per_batch)
    return pl.pallas_call(
        _ffn1_kernel,
        grid=(m // tm, dff // tn),
        in_specs=[pl.BlockSpec((tm, D_MODEL), lambda i, j: (i, 0)),
                  full_mod(sc), full_mod(sh),
                  pl.BlockSpec((1, D_MODEL), lambda i, j: (0, 0)),
                  pl.BlockSpec((D_MODEL, tn), lambda i, j: (0, j)),
                  pl.BlockSpec((D_MODEL, tn), lambda i, j: (0, j))],
        out_specs=pl.BlockSpec((tm, tn), lambda i, j: (i, j)),
        out_shape=jax.ShapeDtypeStruct((m, dff), BF16),
        scratch_shapes=[pltpu.VMEM((tm, D_MODEL), BF16)],
        name="ffn_gate_up",
        compiler_params=_cparams(("arbitrary", "arbitrary"), 44),
    )(x2, sc, sh, g, wg, wu)


def _ffn2_kernel(a_ref, wd_ref, x_ref, gt_ref, y_ref):
    y_ref[...] = x_ref[...] + gt_ref[...] * _dot(a_ref[...], wd_ref[...])


def _ffn2(a, wd, x2, gt, tm, tiles_per_batch):
    m, dff = a.shape
    tn = 256
    return pl.pallas_call(
        _ffn2_kernel,
        grid=(m // tm, D_MODEL // tn),
        in_specs=[pl.BlockSpec((tm, dff), lambda i, j: (i, 0)),
                  pl.BlockSpec((dff, tn), lambda i, j: (0, j)),
                  pl.BlockSpec((tm, tn), lambda i, j: (i, j)),
                  _mod_spec(gt, tm, tn, tiles_per_batch)],
        out_specs=pl.BlockSpec((tm, tn), lambda i, j: (i, j)),
        out_shape=jax.ShapeDtypeStruct((m, D_MODEL), F32),
        name="ffn_down",
        compiler_params=_cparams(("arbitrary", "arbitrary"), 48),
    )(a, wd, x2, gt)


def _final_norm_kernel(x_ref, g_ref, y_ref, *, rows_chunk=64):
    g = g_ref[...]

    def body(r, c):
        r0 = pl.multiple_of(r * rows_chunk, rows_chunk)
        x = x_ref[pl.ds(r0, rows_chunk), :]
        ms = jnp.mean(x * x, axis=-1, keepdims=True)
        y_ref[pl.ds(r0, rows_chunk), :] = x * lax.rsqrt(ms + EPS) * g
        return c

    lax.fori_loop(0, x_ref.shape[0] // rows_chunk, body, 0)


def _final_norm(x2, g, tm):
    m = x2.shape[0]
    return pl.pallas_call(
        _final_norm_kernel,
        grid=(m // tm,),
        in_specs=[pl.BlockSpec((tm, D_MODEL), lambda i: (i, 0)),
                  pl.BlockSpec((1, D_MODEL), lambda i: (0, 0))],
        out_specs=pl.BlockSpec((tm, D_MODEL), lambda i: (i, 0)),
        out_shape=jax.ShapeDtypeStruct((m, D_MODEL), F32),
        name="final_norm",
        compiler_params=_cparams(("arbitrary",), 40),
    )(x2, g.reshape(1, D_MODEL))


def _pad_time(a, total):
    pad = total - a.shape[1]
    if pad == 0:
        return a
    return jnp.pad(a, ((0, 0), (0, pad)) + ((0, 0),) * (a.ndim - 2))


def _mix_and_ffn(x2, oa, ob, oc, mods, wts, tm, tiles_per_batch):
    _, _, gt1, sh2, sc2, gt2 = mods
    m = x2.shape[0]
    r2 = lambda a: a.reshape(m, a.shape[-1])
    x2 = _outproj(r2(oa), r2(ob), r2(oc), wts["wo_a"], wts["wo_b"], wts["wo_c"], x2, gt1, tm, tiles_per_batch)
    a = _ffn1(x2, sc2, sh2, wts["g_ffn"], wts["w_gate"], wts["w_up"], tm, tiles_per_batch)
    return _ffn2(a, wts["w_down"], x2, gt2, tm, tiles_per_batch)


def _prompt_layer(x2, batch, mods, wts, bias, tm, layer, depth, prev_state):
    m = x2.shape[0]
    t = m // batch
    tiles_per_batch = t // tm
    sh1, sc1 = mods[0], mods[1]
    q, ka, kb, kc, va, vb, vc, lf = _inproj(
        x2, sc1, sh1, wts["g_attn"], wts["w_qkv"], wts["w_f"], wts["b_f"], tm, tiles_per_batch,
        state=(layer, depth, batch, prev_state))
    r3 = lambda a: a.reshape(batch, t, a.shape[-1])
    q, ka, va = map(r3, (q, ka, va))
    logf = r3(lf)[:, :, :H_B]
    keep = min(A_LEFT, t)
    small_state = (ka[:, t - keep:], va[:, t - keep:], logf)
    f = _cumsum_time(jnp.swapaxes(logf, 1, 2))
    oa = _band_attn(q, ka, va, bias, CHUNK, A_BAND - CHUNK, A_BAND - CHUNK - A_LEFT)
    tq, tk, ng = 1024, 256, 4
    ob = _causal_attn("fox", q, kb, vb, f, H_A, tq, tk, 0, ng, layer=layer)
    oc = _causal_attn("stick", q, kc, vc, None, H_A + H_B, tq, tk, 0, ng, layer=layer)
    return _mix_and_ffn(x2, oa, ob, oc, mods, wts, tm, tiles_per_batch), small_state, (kb, vb, kc, vc)


def _sample_layer(x2, batch, mods, wts, bias, caches, tm):
    m = x2.shape[0]
    t = m // batch
    sh1, sc1 = mods[0], mods[1]
    q, ka, kb, kc, va, vb, vc, lf = _inproj(x2, sc1, sh1, wts["g_attn"], wts["w_qkv"], wts["w_f"], wts["b_f"], tm, 1)
    r3 = lambda a: a.reshape(batch, t, a.shape[-1])
    q, ka, kb, kc, va, vb, vc = map(r3, (q, ka, kb, kc, va, vb, vc))
    logf = r3(lf)[:, :, :H_B]
    cak, cav, cbk, cbv, cblogf, cck, ccv = caches
    state = (ka, va, kb, vb, logf, kc, vc)
    flat = lambda c: c.reshape(c.shape[0], c.shape[1], -1)
    past = cbk.shape[1]
    tk = 256
    tkv = past + tk

    def cat(cache, new):
        heads = cache.shape[2]
        new_hm = jnp.swapaxes(new.reshape(batch, t, heads, HEAD_DIM), 1, 2)
        both = jnp.concatenate([jnp.swapaxes(cache, 1, 2), new_hm], axis=2)
        return jnp.pad(both, ((0, 0), (0, 0), (0, tkv - both.shape[2]), (0, 0)))[None]

    kka = jnp.concatenate([flat(cak), ka], axis=1)
    vva = jnp.concatenate([flat(cav), va], axis=1)
    wa = kka.shape[1]
    oa = _band_attn(q, kka, vva, bias, t, A_BAND - wa, A_BAND - wa)
    lcat = jnp.concatenate([cblogf.astype(F32), logf], axis=1)
    fpad = -(-tkv // (SUBLANES * LANES)) * (SUBLANES * LANES)
    f = _cumsum_time(jnp.swapaxes(_pad_time(lcat, fpad), 1, 2))[:, :, :tkv]
    qpad = _pad_time(q, LANES)
    ng = tkv // tk
    ob = _causal_attn("fox", qpad, cat(cbk, kb), cat(cbv, vb), f, H_A, LANES, tk, past, ng, layer=0)[:, :t]
    oc = _causal_attn("stick", qpad, cat(cck, kc), cat(ccv, vc), None, H_A + H_B, LANES, tk, past, ng,
                      layer=0)[:, :t]
    return _mix_and_ffn(x2, oa, ob, oc, mods, wts, tm, 1), state


def kernel(x_prompt, x_sample, cache_a_k, cache_a_v, cache_b_k, cache_b_v, cache_b_logf, cache_c_k, cache_c_v,
           c_prompt, c_sample, w_ada, b_ada, g_attn, g_ffn, w_in, b_f, rel_bias, w_o, w_gate, w_up, w_down, g_final):
    depth = w_ada.shape[0]
    bp, tp, _ = x_prompt.shape
    bs, ts, _ = x_sample.shape
    n_qkv = 3 * D_MODEL

    rows = -(-(bp + bs) // SUBLANES) * SUBLANES
    c_all = jnp.concatenate([c_prompt, c_sample, jnp.zeros((rows - bp - bs, D_MODEL), F32)], axis=0)
    mod = _ada(c_all, w_ada, b_ada)

    bias_p = _relbias(rel_bias, CHUNK, A_BAND)
    bias_s = _relbias(rel_bias, ts, A_BAND)

    xp = x_prompt.reshape(bp * tp, D_MODEL)
    xs = x_sample.reshape(bs * ts, D_MODEL)
    tm_p, tm_s = 1024, bs * ts
    small_p, states_s = [], []
    big_p = None
    ea = H_A * HEAD_DIM
    eb = (H_A + H_B) * HEAD_DIM
    for l in range(depth):
        wo = w_o[l].astype(BF16)
        wts = dict(
            g_attn=g_attn[l].reshape(1, D_MODEL), g_ffn=g_ffn[l].reshape(1, D_MODEL),
            w_qkv=w_in[l, :, :n_qkv].astype(BF16),
            w_f=jnp.pad(w_in[l, :, n_qkv:], ((0, 0), (0, LANES - H_B))).astype(BF16),
            b_f=jnp.pad(b_f[l], (0, LANES - H_B)).reshape(1, LANES),
            wo_a=wo[:ea], wo_b=wo[ea:eb], wo_c=wo[eb:],
            w_gate=w_gate[l].astype(BF16), w_up=w_up[l].astype(BF16), w_down=w_down[l].astype(BF16))
        chunks = jnp.split(mod[l], 6, axis=-1)
        mods_p = [c[:bp].reshape(bp, 1, D_MODEL) for c in chunks]
        mods_s = [jnp.repeat(c[bp:bp + bs], ts, axis=0).reshape(1, bs * ts, D_MODEL) for c in chunks]
        xp, st_small, big_p = _prompt_layer(xp, bp, mods_p, wts, bias_p[l], tm_p, l, depth, big_p)
        caches = (cache_a_k[l], cache_a_v[l], cache_b_k[l], cache_b_v[l], cache_b_logf[l],
                  cache_c_k[l], cache_c_v[l])
        xs, st_s = _sample_layer(xs, bs, mods_s, wts, bias_s[l], caches, tm_s)
        small_p.append(st_small)
        states_s.append(st_s)

    def stack_heads(arrs, heads):
        a = jnp.stack(arrs)
        return a if heads is None else a.reshape(a.shape[:3] + (heads, HEAD_DIM))

    a_k_p, a_v_p, b_logf_p = [stack_heads(arrs, h) for arrs, h in zip(zip(*small_p), (H_A, H_A, None))]
    b_k_p, b_v_p, c_k_p, c_v_p = [jnp.swapaxes(a, 2, 3) for a in big_p]
    sample_states = [stack_heads(arrs, h)
                     for arrs, h in zip(zip(*states_s), (H_A, H_A, H_B, H_B, None, H_C, H_C))]
    y_prompt = _final_norm(xp, g_final, 512).reshape(bp, tp, D_MODEL)
    y_sample = _final_norm(xs, g_final, bs * ts).reshape(bs, ts, D_MODEL)
    return tuple([y_prompt, y_sample, a_k_p, a_v_p, b_k_p, b_v_p, b_logf_p, c_k_p, c_v_p] + sample_states)
```

```python
import functools

import jax
import jax.numpy as jnp
from jax import lax
from jax.experimental import pallas as pl
from jax.experimental.pallas import tpu as pltpu

F32 = jnp.float32
BF16 = jnp.bfloat16

D_MODEL = 2048
HEAD_DIM = 128
N_HEADS = D_MODEL // HEAD_DIM
H_A = N_HEADS // 4
H_B = (N_HEADS - H_A) // 2
H_C = N_HEADS - H_A - H_B
CHUNK = 64
A_LEFT = 8 * CHUNK
REL_CLIP = 128
N_REL = 2 * REL_CLIP + 1
EPS = 1e-6
ATTN_SCALE = HEAD_DIM ** -0.5
LOG2E = 1.4426950408889634
Q_SCALE = ATTN_SCALE * LOG2E

V7X_VMEM_BYTES = 64 * 1024 * 1024
LANES = 128
SUBLANES = 8
HEAD_PAIR = 2 * HEAD_DIM
A_BAND = 640
VT_EXTRA = 16

PAIRS_A = H_A // 2
PAIRS_AB = (H_A + H_B) // 2
N_PAIRS = N_HEADS // 2


def _cparams(sem, vmem_mb):
    return pltpu.CompilerParams(dimension_semantics=sem,
                                vmem_limit_bytes=min(vmem_mb * 1024 * 1024, V7X_VMEM_BYTES - (4 << 20)))


def _dot(a, b):
    return jnp.dot(a, b, preferred_element_type=F32)


def _dot_nt(a, b):
    return lax.dot_general(a, b, (((1,), (1,)), ((), ())), preferred_element_type=F32)


def _sigmoid(x):
    return 1.0 / (1.0 + jnp.exp(-x))


def _log_sigmoid(x):
    return jnp.minimum(x, 0.0) - jnp.log1p(jnp.exp(-jnp.abs(x)))


def _modulated_norm(x, g, sc, sh):
    ms = jnp.mean(x * x, axis=-1, keepdims=True)
    y = x * lax.rsqrt(ms + EPS) * g
    return y * (1.0 + sc) + sh


def _norm_rows(x_ref, g_ref, sc_ref, sh_ref, h_ref, rows_chunk=128):
    tm = x_ref.shape[0]
    rows_chunk = min(rows_chunk, tm)
    assert tm % rows_chunk == 0
    g = g_ref[...]
    per_row = sc_ref.shape[0] != 1

    def body(r, c):
        r0 = pl.multiple_of(r * rows_chunk, rows_chunk)
        sc = sc_ref[pl.ds(r0, rows_chunk), :] if per_row else sc_ref[...]
        sh = sh_ref[pl.ds(r0, rows_chunk), :] if per_row else sh_ref[...]
        h = _modulated_norm(x_ref[pl.ds(r0, rows_chunk), :], g, sc, sh)
        h_ref[pl.ds(r0, rows_chunk), :] = h.astype(h_ref.dtype)
        return c

    lax.fori_loop(0, tm // rows_chunk, body, 0)


def _copy_cast(src_ref, dst_ref, rows, dst_off=0, step=256):
    n = rows // step

    def body(i, c):
        r = pl.multiple_of(i * step, step)
        d = pl.multiple_of(dst_off + i * step, 16)
        dst_ref[pl.ds(d, step), :] = src_ref[pl.ds(r, step), :].astype(dst_ref.dtype)
        return c

    lax.fori_loop(0, n, body, 0)
    if rows % step:
        dst_ref[dst_off + n * step:dst_off + rows, :] = src_ref[n * step:rows, :].astype(dst_ref.dtype)


def _split3(x):
    hi = x.astype(BF16)
    r1 = x - hi.astype(F32)
    mid = r1.astype(BF16)
    lo = (r1 - mid.astype(F32)).astype(BF16)
    return hi, mid, lo


def _ada_kernel(c_ref, w_ref, b_ref, o_ref):
    c = c_ref[...]
    a = (c * _sigmoid(c)).astype(BF16)
    o_ref[...] = _dot(a, w_ref[...].astype(BF16)) + b_ref[...]


def _ada(c_all, w_ada, b_ada):
    depth, _, n = w_ada.shape
    rows = c_all.shape[0]
    tn = 1024
    return pl.pallas_call(
        _ada_kernel,
        grid=(depth, n // tn),
        in_specs=[pl.BlockSpec((rows, D_MODEL), lambda l, j: (0, 0)),
                  pl.BlockSpec((None, D_MODEL, tn), lambda l, j: (l, 0, j)),
                  pl.BlockSpec((None, 1, tn), lambda l, j: (l, 0, j))],
        out_specs=pl.BlockSpec((None, rows, tn), lambda l, j: (l, 0, j)),
        out_shape=jax.ShapeDtypeStruct((depth, rows, n), F32),
        name="ada_mod",
        compiler_params=_cparams(("arbitrary", "arbitrary"), 40),
    )(c_all, w_ada, b_ada.reshape(depth, 1, n))


def _relbias_kernel(tab_ref, o_ref, *, cq, band):
    l = pl.program_id(0)
    h = pl.program_id(1)
    qi = lax.broadcasted_iota(jnp.int32, (cq, band), 0)
    s = lax.broadcasted_iota(jnp.int32, (cq, band), 1)
    idx = jnp.clip(qi + (band - cq) - s, -REL_CLIP, REL_CLIP) + REL_CLIP
    base = l * (N_REL * H_A) + h

    def body(r, acc):
        return jnp.where(idx == r, tab_ref[base + r * H_A], acc)

    o_ref[...] = lax.fori_loop(0, N_REL, body, jnp.zeros((cq, band), F32)) * LOG2E


def _relbias(rel_bias, cq, band):
    depth = rel_bias.shape[0]
    return pl.pallas_call(
        functools.partial(_relbias_kernel, cq=cq, band=band),
        grid=(depth, H_A),
        in_specs=[pl.BlockSpec(memory_space=pltpu.SMEM)],
        out_specs=pl.BlockSpec((None, None, cq, band), lambda l, h: (l, h, 0, 0)),
        out_shape=jax.ShapeDtypeStruct((depth, H_A, cq, band), F32),
        name="rel_bias",
    )(rel_bias.reshape(-1))


def _inproj_kernel(*refs, n_prev, head_major):
    x_ref, sc_ref, sh_ref, g_ref, wq_ref, wk_ref, wv_ref, wf_ref, bf_ref = refs[:9]
    outs = refs[9 + n_prev:-1]
    h_scr = refs[-1]
    q_ref, ka_ref, va_ref, kb_ref, vb_ref, kc_ref, vc_ref, lf_ref = outs[:8]
    j = pl.program_id(1)

    @pl.when(j == 0)
    def _():
        _norm_rows(x_ref, g_ref, sc_ref, sh_ref, h_scr)
        lf_ref[...] = _log_sigmoid(_dot(h_scr[...], wf_ref[...]) + bf_ref[...])

    h = h_scr[...]
    q_scale = jnp.where(j >= PAIRS_AB, -Q_SCALE, Q_SCALE)
    q_ref[...] = (_dot(h, wq_ref[...]) * q_scale).astype(BF16)
    kt = _dot(h, wk_ref[...])
    vt = _dot(h, wv_ref[...])

    @pl.when(j < PAIRS_A)
    def _():
        ka_ref[...] = kt
        va_ref[...] = vt

    def put(ref, tile):
        if head_major:
            for c in range(2):
                ref[c] = tile[:, c * HEAD_DIM:(c + 1) * HEAD_DIM]
        else:
            ref[...] = tile

    @pl.when((j >= PAIRS_A) & (j < PAIRS_AB))
    def _():
        put(kb_ref, kt)
        put(vb_ref, vt)

    @pl.when(j >= PAIRS_AB)
    def _():
        put(kc_ref, kt)
        put(vc_ref, vt)


def _mod_spec(mod, tm, tn, tiles_per_batch):
    r = mod.shape[1]
    col = (lambda j: 0) if tn == mod.shape[2] else (lambda j: j)
    if r == 1:
        return pl.BlockSpec((None, 1, tn), lambda i, j: (i // tiles_per_batch, 0, col(j)))
    return pl.BlockSpec((None, tm, tn), lambda i, j: (0, i, col(j)))


def _inproj(x2, sc, sh, g, w_qkv, w_f, b_f, tm, tiles_per_batch, state=None):
    m = x2.shape[0]
    tn = HEAD_PAIR

    def clamp(lo, n):
        return lambda i, j: (i, jnp.clip(j - lo, 0, n - 1))

    def full_mod(a):
        return _mod_spec(a, tm, D_MODEL, tiles_per_batch)

    groups = ((0, PAIRS_A), (PAIRS_A, PAIRS_AB - PAIRS_A), (PAIRS_AB, N_PAIRS - PAIRS_AB))
    out_shape = [jax.ShapeDtypeStruct((m, D_MODEL), BF16)]
    out_specs = [pl.BlockSpec((tm, tn), lambda i, j: (i, j))]
    for lo, n in groups:
        for _ in range(2):
            if state is None or lo == 0:
                out_shape.append(jax.ShapeDtypeStruct((m, n * tn), F32))
                out_specs.append(pl.BlockSpec((tm, tn), clamp(lo, n)))
            else:
                layer, depth, batch, _ = state
                out_shape.append(jax.ShapeDtypeStruct((depth, batch, 2 * n, m // batch, HEAD_DIM), F32))
                out_specs.append(pl.BlockSpec(
                    (None, None, 2, tm, HEAD_DIM),
                    lambda i, j, lo=lo, n=n: (layer, i // tiles_per_batch, jnp.clip(j - lo, 0, n - 1),
                                              i % tiles_per_batch, 0)))
    out_shape.append(jax.ShapeDtypeStruct((m, LANES), F32))
    out_specs.append(pl.BlockSpec((tm, LANES), lambda i, j: (i, 0)))
    in_specs = [pl.BlockSpec((tm, D_MODEL), lambda i, j: (i, 0)),
                full_mod(sc), full_mod(sh),
                pl.BlockSpec((1, D_MODEL), lambda i, j: (0, 0)),
                pl.BlockSpec((D_MODEL, tn), lambda i, j: (0, j)),
                pl.BlockSpec((D_MODEL, tn), lambda i, j: (0, N_PAIRS + j)),
                pl.BlockSpec((D_MODEL, tn), lambda i, j: (0, 2 * N_PAIRS + j)),
                pl.BlockSpec((D_MODEL, LANES), lambda i, j: (0, 0)),
                pl.BlockSpec((1, LANES), lambda i, j: (0, 0))]
    args = [x2, sc, sh, g, w_qkv, w_qkv, w_qkv, w_f, b_f]
    aliases = {}
    prev = None if state is None else state[3]
    if prev is not None:
        first_state_out = 3
        for idx, arr in enumerate(prev):
            in_specs.append(pl.BlockSpec(memory_space=pl.ANY))
            aliases[len(args)] = first_state_out + idx
            args.append(arr)
    q, ka, va, kb, vb, kc, vc, lf = pl.pallas_call(
        functools.partial(_inproj_kernel, n_prev=0 if prev is None else len(prev), head_major=state is not None),
        grid=(m // tm, N_PAIRS),
        in_specs=in_specs,
        out_specs=out_specs,
        out_shape=out_shape,
        input_output_aliases=aliases,
        scratch_shapes=[pltpu.VMEM((tm, D_MODEL), BF16)],
        name="in_proj",
        compiler_params=_cparams(("arbitrary", "arbitrary"), 52),
    )(*args)
    return q, ka, kb, kc, va, vb, vc, lf


def _dot_f32_by_01(x, ones01):
    hi, mid, lo = _split3(x)
    return _dot(hi, ones01) + _dot(mid, ones01) + _dot(lo, ones01)


def _cumsum_kernel(x_ref, o_ref):
    x = x_ref[...]
    rows = x.shape[0]
    i0 = lax.broadcasted_iota(jnp.int32, (LANES, LANES), 0)
    i1 = lax.broadcasted_iota(jnp.int32, (LANES, LANES), 1)
    upper = (i0 <= i1).astype(BF16)
    c = _dot_f32_by_01(x, upper)
    tot = jnp.broadcast_to(c[:, LANES - 1:LANES], (rows, LANES))
    r0 = lax.broadcasted_iota(jnp.int32, (rows, rows), 0)
    r1 = lax.broadcasted_iota(jnp.int32, (rows, rows), 1)
    lower = (r1 < r0).astype(BF16)
    hi, mid, lo = _split3(tot)
    off = _dot(lower, hi) + _dot(lower, mid) + _dot(lower, lo)
    o_ref[...] = (c + off) * LOG2E


def _cumsum_time(logf_hm):
    b, h, t = logf_hm.shape
    rows = t // LANES
    out = pl.pallas_call(
        _cumsum_kernel,
        grid=(b * h,),
        in_specs=[pl.BlockSpec((None, rows, LANES), lambda i: (i, 0, 0))],
        out_specs=pl.BlockSpec((None, rows, LANES), lambda i: (i, 0, 0)),
        out_shape=jax.ShapeDtypeStruct((b * h, rows, LANES), F32),
        name="forget_cumsum",
    )(logf_hm.reshape(b * h, rows, LANES))
    return out.reshape(b, h, t)


def _band_kernel(q_ref, k_ref, v_ref, bias_ref, o_ref, kp_scr, vp_scr, *, cq, npad, nmask, nchunks):
    tk = k_ref.shape[0]
    kp_scr[0:npad, :] = jnp.zeros((npad, HEAD_DIM), BF16)
    vp_scr[0:npad, :] = jnp.zeros((npad, HEAD_DIM), BF16)
    _copy_cast(k_ref, kp_scr, tk, dst_off=npad)
    _copy_cast(v_ref, vp_scr, tk, dst_off=npad)
    bias = bias_ref[...]
    slot = lax.broadcasted_iota(jnp.int32, (cq, A_BAND), 1)
    per_step = min(nchunks, 8)

    def step(i, carry):
        rows = [pl.multiple_of((i * per_step + u) * cq, cq) for u in range(per_step)]
        scores = [_dot_nt(q_ref[pl.ds(r0, cq), :], kp_scr[pl.ds(r0, A_BAND), :]) for r0 in rows]
        probs, denoms = [], []
        for u in range(per_step):
            first_valid = jnp.maximum(nmask, npad - (i * per_step + u) * cq)
            s = jnp.where(slot >= first_valid, scores[u] + bias, -jnp.inf)
            p = jnp.exp2(s - jnp.max(s, axis=-1, keepdims=True))
            denoms.append(jnp.sum(p, axis=-1, keepdims=True))
            probs.append(p.astype(BF16))
        for u in range(per_step):
            o = _dot(probs[u], vp_scr[pl.ds(rows[u], A_BAND), :]) / denoms[u]
            o_ref[pl.ds(rows[u], cq), :] = o.astype(o_ref.dtype)
        return carry

    lax.fori_loop(0, nchunks // per_step, step, 0)


def _band_attn(q, k, v, bias, cq, npad, nmask):
    b, t, _ = q.shape
    tk = k.shape[1]
    kern = functools.partial(_band_kernel, cq=cq, npad=npad, nmask=nmask, nchunks=t // cq)
    return pl.pallas_call(
        kern,
        grid=(b, H_A),
        in_specs=[pl.BlockSpec((None, t, HEAD_DIM), lambda i, h: (i, 0, h)),
                  pl.BlockSpec((None, tk, HEAD_DIM), lambda i, h: (i, 0, h)),
                  pl.BlockSpec((None, tk, HEAD_DIM), lambda i, h: (i, 0, h)),
                  pl.BlockSpec((None, cq, A_BAND), lambda i, h: (h, 0, 0))],
        out_specs=pl.BlockSpec((None, t, HEAD_DIM), lambda i, h: (i, 0, h)),
        out_shape=jax.ShapeDtypeStruct((b, t, H_A * HEAD_DIM), BF16),
        scratch_shapes=[pltpu.VMEM((npad + tk, HEAD_DIM), BF16),
                        pltpu.VMEM((npad + tk, HEAD_DIM), BF16)],
        name="band_attn",
        compiler_params=_cparams(("arbitrary", "arbitrary"), 48),
    )(q, k, v, bias)


def _pad_lanes_left(x, lanes, fill):
    if lanes == 0:
        return x
    return jnp.concatenate([jnp.full((x.shape[0], lanes), fill, x.dtype), x], axis=1)


def _col_replicate(row):
    n = row.shape[1]
    parts = [jnp.broadcast_to(row[:, c * LANES:(c + 1) * LANES], (LANES, LANES)).T for c in range(n // LANES)]
    return parts[0] if len(parts) == 1 else jnp.concatenate(parts, axis=0)


def _aug_columns(col_rep, first):
    hi, mid, lo = _split3(col_rep)
    lane = lax.broadcasted_iota(jnp.int32, col_rep.shape, 1)
    base = 0 if first else 3
    one = jnp.where((lane >= 3 - base) & (lane < 6 - base), 1.0, 0.0)
    x = jnp.where(lane == base, hi.astype(F32),
                  jnp.where(lane == base + 1, mid.astype(F32),
                            jnp.where(lane == base + 2, lo.astype(F32), one)))
    return x.astype(BF16)


def _fox_kernel(q_ref, k_ref, v_ref, g_ref, o_ref, kp_scr, vt_scr, qp_scr, acc_scr, x_scr, row_scr,
                *, tq, tk, nq, qoff, ng):
    nblk = kp_scr.shape[0]
    ones_rows = (lax.broadcasted_iota(jnp.int32, (VT_EXTRA, tk), 0) == 0).astype(BF16)

    def prep(kb, c):
        r0 = pl.multiple_of(kb * tk, tk)
        kp_scr[kb, :, 0:HEAD_DIM] = k_ref[pl.ds(r0, tk), :].astype(BF16)
        kp_scr[kb, :, HEAD_DIM:2 * HEAD_DIM] = _aug_columns(_col_replicate(-g_ref[kb]), True)
        vt_scr[kb, 0:HEAD_DIM, :] = v_ref[pl.ds(r0, tk), :].astype(F32).T.astype(BF16)
        vt_scr[kb, HEAD_DIM:HEAD_DIM + VT_EXTRA, :] = ones_rows
        return c

    lax.fori_loop(0, nblk, prep, 0)
    off = qoff % tk
    row = lax.broadcasted_iota(jnp.int32, (tk, tq), 0)
    col = lax.broadcasted_iota(jnp.int32, (tk, tq), 1)

    def first_lane(u, diag):
        return u * tk if diag and tq == ng * tk else 0

    def scores(g, slot, p0):
        m_blk = None
        for u in range(ng):
            l0 = first_lane(u, p0 is not None)
            x = _dot_nt(kp_scr[g * ng + u], qp_scr[l0:tq, :])
            if p0 is not None:
                row_u = lax.broadcasted_iota(jnp.int32, (tk, tq - l0), 0)
                col_u = lax.broadcasted_iota(jnp.int32, (tk, tq - l0), 1) + l0
                x = jnp.where(row_u + ((g * ng + u) * tk - p0) <= col_u, x, -jnp.inf)
            x_scr[slot, u, :, l0:tq] = x
            m_u = jnp.max(x, axis=0, keepdims=True)
            if p0 is None:
                m_blk = m_u if m_blk is None else jnp.maximum(m_blk, m_u)
            elif u == 0:
                row_scr[...] = jnp.broadcast_to(m_u, (SUBLANES, tq))
            else:
                row_scr[:, l0:tq] = jnp.maximum(row_scr[:, l0:tq], jnp.broadcast_to(m_u, (SUBLANES, tq - l0)))
        return m_blk if p0 is None else row_scr[0:1, :]

    def process(g, slot, m_cur, m_next, diag=False):
        acc = acc_scr[...]
        if diag:
            row_scr[...] = jnp.broadcast_to(m_cur, (SUBLANES, tq))
        for u in range(ng):
            l0 = first_lane(u, diag)
            m_u = m_cur if l0 == 0 else row_scr[0:1, l0:tq]
            p = jnp.exp2(x_scr[slot, u, :, l0:tq] - m_u)
            acc = acc + _pad_lanes_left(_dot(vt_scr[g * ng + u], p.astype(BF16)), l0, 0.0)
        acc_scr[...] = acc * jnp.exp2(m_cur - m_next)

    def qblock(i, carry):
        q0 = pl.multiple_of(i * tq, tq)
        p0 = qoff + i * tq
        pb = p0 // tk
        qp_scr[:, 0:HEAD_DIM] = q_ref[pl.ds(q0, tq), :]
        if tq <= tk:
            gq = g_ref[pb][:, off:off + tq]
        else:
            gq = jnp.concatenate([g_ref[pb + c] for c in range(tq // tk)], axis=1)
        qp_scr[:, HEAD_DIM:2 * HEAD_DIM] = _aug_columns(_col_replicate(gq), False)
        acc_scr[...] = jnp.zeros((HEAD_DIM + VT_EXTRA, tq), F32)
        gd = pb // ng
        m0 = scores(gd, 0, p0)
        m_pre = scores(jnp.maximum(gd - 1, 0), 1, None)
        m1 = jnp.where(gd >= 1, jnp.maximum(m0, m_pre), m0)
        process(gd, 0, m0, m1, diag=True)

        def pair(pi, m_run):
            g1 = gd - 1 - 2 * pi
            m_a = jnp.maximum(m_run, scores(g1 - 1, 0, None))
            process(g1, 1, m_run, m_a)
            m_pre = scores(jnp.maximum(g1 - 2, 0), 1, None)
            m_b = jnp.where(g1 >= 2, jnp.maximum(m_a, m_pre), m_a)
            process(g1 - 1, 0, m_a, m_b)
            return m_b

        m_last = lax.fori_loop(0, gd // 2, pair, m1)

        @pl.when(gd % 2 == 1)
        def _():
            process(0, 1, m_last, m_last)

        denom = acc_scr[HEAD_DIM:HEAD_DIM + 1, :]
        o_ref[pl.ds(q0, tq), :] = (acc_scr[0:HEAD_DIM, :] / denom).T.astype(o_ref.dtype)
        return carry

    lax.fori_loop(0, nq, qblock, 0)


def _stick_kernel(q_ref, k_ref, v_ref, o_ref, kp_scr, vt_scr, acc_scr, x_scr, row_scr, *, tq, tk, nq, qoff, ng):
    nblk = kp_scr.shape[0]
    nv = tk // SUBLANES

    r = lax.broadcasted_iota(jnp.int32, (tk, tk), 0)
    kk = lax.broadcasted_iota(jnp.int32, (tk, tk), 1)
    perm = ((r % SUBLANES) * nv + r // SUBLANES == kk).astype(BF16)
    perm_t = ((kk % SUBLANES) * nv + kk // SUBLANES == r).astype(BF16)

    def prep(kb, c):
        r0 = pl.multiple_of(kb * tk, tk)
        kp_scr[kb] = _dot(perm, k_ref[pl.ds(r0, tk), :].astype(BF16)).astype(BF16)
        vt_scr[kb] = _dot(v_ref[pl.ds(r0, tk), :].astype(F32).T.astype(BF16), perm_t).astype(BF16)
        return c

    lax.fori_loop(0, nblk, prep, 0)
    row = lax.broadcasted_iota(jnp.int32, (tk, tq), 0)
    col = lax.broadcasted_iota(jnp.int32, (tk, tq), 1)
    key_in_block = (row % SUBLANES) * nv + row // SUBLANES
    sub = lax.broadcasted_iota(jnp.int32, (SUBLANES, tq), 0)

    def scores(q, g, slot):
        for u in range(ng):
            x_scr[slot, u] = _dot_nt(kp_scr[g * ng + u], q)

    def tile_terms(x, kb, p0, l0):
        beta = 1.0 / (1.0 + jnp.exp2(x))
        stay = 1.0 - beta
        valid = None
        if p0 is not None:
            row_u = lax.broadcasted_iota(jnp.int32, (tk, tq - l0), 0)
            col_u = lax.broadcasted_iota(jnp.int32, (tk, tq - l0), 1) + l0
            valid = (row_u % SUBLANES) * nv + row_u // SUBLANES + (kb * tk - p0) < col_u
            stay = jnp.where(valid, stay, 1.0)
        after = [None] * nv
        run = jnp.ones((SUBLANES, tq - l0), F32)
        for v in range(nv - 1, -1, -1):
            after[v] = run
            run = run * stay[v * SUBLANES:(v + 1) * SUBLANES, :]
        incl = run
        sub_l = lax.broadcasted_iota(jnp.int32, (SUBLANES, tq - l0), 0)
        for k in (1, 2, 4):
            incl = incl * jnp.where(sub_l + k < SUBLANES, pltpu.roll(incl, SUBLANES - k, axis=0), 1.0)
        later = jnp.where(sub_l + 1 < SUBLANES, pltpu.roll(incl, SUBLANES - 1, axis=0), 1.0)
        return after, later, incl[0:1, :], beta, valid

    def process(g, slot, carry, p0):
        acc = acc_scr[...]
        skip = p0 is not None and tq == ng * tk
        if skip:
            row_scr[...] = jnp.broadcast_to(carry, (SUBLANES, tq))
        for u in range(ng - 1, -1, -1):
            l0 = u * tk if skip else 0
            after, later, total, beta, valid = tile_terms(x_scr[slot, u, :, l0:tq], g * ng + u, p0, l0)
            scale_u = later * (row_scr[:, l0:tq] if skip else carry)
            w = jnp.concatenate([after[v] * scale_u for v in range(nv)], axis=0) * beta
            if valid is not None:
                w = jnp.where(valid, w, 0.0)
            acc = acc + _pad_lanes_left(_dot(vt_scr[g * ng + u], w.astype(BF16)), l0, 0.0)
            if skip:
                row_scr[:, l0:tq] = row_scr[:, l0:tq] * total
            else:
                carry = carry * total
        acc_scr[...] = acc
        return row_scr[0:1, :] if skip else carry

    def qblock(i, c0):
        q0 = pl.multiple_of(i * tq, tq)
        p0 = qoff + i * tq
        gd = (p0 // tk) // ng
        q = q_ref[pl.ds(q0, tq), :]
        acc_scr[...] = jnp.zeros((HEAD_DIM, tq), F32)
        scores(q, gd, 0)
        scores(q, jnp.maximum(gd - 1, 0), 1)
        carry = process(gd, 0, jnp.ones((1, tq), F32), p0)

        def pair(pi, carry):
            g1 = gd - 1 - 2 * pi
            scores(q, g1 - 1, 0)
            carry = process(g1, 1, carry, None)
            scores(q, jnp.maximum(g1 - 2, 0), 1)
            return process(g1 - 1, 0, carry, None)

        carry = lax.fori_loop(0, gd // 2, pair, carry)

        @pl.when(gd % 2 == 1)
        def _():
            process(0, 1, carry, None)

        o_ref[pl.ds(q0, tq), :] = acc_scr[...].T.astype(o_ref.dtype)
        return c0

    lax.fori_loop(0, nq, qblock, 0)


def _causal_attn(kind, q, k, v, f, head0, tq, tk, qoff, ng, layer=None):
    b, t, _ = q.shape
    if layer is None:
        tkv = k.shape[1]
        nh = k.shape[2] // HEAD_DIM
        kv_spec = pl.BlockSpec((None, tkv, HEAD_DIM), lambda i, h: (i, 0, h))
    else:
        nh, tkv = k.shape[2], k.shape[3]
        kv_spec = pl.BlockSpec((None, None, None, tkv, HEAD_DIM), lambda i, h: (layer, i, h, 0, 0))
    nblk = tkv // tk
    assert tkv % tk == 0 and nblk % ng == 0 and tq % LANES == 0 and t % tq == 0
    assert (ng * tk) % tq == 0 and qoff % tq == 0 and (tq % tk == 0 or (qoff % tk) + tq <= tk and t == tq)
    common = dict(tq=tq, tk=tk, nq=t // tq, qoff=qoff, ng=ng)
    in_specs = [pl.BlockSpec((None, t, HEAD_DIM), lambda i, h: (i, 0, head0 + h)), kv_spec, kv_spec]
    if kind == "fox":
        kern = functools.partial(_fox_kernel, **common)
        g = f.reshape(b, nh, nblk, 1, tk)
        in_specs.append(pl.BlockSpec((None, None, nblk, 1, tk), lambda i, h: (i, h, 0, 0, 0)))
        scratch = [pltpu.VMEM((nblk, tk, 2 * HEAD_DIM), BF16), pltpu.VMEM((nblk, HEAD_DIM + VT_EXTRA, tk), BF16),
                   pltpu.VMEM((tq, 2 * HEAD_DIM), BF16), pltpu.VMEM((HEAD_DIM + VT_EXTRA, tq), F32),
                   pltpu.VMEM((2, ng, tk, tq), F32), pltpu.VMEM((SUBLANES, tq), F32)]
        args = (q, k, v, g)
    else:
        kern = functools.partial(_stick_kernel, **common)
        scratch = [pltpu.VMEM((nblk, tk, HEAD_DIM), BF16), pltpu.VMEM((nblk, HEAD_DIM, tk), BF16),
                   pltpu.VMEM((HEAD_DIM, tq), F32), pltpu.VMEM((2, ng, tk, tq), F32),
                   pltpu.VMEM((SUBLANES, tq), F32)]
        args = (q, k, v)
    return pl.pallas_call(
        kern,
        grid=(b, nh),
        in_specs=in_specs,
        out_specs=pl.BlockSpec((None, t, HEAD_DIM), lambda i, h: (i, 0, h)),
        out_shape=jax.ShapeDtypeStruct((b, t, nh * HEAD_DIM), BF16),
        scratch_shapes=scratch,
        name=kind + "_attn",
        compiler_params=_cparams(("arbitrary", "arbitrary"), 52),
    )(*args)


def _outproj_kernel(oa_ref, ob_ref, oc_ref, wa_ref, wb_ref, wc_ref, x_ref, gt_ref, y_ref):
    acc = _dot(oa_ref[...], wa_ref[...]) + _dot(ob_ref[...], wb_ref[...]) + _dot(oc_ref[...], wc_ref[...])
    y_ref[...] = x_ref[...] + gt_ref[...] * acc


def _outproj(oa, ob, oc, wa, wb, wc, x2, gt, tm, tiles_per_batch):
    m = x2.shape[0]
    tn = 1024
    row = lambda a: pl.BlockSpec((tm, a.shape[1]), lambda i, j: (i, 0))
    wsp = lambda a: pl.BlockSpec((a.shape[0], tn), lambda i, j: (0, j))
    return pl.pallas_call(
        _outproj_kernel,
        grid=(m // tm, D_MODEL // tn),
        in_specs=[row(oa), row(ob), row(oc), wsp(wa), wsp(wb), wsp(wc),
                  pl.BlockSpec((tm, tn), lambda i, j: (i, j)),
                  _mod_spec(gt, tm, tn, tiles_per_batch)],
        out_specs=pl.BlockSpec((tm, tn), lambda i, j: (i, j)),
        out_shape=jax.ShapeDtypeStruct((m, D_MODEL), F32),
        name="out_proj",
        compiler_params=_cparams(("arbitrary", "arbitrary"), 40),
    )(oa, ob, oc, wa, wb, wc, x2, gt)


def _ffn1_kernel(x_ref, sc_ref, sh_ref, g_ref, wg_ref, wu_ref, a_ref, h_scr):
    @pl.when(pl.program_id(1) == 0)
    def _():
        _norm_rows(x_ref, g_ref, sc_ref, sh_ref, h_scr)

    h = h_scr[...]
    gate = _dot(h, wg_ref[...])
    up = _dot(h, wu_ref[...])
    a_ref[...] = (gate * _sigmoid(gate) * up).astype(BF16)


def _ffn1(x2, sc, sh, g, wg, wu, tm, tiles_per_batch):
    m = x2.shape[0]
    dff = wg.shape[1]
    tn = 512
    full_mod = lambda a: _mod_spec(a, tm, D_MODEL, tiles_per_batch)
    return pl.pallas_call(
        _ffn1_kernel,
        grid=(m // tm, dff // tn),
        in_specs=[pl.BlockSpec((tm, D_MODEL), lambda i, j: (i, 0)),
                  full_mod(sc), full_mod(sh),
                  pl.BlockSpec((1, D_MODEL), lambda i, j: (0, 0)),
                  pl.BlockSpec((D_MODEL, tn), lambda i, j: (0, j)),
                  pl.BlockSpec((D_MODEL, tn), lambda i, j: (0, j))],
        out_specs=pl.BlockSpec((tm, tn), lambda i, j: (i, j)),
        out_shape=jax.ShapeDtypeStruct((m, dff), BF16),
        scratch_shapes=[pltpu.VMEM((tm, D_MODEL), BF16)],
        name="ffn_gate_up",
        compiler_params=_cparams(("arbitrary", "arbitrary"), 44),
    )(x2, sc, sh, g, wg, wu)


def _ffn2_kernel(a_ref, wd_ref, x_ref, gt_ref, y_ref):
    y_ref[...] = x_ref[...] + gt_ref[...] * _dot(a_ref[...], wd_ref[...])


def _ffn2(a, wd, x2, gt, tm, tiles_per_batch):
    m, dff = a.shape
    tn = 512
    return pl.pallas_call(
        _ffn2_kernel,
        grid=(m // tm, D_MODEL // tn),
        in_specs=[pl.BlockSpec((tm, dff), lambda i, j: (i, 0)),
                  pl.BlockSpec((dff, tn), lambda i, j: (0, j)),
                  pl.BlockSpec((tm, tn), lambda i, j: (i, j)),
                  _mod_spec(gt, tm, tn, tiles_per_batch)],
        out_specs=pl.BlockSpec((tm, tn), lambda i, j: (i, j)),
        out_shape=jax.ShapeDtypeStruct((m, D_MODEL), F32),
        name="ffn_down",
        compiler_params=_cparams(("arbitrary", "arbitrary"), 48),
    )(a, wd, x2, gt)


def _final_norm_kernel(x_ref, g_ref, y_ref, *, rows_chunk=64):
    g = g_ref[...]

    def body(r, c):
        r0 = pl.multiple_of(r * rows_chunk, rows_chunk)
        x = x_ref[pl.ds(r0, rows_chunk), :]
        ms = jnp.mean(x * x, axis=-1, keepdims=True)
        y_ref[pl.ds(r0, rows_chunk), :] = x * lax.rsqrt(ms + EPS) * g
        return c

    lax.fori_loop(0, x_ref.shape[0] // rows_chunk, body, 0)


def _final_norm(x2, g, tm):
    m = x2.shape[0]
    return pl.pallas_call(
        _final_norm_kernel,
        grid=(m // tm,),
        in_specs=[pl.BlockSpec((tm, D_MODEL), lambda i: (i, 0)),
                  pl.BlockSpec((1, D_MODEL), lambda i: (0, 0))],
        out_specs=pl.BlockSpec((tm, D_MODEL), lambda i: (i, 0)),
        out_shape=jax.ShapeDtypeStruct((m, D_MODEL), F32),
        name="final_norm",
        compiler_params=_cparams(("arbitrary",), 40),
    )(x2, g.reshape(1, D_MODEL))


def _pad_time(a, total):
    pad = total - a.shape[1]
    if pad == 0:
        return a
    return jnp.pad(a, ((0, 0), (0, pad)) + ((0, 0),) * (a.ndim - 2))


def _mix_and_ffn(x2, oa, ob, oc, mods, wts, tm, tiles_per_batch):
    _, _, gt1, sh2, sc2, gt2 = mods
    m = x2.shape[0]
    r2 = lambda a: a.reshape(m, a.shape[-1])
    x2 = _outproj(r2(oa), r2(ob), r2(oc), wts["wo_a"], wts["wo_b"], wts["wo_c"], x2, gt1, tm, tiles_per_batch)
    a = _ffn1(x2, sc2, sh2, wts["g_ffn"], wts["w_gate"], wts["w_up"], tm, tiles_per_batch)
    return _ffn2(a, wts["w_down"], x2, gt2, tm, tiles_per_batch)


def _prompt_layer(x2, batch, mods, wts, bias, tm, layer, depth, prev_state):
    m = x2.shape[0]
    t = m // batch
    tiles_per_batch = t // tm
    sh1, sc1 = mods[0], mods[1]
    q, ka, kb, kc, va, vb, vc, lf = _inproj(
        x2, sc1, sh1, wts["g_attn"], wts["w_qkv"], wts["w_f"], wts["b_f"], tm, tiles_per_batch,
        state=(layer, depth, batch, prev_state))
    r3 = lambda a: a.reshape(batch, t, a.shape[-1])
    q, ka, va = map(r3, (q, ka, va))
    logf = r3(lf)[:, :, :H_B]
    keep = min(A_LEFT, t)
    small_state = (ka[:, t - keep:], va[:, t - keep:], logf)
    f = _cumsum_time(jnp.swapaxes(logf, 1, 2))
    oa = _band_attn(q, ka, va, bias, CHUNK, A_BAND - CHUNK, A_BAND - CHUNK - A_LEFT)
    tq, tk, ng = 1024, 256, 4
    ob = _causal_attn("fox", q, kb, vb, f, H_A, tq, tk, 0, ng, layer=layer)
    oc = _causal_attn("stick", q, kc, vc, None, H_A + H_B, tq, tk, 0, ng, layer=layer)
    return _mix_and_ffn(x2, oa, ob, oc, mods, wts, tm, tiles_per_batch), small_state, (kb, vb, kc, vc)


def _sample_layer(x2, batch, mods, wts, bias, caches, tm):
    m = x2.shape[0]
    t = m // batch
    sh1, sc1 = mods[0], mods[1]
    q, ka, kb, kc, va, vb, vc, lf = _inproj(x2, sc1, sh1, wts["g_attn"], wts["w_qkv"], wts["w_f"], wts["b_f"], tm, 1)
    r3 = lambda a: a.reshape(batch, t, a.shape[-1])
    q, ka, kb, kc, va, vb, vc = map(r3, (q, ka, kb, kc, va, vb, vc))
    logf = r3(lf)[:, :, :H_B]
    cak, cav, cbk, cbv, cblogf, cck, ccv = caches
    state = (ka, va, kb, vb, logf, kc, vc)
    flat = lambda c: c.reshape(c.shape[0], c.shape[1], -1)
    past = cbk.shape[1]
    tk = 256
    tkv = past + tk

    def cat(cache, new):
        heads = cache.shape[2]
        new_hm = jnp.swapaxes(new.reshape(batch, t, heads, HEAD_DIM), 1, 2)
        both = jnp.concatenate([jnp.swapaxes(cache, 1, 2), new_hm], axis=2)
        return jnp.pad(both, ((0, 0), (0, 0), (0, tkv - both.shape[2]), (0, 0)))[None]

    kka = jnp.concatenate([flat(cak), ka], axis=1)
    vva = jnp.concatenate([flat(cav), va], axis=1)
    wa = kka.shape[1]
    oa = _band_attn(q, kka, vva, bias, t, A_BAND - wa, A_BAND - wa)
    lcat = jnp.concatenate([cblogf.astype(F32), logf], axis=1)
    fpad = -(-tkv // (SUBLANES * LANES)) * (SUBLANES * LANES)
    f = _cumsum_time(jnp.swapaxes(_pad_time(lcat, fpad), 1, 2))[:, :, :tkv]
    qpad = _pad_time(q, LANES)
    ng = tkv // tk
    ob = _causal_attn("fox", qpad, cat(cbk, kb), cat(cbv, vb), f, H_A, LANES, tk, past, ng, layer=0)[:, :t]
    oc = _causal_attn("stick", qpad, cat(cck, kc), cat(ccv, vc), None, H_A + H_B, LANES, tk, past, ng,
                      layer=0)[:, :t]
    return _mix_and_ffn(x2, oa, ob, oc, mods, wts, tm, 1), state


def kernel(x_prompt, x_sample, cache_a_k, cache_a_v, cache_b_k, cache_b_v, cache_b_logf, cache_c_k, cache_c_v,
           c_prompt, c_sample, w_ada, b_ada, g_attn, g_ffn, w_in, b_f, rel_bias, w_o, w_gate, w_up, w_down, g_final):
    depth = w_ada.shape[0]
    bp, tp, _ = x_prompt.shape
    bs, ts, _ = x_sample.shape
    n_qkv = 3 * D_MODEL

    rows = -(-(bp + bs) // SUBLANES) * SUBLANES
    c_all = jnp.concatenate([c_prompt, c_sample, jnp.zeros((rows - bp - bs, D_MODEL), F32)], axis=0)
    mod = _ada(c_all, w_ada, b_ada)

    bias_p = _relbias(rel_bias, CHUNK, A_BAND)
    bias_s = _relbias(rel_bias, ts, A_BAND)

    xp = x_prompt.reshape(bp * tp, D_MODEL)
    xs = x_sample.reshape(bs * ts, D_MODEL)
    tm_p, tm_s = 1024, bs * ts
    small_p, states_s = [], []
    big_p = None
    ea = H_A * HEAD_DIM
    eb = (H_A + H_B) * HEAD_DIM
    for l in range(depth):
        wo = w_o[l].astype(BF16)
        wts = dict(
            g_attn=g_attn[l].reshape(1, D_MODEL), g_ffn=g_ffn[l].reshape(1, D_MODEL),
            w_qkv=w_in[l, :, :n_qkv].astype(BF16),
            w_f=jnp.pad(w_in[l, :, n_qkv:], ((0, 0), (0, LANES - H_B))).astype(BF16),
            b_f=jnp.pad(b_f[l], (0, LANES - H_B)).reshape(1, LANES),
            wo_a=wo[:ea], wo_b=wo[ea:eb], wo_c=wo[eb:],
            w_gate=w_gate[l].astype(BF16), w_up=w_up[l].astype(BF16), w_down=w_down[l].astype(BF16))
        chunks = jnp.split(mod[l], 6, axis=-1)
        mods_p = [c[:bp].reshape(bp, 1, D_MODEL) for c in chunks]
        mods_s = [jnp.repeat(c[bp:bp + bs], ts, axis=0).reshape(1, bs * ts, D_MODEL) for c in chunks]
        xp, st_small, big_p = _prompt_layer(xp, bp, mods_p, wts, bias_p[l], tm_p, l, depth, big_p)
        caches = (cache_a_k[l], cache_a_v[l], cache_b_k[l], cache_b_v[l], cache_b_logf[l],
                  cache_c_k[l], cache_c_v[l])
        xs, st_s = _sample_layer(xs, bs, mods_s, wts, bias_s[l], caches, tm_s)
        small_p.append(st_small)
        states_s.append(st_s)

    def stack_heads(arrs, heads):
        a = jnp.stack(arrs)
        return a if heads is None else a.reshape(a.shape[:3] + (heads, HEAD_DIM))

    a_k_p, a_v_p, b_logf_p = [stack_heads(arrs, h) for arrs, h in zip(zip(*small_p), (H_A, H_A, None))]
    b_k_p, b_v_p, c_k_p, c_v_p = [jnp.swapaxes(a, 2, 3) for a in big_p]
    sample_states = [stack_heads(arrs, h)
                     for arrs, h in zip(zip(*states_s), (H_A, H_A, H_B, H_B, None, H_C, H_C))]
    y_prompt = _final_norm(xp, g_final, 512).reshape(bp, tp, D_MODEL)
    y_sample = _final_norm(xs, g_final, bs * ts).reshape(bs, ts, D_MODEL)
    return tuple([y_prompt, y_sample, a_k_p, a_v_p, b_k_p, b_v_p, b_logf_p, c_k_p, c_v_p] + sample_states)
```

```python
import functools

import jax
import jax.numpy as jnp
from jax import lax
from jax.experimental import pallas as pl
from jax.experimental.pallas import tpu as pltpu

F32 = jnp.float32
BF16 = jnp.bfloat16

D_MODEL = 2048
HEAD_DIM = 128
N_HEADS = D_MODEL // HEAD_DIM
H_A = N_HEADS // 4
H_B = (N_HEADS - H_A) // 2
H_C = N_HEADS - H_A - H_B
CHUNK = 64
A_LEFT = 8 * CHUNK
REL_CLIP = 128
N_REL = 2 * REL_CLIP + 1
EPS = 1e-6
ATTN_SCALE = HEAD_DIM ** -0.5
LOG2E = 1.4426950408889634
Q_SCALE = ATTN_SCALE * LOG2E

V7X_VMEM_BYTES = 64 * 1024 * 1024
LANES = 128
SUBLANES = 8
HEAD_PAIR = 2 * HEAD_DIM
A_BAND = 640
VT_EXTRA = 16

PAIRS_A = H_A // 2
PAIRS_AB = (H_A + H_B) // 2
N_PAIRS = N_HEADS // 2


def _cparams(sem, vmem_mb):
    return pltpu.CompilerParams(dimension_semantics=sem,
                                vmem_limit_bytes=min(vmem_mb * 1024 * 1024, V7X_VMEM_BYTES - (4 << 20)))


def _dot(a, b):
    return jnp.dot(a, b, preferred_element_type=F32)


def _dot_nt(a, b):
    return lax.dot_general(a, b, (((1,), (1,)), ((), ())), preferred_element_type=F32)


def _sigmoid(x):
    return 1.0 / (1.0 + jnp.exp(-x))


def _log_sigmoid(x):
    return jnp.minimum(x, 0.0) - jnp.log1p(jnp.exp(-jnp.abs(x)))


def _modulated_norm(x, g, sc, sh):
    ms = jnp.mean(x * x, axis=-1, keepdims=True)
    y = x * lax.rsqrt(ms + EPS) * g
    return y * (1.0 + sc) + sh


def _norm_rows(x_ref, g_ref, sc_ref, sh_ref, h_ref, rows_chunk=128):
    tm = x_ref.shape[0]
    rows_chunk = min(rows_chunk, tm)
    assert tm % rows_chunk == 0
    g = g_ref[...]
    per_row = sc_ref.shape[0] != 1

    def body(r, c):
        r0 = pl.multiple_of(r * rows_chunk, rows_chunk)
        sc = sc_ref[pl.ds(r0, rows_chunk), :] if per_row else sc_ref[...]
        sh = sh_ref[pl.ds(r0, rows_chunk), :] if per_row else sh_ref[...]
        h = _modulated_norm(x_ref[pl.ds(r0, rows_chunk), :], g, sc, sh)
        h_ref[pl.ds(r0, rows_chunk), :] = h.astype(h_ref.dtype)
        return c

    lax.fori_loop(0, tm // rows_chunk, body, 0)


def _copy_cast(src_ref, dst_ref, rows, dst_off=0, step=256):
    n = rows // step

    def body(i, c):
        r = pl.multiple_of(i * step, step)
        d = pl.multiple_of(dst_off + i * step, 16)
        dst_ref[pl.ds(d, step), :] = src_ref[pl.ds(r, step), :].astype(dst_ref.dtype)
        return c

    lax.fori_loop(0, n, body, 0)
    if rows % step:
        dst_ref[dst_off + n * step:dst_off + rows, :] = src_ref[n * step:rows, :].astype(dst_ref.dtype)


def _split3(x):
    hi = x.astype(BF16)
    r1 = x - hi.astype(F32)
    mid = r1.astype(BF16)
    lo = (r1 - mid.astype(F32)).astype(BF16)
    return hi, mid, lo


def _ada_kernel(c_ref, w_ref, b_ref, o_ref):
    c = c_ref[...]
    a = (c * _sigmoid(c)).astype(BF16)
    o_ref[...] = _dot(a, w_ref[...].astype(BF16)) + b_ref[...]


def _ada(c_all, w_ada, b_ada):
    depth, _, n = w_ada.shape
    rows = c_all.shape[0]
    tn = 1024
    return pl.pallas_call(
        _ada_kernel,
        grid=(depth, n // tn),
        in_specs=[pl.BlockSpec((rows, D_MODEL), lambda l, j: (0, 0)),
                  pl.BlockSpec((None, D_MODEL, tn), lambda l, j: (l, 0, j)),
                  pl.BlockSpec((None, 1, tn), lambda l, j: (l, 0, j))],
        out_specs=pl.BlockSpec((None, rows, tn), lambda l, j: (l, 0, j)),
        out_shape=jax.ShapeDtypeStruct((depth, rows, n), F32),
        name="ada_mod",
        compiler_params=_cparams(("arbitrary", "arbitrary"), 40),
    )(c_all, w_ada, b_ada.reshape(depth, 1, n))


def _relbias_kernel(tab_ref, o_ref, *, cq, band):
    l = pl.program_id(0)
    h = pl.program_id(1)
    qi = lax.broadcasted_iota(jnp.int32, (cq, band), 0)
    s = lax.broadcasted_iota(jnp.int32, (cq, band), 1)
    idx = jnp.clip(qi + (band - cq) - s, -REL_CLIP, REL_CLIP) + REL_CLIP
    base = l * (N_REL * H_A) + h

    def body(r, acc):
        return jnp.where(idx == r, tab_ref[base + r * H_A], acc)

    o_ref[...] = lax.fori_loop(0, N_REL, body, jnp.zeros((cq, band), F32)) * LOG2E


def _relbias(rel_bias, cq, band):
    depth = rel_bias.shape[0]
    return pl.pallas_call(
        functools.partial(_relbias_kernel, cq=cq, band=band),
        grid=(depth, H_A),
        in_specs=[pl.BlockSpec(memory_space=pltpu.SMEM)],
        out_specs=pl.BlockSpec((None, None, cq, band), lambda l, h: (l, h, 0, 0)),
        out_shape=jax.ShapeDtypeStruct((depth, H_A, cq, band), F32),
        name="rel_bias",
    )(rel_bias.reshape(-1))


def _inproj_kernel(*refs, n_prev, head_major):
    x_ref, sc_ref, sh_ref, g_ref, wq_ref, wk_ref, wv_ref, wf_ref, bf_ref = refs[:9]
    outs = refs[9 + n_prev:-1]
    h_scr = refs[-1]
    q_ref, ka_ref, va_ref, kb_ref, vb_ref, kc_ref, vc_ref, lf_ref = outs[:8]
    j = pl.program_id(1)

    @pl.when(j == 0)
    def _():
        _norm_rows(x_ref, g_ref, sc_ref, sh_ref, h_scr)
        lf_ref[...] = _log_sigmoid(_dot(h_scr[...], wf_ref[...]) + bf_ref[...])

    h = h_scr[...]
    q_scale = jnp.where(j >= PAIRS_AB, -Q_SCALE, Q_SCALE)
    q_ref[...] = (_dot(h, wq_ref[...]) * q_scale).astype(BF16)
    kt = _dot(h, wk_ref[...])
    vt = _dot(h, wv_ref[...])

    @pl.when(j < PAIRS_A)
    def _():
        ka_ref[...] = kt
        va_ref[...] = vt

    def put(ref, tile):
        if head_major:
            for c in range(2):
                ref[c] = tile[:, c * HEAD_DIM:(c + 1) * HEAD_DIM]
        else:
            ref[...] = tile

    @pl.when((j >= PAIRS_A) & (j < PAIRS_AB))
    def _():
        put(kb_ref, kt)
        put(vb_ref, vt)

    @pl.when(j >= PAIRS_AB)
    def _():
        put(kc_ref, kt)
        put(vc_ref, vt)


def _weight_spec(w, layer, tn, col):
    return pl.BlockSpec((None, w.shape[1], tn), lambda i, j: (layer, 0, col(j)))


def _mod_spec(mod, tm, tn, tiles_per_batch):
    r = mod.shape[1]
    col = (lambda j: 0) if tn == mod.shape[2] else (lambda j: j)
    if r == 1:
        return pl.BlockSpec((None, 1, tn), lambda i, j: (i // tiles_per_batch, 0, col(j)))
    return pl.BlockSpec((None, tm, tn), lambda i, j: (0, i, col(j)))


def _inproj(x2, sc, sh, g, w_in, wl, w_f, b_f, tm, tiles_per_batch, state=None):
    m = x2.shape[0]
    tn = HEAD_PAIR

    def clamp(lo, n):
        return lambda i, j: (i, jnp.clip(j - lo, 0, n - 1))

    def full_mod(a):
        return _mod_spec(a, tm, D_MODEL, tiles_per_batch)

    groups = ((0, PAIRS_A), (PAIRS_A, PAIRS_AB - PAIRS_A), (PAIRS_AB, N_PAIRS - PAIRS_AB))
    out_shape = [jax.ShapeDtypeStruct((m, D_MODEL), BF16)]
    out_specs = [pl.BlockSpec((tm, tn), lambda i, j: (i, j))]
    for lo, n in groups:
        for _ in range(2):
            if state is None or lo == 0:
                out_shape.append(jax.ShapeDtypeStruct((m, n * tn), F32))
                out_specs.append(pl.BlockSpec((tm, tn), clamp(lo, n)))
            else:
                layer, depth, batch, _ = state
                out_shape.append(jax.ShapeDtypeStruct((depth, batch, 2 * n, m // batch, HEAD_DIM), F32))
                out_specs.append(pl.BlockSpec(
                    (None, None, 2, tm, HEAD_DIM),
                    lambda i, j, lo=lo, n=n: (layer, i // tiles_per_batch, jnp.clip(j - lo, 0, n - 1),
                                              i % tiles_per_batch, 0)))
    out_shape.append(jax.ShapeDtypeStruct((m, LANES), F32))
    out_specs.append(pl.BlockSpec((tm, LANES), lambda i, j: (i, 0)))
    in_specs = [pl.BlockSpec((tm, D_MODEL), lambda i, j: (i, 0)),
                full_mod(sc), full_mod(sh),
                pl.BlockSpec((1, D_MODEL), lambda i, j: (0, 0)),
                _weight_spec(w_in, wl, tn, lambda j: j),
                _weight_spec(w_in, wl, tn, lambda j: N_PAIRS + j),
                _weight_spec(w_in, wl, tn, lambda j: 2 * N_PAIRS + j),
                pl.BlockSpec((D_MODEL, LANES), lambda i, j: (0, 0)),
                pl.BlockSpec((1, LANES), lambda i, j: (0, 0))]
    args = [x2, sc, sh, g, w_in, w_in, w_in, w_f, b_f]
    aliases = {}
    prev = None if state is None else state[3]
    if prev is not None:
        first_state_out = 3
        for idx, arr in enumerate(prev):
            in_specs.append(pl.BlockSpec(memory_space=pl.ANY))
            aliases[len(args)] = first_state_out + idx
            args.append(arr)
    q, ka, va, kb, vb, kc, vc, lf = pl.pallas_call(
        functools.partial(_inproj_kernel, n_prev=0 if prev is None else len(prev), head_major=state is not None),
        grid=(m // tm, N_PAIRS),
        in_specs=in_specs,
        out_specs=out_specs,
        out_shape=out_shape,
        input_output_aliases=aliases,
        scratch_shapes=[pltpu.VMEM((tm, D_MODEL), BF16)],
        name="in_proj",
        compiler_params=_cparams(("arbitrary", "arbitrary"), 52),
    )(*args)
    return q, ka, kb, kc, va, vb, vc, lf


def _dot_f32_by_01(x, ones01):
    hi, mid, lo = _split3(x)
    return _dot(hi, ones01) + _dot(mid, ones01) + _dot(lo, ones01)


def _cumsum_kernel(x_ref, o_ref):
    x = x_ref[...]
    rows = x.shape[0]
    i0 = lax.broadcasted_iota(jnp.int32, (LANES, LANES), 0)
    i1 = lax.broadcasted_iota(jnp.int32, (LANES, LANES), 1)
    upper = (i0 <= i1).astype(BF16)
    c = _dot_f32_by_01(x, upper)
    tot = jnp.broadcast_to(c[:, LANES - 1:LANES], (rows, LANES))
    r0 = lax.broadcasted_iota(jnp.int32, (rows, rows), 0)
    r1 = lax.broadcasted_iota(jnp.int32, (rows, rows), 1)
    lower = (r1 < r0).astype(BF16)
    hi, mid, lo = _split3(tot)
    off = _dot(lower, hi) + _dot(lower, mid) + _dot(lower, lo)
    o_ref[...] = (c + off) * LOG2E


def _cumsum_time(logf_hm):
    b, h, t = logf_hm.shape
    rows = t // LANES
    out = pl.pallas_call(
        _cumsum_kernel,
        grid=(b * h,),
        in_specs=[pl.BlockSpec((None, rows, LANES), lambda i: (i, 0, 0))],
        out_specs=pl.BlockSpec((None, rows, LANES), lambda i: (i, 0, 0)),
        out_shape=jax.ShapeDtypeStruct((b * h, rows, LANES), F32),
        name="forget_cumsum",
    )(logf_hm.reshape(b * h, rows, LANES))
    return out.reshape(b, h, t)


def _band_kernel(q_ref, k_ref, v_ref, bias_ref, o_ref, kp_scr, vp_scr, *, cq, npad, nmask, nchunks):
    tk = k_ref.shape[0]
    kp_scr[0:npad, :] = jnp.zeros((npad, HEAD_DIM), BF16)
    vp_scr[0:npad, :] = jnp.zeros((npad, HEAD_DIM), BF16)
    _copy_cast(k_ref, kp_scr, tk, dst_off=npad)
    _copy_cast(v_ref, vp_scr, tk, dst_off=npad)
    bias = bias_ref[...]
    slot = lax.broadcasted_iota(jnp.int32, (cq, A_BAND), 1)
    per_step = min(nchunks, 8)

    def step(i, carry):
        rows = [pl.multiple_of((i * per_step + u) * cq, cq) for u in range(per_step)]
        scores = [_dot_nt(q_ref[pl.ds(r0, cq), :], kp_scr[pl.ds(r0, A_BAND), :]) for r0 in rows]
        probs, denoms = [], []
        for u in range(per_step):
            first_valid = jnp.maximum(nmask, npad - (i * per_step + u) * cq)
            s = jnp.where(slot >= first_valid, scores[u] + bias, -jnp.inf)
            p = jnp.exp2(s - jnp.max(s, axis=-1, keepdims=True))
            denoms.append(jnp.sum(p, axis=-1, keepdims=True))
            probs.append(p.astype(BF16))
        for u in range(per_step):
            o = _dot(probs[u], vp_scr[pl.ds(rows[u], A_BAND), :]) / denoms[u]
            o_ref[pl.ds(rows[u], cq), :] = o.astype(o_ref.dtype)
        return carry

    lax.fori_loop(0, nchunks // per_step, step, 0)


def _band_attn(q, k, v, bias, cq, npad, nmask):
    b, t, _ = q.shape
    tk = k.shape[1]
    kern = functools.partial(_band_kernel, cq=cq, npad=npad, nmask=nmask, nchunks=t // cq)
    return pl.pallas_call(
        kern,
        grid=(b, H_A),
        in_specs=[pl.BlockSpec((None, t, HEAD_DIM), lambda i, h: (i, 0, h)),
                  pl.BlockSpec((None, tk, HEAD_DIM), lambda i, h: (i, 0, h)),
                  pl.BlockSpec((None, tk, HEAD_DIM), lambda i, h: (i, 0, h)),
                  pl.BlockSpec((None, cq, A_BAND), lambda i, h: (h, 0, 0))],
        out_specs=pl.BlockSpec((None, t, HEAD_DIM), lambda i, h: (i, 0, h)),
        out_shape=jax.ShapeDtypeStruct((b, t, H_A * HEAD_DIM), BF16),
        scratch_shapes=[pltpu.VMEM((npad + tk, HEAD_DIM), BF16),
                        pltpu.VMEM((npad + tk, HEAD_DIM), BF16)],
        name="band_attn",
        compiler_params=_cparams(("arbitrary", "arbitrary"), 48),
    )(q, k, v, bias)


def _pad_lanes_left(x, lanes, fill):
    if lanes == 0:
        return x
    return jnp.concatenate([jnp.full((x.shape[0], lanes), fill, x.dtype), x], axis=1)


def _col_replicate(row):
    n = row.shape[1]
    parts = [jnp.broadcast_to(row[:, c * LANES:(c + 1) * LANES], (LANES, LANES)).T for c in range(n // LANES)]
    return parts[0] if len(parts) == 1 else jnp.concatenate(parts, axis=0)


def _aug_columns(col_rep, first):
    hi, mid, lo = _split3(col_rep)
    lane = lax.broadcasted_iota(jnp.int32, col_rep.shape, 1)
    base = 0 if first else 3
    one = jnp.where((lane >= 3 - base) & (lane < 6 - base), 1.0, 0.0)
    x = jnp.where(lane == base, hi.astype(F32),
                  jnp.where(lane == base + 1, mid.astype(F32),
                            jnp.where(lane == base + 2, lo.astype(F32), one)))
    return x.astype(BF16)


def _fox_kernel(q_ref, k_ref, v_ref, g_ref, o_ref, kp_scr, vt_scr, qp_scr, acc_scr, x_scr, row_scr,
                *, tq, tk, nq, qoff, ng):
    nblk = kp_scr.shape[0]
    ones_rows = (lax.broadcasted_iota(jnp.int32, (VT_EXTRA, tk), 0) == 0).astype(BF16)

    def prep(kb, c):
        r0 = pl.multiple_of(kb * tk, tk)
        kp_scr[kb, :, 0:HEAD_DIM] = k_ref[pl.ds(r0, tk), :].astype(BF16)
        kp_scr[kb, :, HEAD_DIM:2 * HEAD_DIM] = _aug_columns(_col_replicate(-g_ref[kb]), True)
        vt_scr[kb, 0:HEAD_DIM, :] = v_ref[pl.ds(r0, tk), :].astype(F32).T.astype(BF16)
        vt_scr[kb, HEAD_DIM:HEAD_DIM + VT_EXTRA, :] = ones_rows
        return c

    lax.fori_loop(0, nblk, prep, 0)
    off = qoff % tk

    def first_lane(u, diag):
        return u * tk if diag and tq == ng * tk else 0

    def scores(g, slot, p0):
        m_blk = None
        for u in range(ng):
            l0 = first_lane(u, p0 is not None)
            x = _dot_nt(kp_scr[g * ng + u], qp_scr[l0:tq, :])
            if p0 is not None:
                row_u = lax.broadcasted_iota(jnp.int32, (tk, tq - l0), 0)
                col_u = lax.broadcasted_iota(jnp.int32, (tk, tq - l0), 1) + l0
                x = jnp.where(row_u + ((g * ng + u) * tk - p0) <= col_u, x, -jnp.inf)
            x_scr[slot, u, :, l0:tq] = x
            m_u = jnp.max(x, axis=0, keepdims=True)
            if p0 is None:
                m_blk = m_u if m_blk is None else jnp.maximum(m_blk, m_u)
            elif u == 0:
                row_scr[...] = jnp.broadcast_to(m_u, (SUBLANES, tq))
            else:
                row_scr[:, l0:tq] = jnp.maximum(row_scr[:, l0:tq], jnp.broadcast_to(m_u, (SUBLANES, tq - l0)))
        return m_blk if p0 is None else row_scr[0:1, :]

    def process(g, slot, m_cur, m_next, diag=False):
        acc = acc_scr[...]
        if diag:
            row_scr[...] = jnp.broadcast_to(m_cur, (SUBLANES, tq))
        for u in range(ng):
            l0 = first_lane(u, diag)
            m_u = m_cur if l0 == 0 else row_scr[0:1, l0:tq]
            p = jnp.exp2(x_scr[slot, u, :, l0:tq] - m_u)
            acc = acc + _pad_lanes_left(_dot(vt_scr[g * ng + u], p.astype(BF16)), l0, 0.0)
        acc_scr[...] = acc * jnp.exp2(m_cur - m_next)

    def qblock(i, carry):
        q0 = pl.multiple_of(i * tq, tq)
        p0 = qoff + i * tq
        pb = p0 // tk
        qp_scr[:, 0:HEAD_DIM] = q_ref[pl.ds(q0, tq), :]
        if tq <= tk:
            gq = g_ref[pb][:, off:off + tq]
        else:
            gq = jnp.concatenate([g_ref[pb + c] for c in range(tq // tk)], axis=1)
        qp_scr[:, HEAD_DIM:2 * HEAD_DIM] = _aug_columns(_col_replicate(gq), False)
        acc_scr[...] = jnp.zeros((HEAD_DIM + VT_EXTRA, tq), F32)
        gd = pb // ng
        m0 = scores(gd, 0, p0)
        m_pre = scores(jnp.maximum(gd - 1, 0), 1, None)
        m1 = jnp.where(gd >= 1, jnp.maximum(m0, m_pre), m0)
        process(gd, 0, m0, m1, diag=True)

        def pair(pi, m_run):
            g1 = gd - 1 - 2 * pi
            m_a = jnp.maximum(m_run, scores(g1 - 1, 0, None))
            process(g1, 1, m_run, m_a)
            m_pre = scores(jnp.maximum(g1 - 2, 0), 1, None)
            m_b = jnp.where(g1 >= 2, jnp.maximum(m_a, m_pre), m_a)
            process(g1 - 1, 0, m_a, m_b)
            return m_b

        m_last = lax.fori_loop(0, gd // 2, pair, m1)

        @pl.when(gd % 2 == 1)
        def _():
            process(0, 1, m_last, m_last)

        denom = acc_scr[HEAD_DIM:HEAD_DIM + 1, :]
        o_ref[pl.ds(q0, tq), :] = (acc_scr[0:HEAD_DIM, :] / denom).T.astype(o_ref.dtype)
        return carry

    lax.fori_loop(0, nq, qblock, 0)


def _stick_kernel(q_ref, k_ref, v_ref, o_ref, kp_scr, vt_scr, acc_scr, x_scr, row_scr, *, tq, tk, nq, qoff, ng):
    nblk = kp_scr.shape[0]
    nv = tk // SUBLANES

    r = lax.broadcasted_iota(jnp.int32, (tk, tk), 0)
    kk = lax.broadcasted_iota(jnp.int32, (tk, tk), 1)
    perm = ((r % SUBLANES) * nv + r // SUBLANES == kk).astype(BF16)
    perm_t = ((kk % SUBLANES) * nv + kk // SUBLANES == r).astype(BF16)

    def prep(kb, c):
        r0 = pl.multiple_of(kb * tk, tk)
        kp_scr[kb] = _dot(perm, k_ref[pl.ds(r0, tk), :].astype(BF16)).astype(BF16)
        vt_scr[kb] = _dot(v_ref[pl.ds(r0, tk), :].astype(F32).T.astype(BF16), perm_t).astype(BF16)
        return c

    lax.fori_loop(0, nblk, prep, 0)

    def scores(q, g, slot):
        for u in range(ng):
            x_scr[slot, u] = _dot_nt(kp_scr[g * ng + u], q)

    def tile_terms(x, kb, p0, l0):
        beta = 1.0 / (1.0 + jnp.exp2(x))
        stay = 1.0 - beta
        valid = None
        if p0 is not None:
            row_u = lax.broadcasted_iota(jnp.int32, (tk, tq - l0), 0)
            col_u = lax.broadcasted_iota(jnp.int32, (tk, tq - l0), 1) + l0
            valid = (row_u % SUBLANES) * nv + row_u // SUBLANES + (kb * tk - p0) < col_u
            stay = jnp.where(valid, stay, 1.0)
        after = [None] * nv
        run = jnp.ones((SUBLANES, tq - l0), F32)
        for v in range(nv - 1, -1, -1):
            after[v] = run
            run = run * stay[v * SUBLANES:(v + 1) * SUBLANES, :]
        incl = run
        sub_l = lax.broadcasted_iota(jnp.int32, (SUBLANES, tq - l0), 0)
        for k in (1, 2, 4):
            incl = incl * jnp.where(sub_l + k < SUBLANES, pltpu.roll(incl, SUBLANES - k, axis=0), 1.0)
        later = jnp.where(sub_l + 1 < SUBLANES, pltpu.roll(incl, SUBLANES - 1, axis=0), 1.0)
        return after, later, incl[0:1, :], beta, valid

    def process(g, slot, carry, p0):
        acc = acc_scr[...]
        skip = p0 is not None and tq == ng * tk
        if skip:
            row_scr[...] = jnp.broadcast_to(carry, (SUBLANES, tq))
        for u in range(ng - 1, -1, -1):
            l0 = u * tk if skip else 0
            after, later, total, beta, valid = tile_terms(x_scr[slot, u, :, l0:tq], g * ng + u, p0, l0)
            scale_u = later * (row_scr[:, l0:tq] if skip else carry)
            w = jnp.concatenate([after[v] * scale_u for v in range(nv)], axis=0) * beta
            if valid is not None:
                w = jnp.where(valid, w, 0.0)
            acc = acc + _pad_lanes_left(_dot(vt_scr[g * ng + u], w.astype(BF16)), l0, 0.0)
            if skip:
                row_scr[:, l0:tq] = row_scr[:, l0:tq] * total
            else:
                carry = carry * total
        acc_scr[...] = acc
        return row_scr[0:1, :] if skip else carry

    def qblock(i, c0):
        q0 = pl.multiple_of(i * tq, tq)
        p0 = qoff + i * tq
        gd = (p0 // tk) // ng
        q = q_ref[pl.ds(q0, tq), :]
        acc_scr[...] = jnp.zeros((HEAD_DIM, tq), F32)
        scores(q, gd, 0)
        scores(q, jnp.maximum(gd - 1, 0), 1)
        carry = process(gd, 0, jnp.ones((1, tq), F32), p0)

        def pair(pi, carry):
            g1 = gd - 1 - 2 * pi
            scores(q, g1 - 1, 0)
            carry = process(g1, 1, carry, None)
            scores(q, jnp.maximum(g1 - 2, 0), 1)
            return process(g1 - 1, 0, carry, None)

        carry = lax.fori_loop(0, gd // 2, pair, carry)

        @pl.when(gd % 2 == 1)
        def _():
            process(0, 1, carry, None)

        o_ref[pl.ds(q0, tq), :] = acc_scr[...].T.astype(o_ref.dtype)
        return c0

    lax.fori_loop(0, nq, qblock, 0)


def _causal_attn(kind, q, k, v, f, head0, tq, tk, qoff, ng, layer=None):
    b, t, _ = q.shape
    if layer is None:
        tkv = k.shape[1]
        nh = k.shape[2] // HEAD_DIM
        kv_spec = pl.BlockSpec((None, tkv, HEAD_DIM), lambda i, h: (i, 0, h))
    else:
        nh, tkv = k.shape[2], k.shape[3]
        kv_spec = pl.BlockSpec((None, None, None, tkv, HEAD_DIM), lambda i, h: (layer, i, h, 0, 0))
    nblk = tkv // tk
    assert tkv % tk == 0 and nblk % ng == 0 and tq % LANES == 0 and t % tq == 0
    assert (ng * tk) % tq == 0 and qoff % tq == 0 and (tq % tk == 0 or (qoff % tk) + tq <= tk and t == tq)
    common = dict(tq=tq, tk=tk, nq=t // tq, qoff=qoff, ng=ng)
    in_specs = [pl.BlockSpec((None, t, HEAD_DIM), lambda i, h: (i, 0, head0 + h)), kv_spec, kv_spec]
    if kind == "fox":
        kern = functools.partial(_fox_kernel, **common)
        g = f.reshape(b, nh, nblk, 1, tk)
        in_specs.append(pl.BlockSpec((None, None, nblk, 1, tk), lambda i, h: (i, h, 0, 0, 0)))
        scratch = [pltpu.VMEM((nblk, tk, 2 * HEAD_DIM), BF16), pltpu.VMEM((nblk, HEAD_DIM + VT_EXTRA, tk), BF16),
                   pltpu.VMEM((tq, 2 * HEAD_DIM), BF16), pltpu.VMEM((HEAD_DIM + VT_EXTRA, tq), F32),
                   pltpu.VMEM((2, ng, tk, tq), F32), pltpu.VMEM((SUBLANES, tq), F32)]
        args = (q, k, v, g)
    else:
        kern = functools.partial(_stick_kernel, **common)
        scratch = [pltpu.VMEM((nblk, tk, HEAD_DIM), BF16), pltpu.VMEM((nblk, HEAD_DIM, tk), BF16),
                   pltpu.VMEM((HEAD_DIM, tq), F32), pltpu.VMEM((2, ng, tk, tq), F32),
                   pltpu.VMEM((SUBLANES, tq), F32)]
        args = (q, k, v)
    return pl.pallas_call(
        kern,
        grid=(b, nh),
        in_specs=in_specs,
        out_specs=pl.BlockSpec((None, t, HEAD_DIM), lambda i, h: (i, 0, h)),
        out_shape=jax.ShapeDtypeStruct((b, t, nh * HEAD_DIM), BF16),
        scratch_shapes=scratch,
        name=kind + "_attn",
        compiler_params=_cparams(("arbitrary", "arbitrary"), 52),
    )(*args)


def _outproj_kernel(oa_ref, ob_ref, oc_ref, wa_ref, wb_ref, wc_ref, x_ref, gt_ref, y_ref):
    acc = _dot(oa_ref[...], wa_ref[...]) + _dot(ob_ref[...], wb_ref[...]) + _dot(oc_ref[...], wc_ref[...])
    y_ref[...] = x_ref[...] + gt_ref[...] * acc


def _outproj(oa, ob, oc, wa, wb, wc, x2, gt, tm, tiles_per_batch):
    m = x2.shape[0]
    tn = 1024
    row = lambda a: pl.BlockSpec((tm, a.shape[1]), lambda i, j: (i, 0))
    wsp = lambda a: pl.BlockSpec((a.shape[0], tn), lambda i, j: (0, j))
    return pl.pallas_call(
        _outproj_kernel,
        grid=(m // tm, D_MODEL // tn),
        in_specs=[row(oa), row(ob), row(oc), wsp(wa), wsp(wb), wsp(wc),
                  pl.BlockSpec((tm, tn), lambda i, j: (i, j)),
                  _mod_spec(gt, tm, tn, tiles_per_batch)],
        out_specs=pl.BlockSpec((tm, tn), lambda i, j: (i, j)),
        out_shape=jax.ShapeDtypeStruct((m, D_MODEL), F32),
        name="out_proj",
        compiler_params=_cparams(("arbitrary", "arbitrary"), 40),
    )(oa, ob, oc, wa, wb, wc, x2, gt)


def _ffn1_kernel(x_ref, sc_ref, sh_ref, g_ref, wg_ref, wu_ref, a_ref, h_scr):
    @pl.when(pl.program_id(1) == 0)
    def _():
        _norm_rows(x_ref, g_ref, sc_ref, sh_ref, h_scr)

    h = h_scr[...]
    gate = _dot(h, wg_ref[...])
    up = _dot(h, wu_ref[...])
    a_ref[...] = (gate * _sigmoid(gate) * up).astype(BF16)


def _ffn1(x2, sc, sh, g, wg, wu, wl, tm, tiles_per_batch):
    m = x2.shape[0]
    dff = wg.shape[2]
    tn = 512
    full_mod = lambda a: _mod_spec(a, tm, D_MODEL, tiles_per_batch)
    return pl.pallas_call(
        _ffn1_kernel,
        grid=(m // tm, dff // tn),
        in_specs=[pl.BlockSpec((tm, D_MODEL), lambda i, j: (i, 0)),
                  full_mod(sc), full_mod(sh),
                  pl.BlockSpec((1, D_MODEL), lambda i, j: (0, 0)),
                  _weight_spec(wg, wl, tn, lambda j: j),
                  _weight_spec(wu, wl, tn, lambda j: j)],
        out_specs=pl.BlockSpec((tm, tn), lambda i, j: (i, j)),
        out_shape=jax.ShapeDtypeStruct((m, dff), BF16),
        scratch_shapes=[pltpu.VMEM((tm, D_MODEL), BF16)],
        name="ffn_gate_up",
        compiler_params=_cparams(("arbitrary", "arbitrary"), 44),
    )(x2, sc, sh, g, wg, wu)


def _ffn2_kernel(a_ref, wd_ref, x_ref, gt_ref, y_ref):
    y_ref[...] = x_ref[...] + gt_ref[...] * _dot(a_ref[...], wd_ref[...])


def _ffn2(a, wd, wl, x2, gt, tm, tiles_per_batch):
    m, dff = a.shape
    tn = 512
    return pl.pallas_call(
        _ffn2_kernel,
        grid=(m // tm, D_MODEL // tn),
        in_specs=[pl.BlockSpec((tm, dff), lambda i, j: (i, 0)),
                  _weight_spec(wd, wl, tn, lambda j: j),
                  pl.BlockSpec((tm, tn), lambda i, j: (i, j)),
                  _mod_spec(gt, tm, tn, tiles_per_batch)],
        out_specs=pl.BlockSpec((tm, tn), lambda i, j: (i, j)),
        out_shape=jax.ShapeDtypeStruct((m, D_MODEL), F32),
        name="ffn_down",
        compiler_params=_cparams(("arbitrary", "arbitrary"), 48),
    )(a, wd, x2, gt)


def _final_norm_kernel(x_ref, g_ref, y_ref, *, rows_chunk=64):
    g = g_ref[...]

    def body(r, c):
        r0 = pl.multiple_of(r * rows_chunk, rows_chunk)
        x = x_ref[pl.ds(r0, rows_chunk), :]
        ms = jnp.mean(x * x, axis=-1, keepdims=True)
        y_ref[pl.ds(r0, rows_chunk), :] = x * lax.rsqrt(ms + EPS) * g
        return c

    lax.fori_loop(0, x_ref.shape[0] // rows_chunk, body, 0)


def _final_norm(x2, g, tm):
    m = x2.shape[0]
    return pl.pallas_call(
        _final_norm_kernel,
        grid=(m // tm,),
        in_specs=[pl.BlockSpec((tm, D_MODEL), lambda i: (i, 0)),
                  pl.BlockSpec((1, D_MODEL), lambda i: (0, 0))],
        out_specs=pl.BlockSpec((tm, D_MODEL), lambda i: (i, 0)),
        out_shape=jax.ShapeDtypeStruct((m, D_MODEL), F32),
        name="final_norm",
        compiler_params=_cparams(("arbitrary",), 40),
    )(x2, g.reshape(1, D_MODEL))


def _pad_time(a, total):
    pad = total - a.shape[1]
    if pad == 0:
        return a
    return jnp.pad(a, ((0, 0), (0, pad)) + ((0, 0),) * (a.ndim - 2))


def _mix_and_ffn(x2, oa, ob, oc, mods, wts, tm, tiles_per_batch):
    _, _, gt1, sh2, sc2, gt2 = mods
    m = x2.shape[0]
    r2 = lambda a: a.reshape(m, a.shape[-1])
    x2 = _outproj(r2(oa), r2(ob), r2(oc), wts["wo_a"], wts["wo_b"], wts["wo_c"], x2, gt1, tm, tiles_per_batch)
    a = _ffn1(x2, sc2, sh2, wts["g_ffn"], wts["w_gate"], wts["w_up"], wts["layer"], tm, tiles_per_batch)
    return _ffn2(a, wts["w_down"], wts["layer"], x2, gt2, tm, tiles_per_batch)


def _prompt_layer(x2, batch, mods, wts, bias, tm, layer, depth, prev_state):
    m = x2.shape[0]
    t = m // batch
    tiles_per_batch = t // tm
    sh1, sc1 = mods[0], mods[1]
    q, ka, kb, kc, va, vb, vc, lf = _inproj(
        x2, sc1, sh1, wts["g_attn"], wts["w_in"], wts["layer"], wts["w_f"], wts["b_f"], tm, tiles_per_batch,
        state=(layer, depth, batch, prev_state))
    r3 = lambda a: a.reshape(batch, t, a.shape[-1])
    q, ka, va = map(r3, (q, ka, va))
    logf = r3(lf)[:, :, :H_B]
    keep = min(A_LEFT, t)
    small_state = (ka[:, t - keep:], va[:, t - keep:], logf)
    f = _cumsum_time(jnp.swapaxes(logf, 1, 2))
    oa = _band_attn(q, ka, va, bias, CHUNK, A_BAND - CHUNK, A_BAND - CHUNK - A_LEFT)
    tq, tk, ng = 1024, 256, 4
    ob = _causal_attn("fox", q, kb, vb, f, H_A, tq, tk, 0, ng, layer=layer)
    oc = _causal_attn("stick", q, kc, vc, None, H_A + H_B, tq, tk, 0, ng, layer=layer)
    return _mix_and_ffn(x2, oa, ob, oc, mods, wts, tm, tiles_per_batch), small_state, (kb, vb, kc, vc)


def _sample_layer(x2, batch, mods, wts, bias, caches, tm):
    m = x2.shape[0]
    t = m // batch
    sh1, sc1 = mods[0], mods[1]
    q, ka, kb, kc, va, vb, vc, lf = _inproj(x2, sc1, sh1, wts["g_attn"], wts["w_in"], wts["layer"], wts["w_f"],
                                            wts["b_f"], tm, 1)
    r3 = lambda a: a.reshape(batch, t, a.shape[-1])
    q, ka, kb, kc, va, vb, vc = map(r3, (q, ka, kb, kc, va, vb, vc))
    logf = r3(lf)[:, :, :H_B]
    cak, cav, cbk, cbv, cblogf, cck, ccv = caches
    state = (ka, va, kb, vb, logf, kc, vc)
    flat = lambda c: c.reshape(c.shape[0], c.shape[1], -1)
    past = cbk.shape[1]
    tk = 256
    tkv = past + tk

    def cat(cache, new):
        heads = cache.shape[2]
        new_hm = jnp.swapaxes(new.reshape(batch, t, heads, HEAD_DIM), 1, 2)
        both = jnp.concatenate([jnp.swapaxes(cache, 1, 2), new_hm], axis=2)
        return jnp.pad(both, ((0, 0), (0, 0), (0, tkv - both.shape[2]), (0, 0)))[None]

    kka = jnp.concatenate([flat(cak), ka], axis=1)
    vva = jnp.concatenate([flat(cav), va], axis=1)
    wa = kka.shape[1]
    oa = _band_attn(q, kka, vva, bias, t, A_BAND - wa, A_BAND - wa)
    lcat = jnp.concatenate([cblogf.astype(F32), logf], axis=1)
    fpad = -(-tkv // (SUBLANES * LANES)) * (SUBLANES * LANES)
    f = _cumsum_time(jnp.swapaxes(_pad_time(lcat, fpad), 1, 2))[:, :, :tkv]
    qpad = _pad_time(q, LANES)
    ng = tkv // tk
    ob = _causal_attn("fox", qpad, cat(cbk, kb), cat(cbv, vb), f, H_A, LANES, tk, past, ng, layer=0)[:, :t]
    oc = _causal_attn("stick", qpad, cat(cck, kc), cat(ccv, vc), None, H_A + H_B, LANES, tk, past, ng,
                      layer=0)[:, :t]
    return _mix_and_ffn(x2, oa, ob, oc, mods, wts, tm, 1), state


def kernel(x_prompt, x_sample, cache_a_k, cache_a_v, cache_b_k, cache_b_v, cache_b_logf, cache_c_k, cache_c_v,
           c_prompt, c_sample, w_ada, b_ada, g_attn, g_ffn, w_in, b_f, rel_bias, w_o, w_gate, w_up, w_down, g_final):
    depth = w_ada.shape[0]
    bp, tp, _ = x_prompt.shape
    bs, ts, _ = x_sample.shape
    n_qkv = 3 * D_MODEL

    rows = -(-(bp + bs) // SUBLANES) * SUBLANES
    c_all = jnp.concatenate([c_prompt, c_sample, jnp.zeros((rows - bp - bs, D_MODEL), F32)], axis=0)
    mod = _ada(c_all, w_ada, b_ada)

    bias_p = _relbias(rel_bias, CHUNK, A_BAND)
    bias_s = _relbias(rel_bias, ts, A_BAND)

    xp = x_prompt.reshape(bp * tp, D_MODEL)
    xs = x_sample.reshape(bs * ts, D_MODEL)
    tm_p, tm_s = 1024, bs * ts
    small_p, states_s = [], []
    big_p = None
    ea = H_A * HEAD_DIM
    eb = (H_A + H_B) * HEAD_DIM
    w_in_b, w_gate_b, w_up_b, w_down_b = (w.astype(BF16) for w in (w_in, w_gate, w_up, w_down))
    for l in range(depth):
        wo = w_o[l].astype(BF16)
        wts = dict(
            layer=l, g_attn=g_attn[l].reshape(1, D_MODEL), g_ffn=g_ffn[l].reshape(1, D_MODEL),
            w_in=w_in_b,
            w_f=jnp.pad(w_in[l, :, n_qkv:], ((0, 0), (0, LANES - H_B))).astype(BF16),
            b_f=jnp.pad(b_f[l], (0, LANES - H_B)).reshape(1, LANES),
            wo_a=wo[:ea], wo_b=wo[ea:eb], wo_c=wo[eb:],
            w_gate=w_gate_b, w_up=w_up_b, w_down=w_down_b)
        chunks = jnp.split(mod[l], 6, axis=-1)
        mods_p = [c[:bp].reshape(bp, 1, D_MODEL) for c in chunks]
        mods_s = [jnp.repeat(c[bp:bp + bs], ts, axis=0).reshape(1, bs * ts, D_MODEL) for c in chunks]
        xp, st_small, big_p = _prompt_layer(xp, bp, mods_p, wts, bias_p[l], tm_p, l, depth, big_p)
        caches = (cache_a_k[l], cache_a_v[l], cache_b_k[l], cache_b_v[l], cache_b_logf[l],
                  cache_c_k[l], cache_c_v[l])
        xs, st_s = _sample_layer(xs, bs, mods_s, wts, bias_s[l], caches, tm_s)
        small_p.append(st_small)
        states_s.append(st_s)

    def stack_heads(arrs, heads):
        a = jnp.stack(arrs)
        return a if heads is None else a.reshape(a.shape[:3] + (heads, HEAD_DIM))

    a_k_p, a_v_p, b_logf_p = [stack_heads(arrs, h) for arrs, h in zip(zip(*small_p), (H_A, H_A, None))]
    b_k_p, b_v_p, c_k_p, c_v_p = [jnp.swapaxes(a, 2, 3) for a in big_p]
    sample_states = [stack_heads(arrs, h)
                     for arrs, h in zip(zip(*states_s), (H_A, H_A, H_B, H_B, None, H_C, H_C))]
    y_prompt = _final_norm(xp, g_final, 512).reshape(bp, tp, D_MODEL)
    y_sample = _final_norm(xs, g_final, bs * ts).reshape(bs, ts, D_MODEL)
    return tuple([y_prompt, y_sample, a_k_p, a_v_p, b_k_p, b_v_p, b_logf_p, c_k_p, c_v_p] + sample_states)
```

```python
import functools

import jax
import jax.numpy as jnp
from jax import lax
from jax.experimental import pallas as pl
from jax.experimental.pallas import tpu as pltpu

F32 = jnp.float32
BF16 = jnp.bfloat16

D_MODEL = 2048
HEAD_DIM = 128
N_HEADS = D_MODEL // HEAD_DIM
H_A = N_HEADS // 4
H_B = (N_HEADS - H_A) // 2
H_C = N_HEADS - H_A - H_B
CHUNK = 64
A_LEFT = 8 * CHUNK
REL_CLIP = 128
N_REL = 2 * REL_CLIP + 1
EPS = 1e-6
ATTN_SCALE = HEAD_DIM ** -0.5
LOG2E = 1.4426950408889634
Q_SCALE = ATTN_SCALE * LOG2E

V7X_VMEM_BYTES = 64 * 1024 * 1024
LANES = 128
SUBLANES = 8
HEAD_PAIR = 2 * HEAD_DIM
A_BAND = 640
VT_EXTRA = 16

PAIRS_A = H_A // 2
PAIRS_AB = (H_A + H_B) // 2
N_PAIRS = N_HEADS // 2


def _cparams(sem, vmem_mb):
    return pltpu.CompilerParams(dimension_semantics=sem,
                                vmem_limit_bytes=min(vmem_mb * 1024 * 1024, V7X_VMEM_BYTES - (4 << 20)))


def _dot(a, b):
    return jnp.dot(a, b, preferred_element_type=F32)


def _dot_nt(a, b):
    return lax.dot_general(a, b, (((1,), (1,)), ((), ())), preferred_element_type=F32)


def _sigmoid(x):
    return 1.0 / (1.0 + jnp.exp(-x))


def _log_sigmoid(x):
    return jnp.minimum(x, 0.0) - jnp.log1p(jnp.exp(-jnp.abs(x)))


def _modulated_norm(x, g, sc, sh):
    ms = jnp.mean(x * x, axis=-1, keepdims=True)
    y = x * lax.rsqrt(ms + EPS) * g
    return y * (1.0 + sc) + sh


def _norm_rows(x_ref, g_ref, sc_ref, sh_ref, h_ref, rows_chunk=128):
    tm = x_ref.shape[0]
    rows_chunk = min(rows_chunk, tm)
    assert tm % rows_chunk == 0
    g = g_ref[...]
    per_row = sc_ref.shape[0] != 1

    def body(r, c):
        r0 = pl.multiple_of(r * rows_chunk, rows_chunk)
        sc = sc_ref[pl.ds(r0, rows_chunk), :] if per_row else sc_ref[...]
        sh = sh_ref[pl.ds(r0, rows_chunk), :] if per_row else sh_ref[...]
        h = _modulated_norm(x_ref[pl.ds(r0, rows_chunk), :], g, sc, sh)
        h_ref[pl.ds(r0, rows_chunk), :] = h.astype(h_ref.dtype)
        return c

    lax.fori_loop(0, tm // rows_chunk, body, 0)


def _copy_cast(src_ref, dst_ref, rows, dst_off=0, step=256):
    n = rows // step

    def body(i, c):
        r = pl.multiple_of(i * step, step)
        d = pl.multiple_of(dst_off + i * step, 16)
        dst_ref[pl.ds(d, step), :] = src_ref[pl.ds(r, step), :].astype(dst_ref.dtype)
        return c

    lax.fori_loop(0, n, body, 0)
    if rows % step:
        dst_ref[dst_off + n * step:dst_off + rows, :] = src_ref[n * step:rows, :].astype(dst_ref.dtype)


def _split3(x):
    hi = x.astype(BF16)
    r1 = x - hi.astype(F32)
    mid = r1.astype(BF16)
    lo = (r1 - mid.astype(F32)).astype(BF16)
    return hi, mid, lo


def _ada_kernel(c_ref, w_ref, b_ref, o_ref):
    c = c_ref[...]
    a = (c * _sigmoid(c)).astype(BF16)
    o_ref[...] = _dot(a, w_ref[...].astype(BF16)) + b_ref[...]


def _ada(c_all, w_ada, b_ada):
    depth, _, n = w_ada.shape
    rows = c_all.shape[0]
    tn = 1024
    return pl.pallas_call(
        _ada_kernel,
        grid=(depth, n // tn),
        in_specs=[pl.BlockSpec((rows, D_MODEL), lambda l, j: (0, 0)),
                  pl.BlockSpec((None, D_MODEL, tn), lambda l, j: (l, 0, j)),
                  pl.BlockSpec((None, 1, tn), lambda l, j: (l, 0, j))],
        out_specs=pl.BlockSpec((None, rows, tn), lambda l, j: (l, 0, j)),
        out_shape=jax.ShapeDtypeStruct((depth, rows, n), F32),
        name="ada_mod",
        compiler_params=_cparams(("arbitrary", "arbitrary"), 40),
    )(c_all, w_ada, b_ada.reshape(depth, 1, n))


def _relbias_kernel(tab_ref, o_ref, *, cq, band):
    l = pl.program_id(0)
    h = pl.program_id(1)
    qi = lax.broadcasted_iota(jnp.int32, (cq, band), 0)
    s = lax.broadcasted_iota(jnp.int32, (cq, band), 1)
    idx = jnp.clip(qi + (band - cq) - s, -REL_CLIP, REL_CLIP) + REL_CLIP
    base = l * (N_REL * H_A) + h

    def body(r, acc):
        return jnp.where(idx == r, tab_ref[base + r * H_A], acc)

    o_ref[...] = lax.fori_loop(0, N_REL, body, jnp.zeros((cq, band), F32)) * LOG2E


def _relbias(rel_bias, cq, band):
    depth = rel_bias.shape[0]
    return pl.pallas_call(
        functools.partial(_relbias_kernel, cq=cq, band=band),
        grid=(depth, H_A),
        in_specs=[pl.BlockSpec(memory_space=pltpu.SMEM)],
        out_specs=pl.BlockSpec((None, None, cq, band), lambda l, h: (l, h, 0, 0)),
        out_shape=jax.ShapeDtypeStruct((depth, H_A, cq, band), F32),
        name="rel_bias",
    )(rel_bias.reshape(-1))


def _inproj_kernel(*refs, n_prev, head_major):
    x_ref, sc_ref, sh_ref, g_ref, wq_ref, wk_ref, wv_ref, wf_ref, bf_ref = refs[:9]
    outs = refs[9 + n_prev:-1]
    h_scr = refs[-1]
    q_ref, ka_ref, va_ref, kb_ref, vb_ref, kc_ref, vc_ref, lf_ref = outs[:8]
    j = pl.program_id(1)

    @pl.when(j == 0)
    def _():
        _norm_rows(x_ref, g_ref, sc_ref, sh_ref, h_scr)
        lf_ref[...] = _log_sigmoid(_dot(h_scr[...], wf_ref[...]) + bf_ref[...])

    h = h_scr[...]
    q_scale = jnp.where(j >= PAIRS_AB, -Q_SCALE, Q_SCALE)
    q_ref[...] = (_dot(h, wq_ref[...].astype(BF16)) * q_scale).astype(BF16)
    kt = _dot(h, wk_ref[...].astype(BF16))
    vt = _dot(h, wv_ref[...].astype(BF16))

    @pl.when(j < PAIRS_A)
    def _():
        ka_ref[...] = kt
        va_ref[...] = vt

    def put(ref, tile):
        if head_major:
            for c in range(2):
                ref[c] = tile[:, c * HEAD_DIM:(c + 1) * HEAD_DIM]
        else:
            ref[...] = tile

    @pl.when((j >= PAIRS_A) & (j < PAIRS_AB))
    def _():
        put(kb_ref, kt)
        put(vb_ref, vt)

    @pl.when(j >= PAIRS_AB)
    def _():
        put(kc_ref, kt)
        put(vc_ref, vt)


def _weight_spec(w, layer, tn, col):
    return pl.BlockSpec((None, w.shape[1], tn), lambda i, j: (layer, 0, col(j)))


def _mod_spec(mod, tm, tn, tiles_per_batch):
    r = mod.shape[1]
    col = (lambda j: 0) if tn == mod.shape[2] else (lambda j: j)
    if r == 1:
        return pl.BlockSpec((None, 1, tn), lambda i, j: (i // tiles_per_batch, 0, col(j)))
    return pl.BlockSpec((None, tm, tn), lambda i, j: (0, i, col(j)))


def _inproj(x2, sc, sh, g, w_in, wl, w_f, b_f, tm, tiles_per_batch, state=None):
    m = x2.shape[0]
    tn = HEAD_PAIR

    def clamp(lo, n):
        return lambda i, j: (i, jnp.clip(j - lo, 0, n - 1))

    def full_mod(a):
        return _mod_spec(a, tm, D_MODEL, tiles_per_batch)

    groups = ((0, PAIRS_A), (PAIRS_A, PAIRS_AB - PAIRS_A), (PAIRS_AB, N_PAIRS - PAIRS_AB))
    out_shape = [jax.ShapeDtypeStruct((m, D_MODEL), BF16)]
    out_specs = [pl.BlockSpec((tm, tn), lambda i, j: (i, j))]
    for lo, n in groups:
        for _ in range(2):
            if state is None or lo == 0:
                out_shape.append(jax.ShapeDtypeStruct((m, n * tn), F32))
                out_specs.append(pl.BlockSpec((tm, tn), clamp(lo, n)))
            else:
                layer, depth, batch, _ = state
                out_shape.append(jax.ShapeDtypeStruct((depth, batch, 2 * n, m // batch, HEAD_DIM), F32))
                out_specs.append(pl.BlockSpec(
                    (None, None, 2, tm, HEAD_DIM),
                    lambda i, j, lo=lo, n=n: (layer, i // tiles_per_batch, jnp.clip(j - lo, 0, n - 1),
                                              i % tiles_per_batch, 0)))
    out_shape.append(jax.ShapeDtypeStruct((m, LANES), F32))
    out_specs.append(pl.BlockSpec((tm, LANES), lambda i, j: (i, 0)))
    in_specs = [pl.BlockSpec((tm, D_MODEL), lambda i, j: (i, 0)),
                full_mod(sc), full_mod(sh),
                pl.BlockSpec((1, D_MODEL), lambda i, j: (0, 0)),
                _weight_spec(w_in, wl, tn, lambda j: j),
                _weight_spec(w_in, wl, tn, lambda j: N_PAIRS + j),
                _weight_spec(w_in, wl, tn, lambda j: 2 * N_PAIRS + j),
                pl.BlockSpec((D_MODEL, LANES), lambda i, j: (0, 0)),
                pl.BlockSpec((1, LANES), lambda i, j: (0, 0))]
    args = [x2, sc, sh, g, w_in, w_in, w_in, w_f, b_f]
    aliases = {}
    prev = None if state is None else state[3]
    if prev is not None:
        first_state_out = 3
        for idx, arr in enumerate(prev):
            in_specs.append(pl.BlockSpec(memory_space=pl.ANY))
            aliases[len(args)] = first_state_out + idx
            args.append(arr)
    q, ka, va, kb, vb, kc, vc, lf = pl.pallas_call(
        functools.partial(_inproj_kernel, n_prev=0 if prev is None else len(prev), head_major=state is not None),
        grid=(m // tm, N_PAIRS),
        in_specs=in_specs,
        out_specs=out_specs,
        out_shape=out_shape,
        input_output_aliases=aliases,
        scratch_shapes=[pltpu.VMEM((tm, D_MODEL), BF16)],
        name="in_proj",
        compiler_params=_cparams(("arbitrary", "arbitrary"), 52),
    )(*args)
    return q, ka, kb, kc, va, vb, vc, lf


def _dot_f32_by_01(x, ones01):
    hi, mid, lo = _split3(x)
    return _dot(hi, ones01) + _dot(mid, ones01) + _dot(lo, ones01)


def _cumsum_kernel(x_ref, o_ref):
    x = x_ref[...]
    rows = x.shape[0]
    i0 = lax.broadcasted_iota(jnp.int32, (LANES, LANES), 0)
    i1 = lax.broadcasted_iota(jnp.int32, (LANES, LANES), 1)
    upper = (i0 <= i1).astype(BF16)
    c = _dot_f32_by_01(x, upper)
    tot = jnp.broadcast_to(c[:, LANES - 1:LANES], (rows, LANES))
    r0 = lax.broadcasted_iota(jnp.int32, (rows, rows), 0)
    r1 = lax.broadcasted_iota(jnp.int32, (rows, rows), 1)
    lower = (r1 < r0).astype(BF16)
    hi, mid, lo = _split3(tot)
    off = _dot(lower, hi) + _dot(lower, mid) + _dot(lower, lo)
    o_ref[...] = (c + off) * LOG2E


def _cumsum_time(logf_hm):
    b, h, t = logf_hm.shape
    rows = t // LANES
    out = pl.pallas_call(
        _cumsum_kernel,
        grid=(b * h,),
        in_specs=[pl.BlockSpec((None, rows, LANES), lambda i: (i, 0, 0))],
        out_specs=pl.BlockSpec((None, rows, LANES), lambda i: (i, 0, 0)),
        out_shape=jax.ShapeDtypeStruct((b * h, rows, LANES), F32),
        name="forget_cumsum",
    )(logf_hm.reshape(b * h, rows, LANES))
    return out.reshape(b, h, t)


def _band_kernel(q_ref, k_ref, v_ref, bias_ref, o_ref, kp_scr, vp_scr, *, cq, npad, nmask, nchunks):
    tk = k_ref.shape[0]
    kp_scr[0:npad, :] = jnp.zeros((npad, HEAD_DIM), BF16)
    vp_scr[0:npad, :] = jnp.zeros((npad, HEAD_DIM), BF16)
    _copy_cast(k_ref, kp_scr, tk, dst_off=npad)
    _copy_cast(v_ref, vp_scr, tk, dst_off=npad)
    bias = bias_ref[...]
    slot = lax.broadcasted_iota(jnp.int32, (cq, A_BAND), 1)
    per_step = min(nchunks, 8)

    def step(i, carry):
        rows = [pl.multiple_of((i * per_step + u) * cq, cq) for u in range(per_step)]
        scores = [_dot_nt(q_ref[pl.ds(r0, cq), :], kp_scr[pl.ds(r0, A_BAND), :]) for r0 in rows]
        probs, denoms = [], []
        for u in range(per_step):
            first_valid = jnp.maximum(nmask, npad - (i * per_step + u) * cq)
            s = jnp.where(slot >= first_valid, scores[u] + bias, -jnp.inf)
            p = jnp.exp2(s - jnp.max(s, axis=-1, keepdims=True))
            denoms.append(jnp.sum(p, axis=-1, keepdims=True))
            probs.append(p.astype(BF16))
        for u in range(per_step):
            o = _dot(probs[u], vp_scr[pl.ds(rows[u], A_BAND), :]) / denoms[u]
            o_ref[pl.ds(rows[u], cq), :] = o.astype(o_ref.dtype)
        return carry

    lax.fori_loop(0, nchunks // per_step, step, 0)


def _band_attn(q, k, v, bias, cq, npad, nmask):
    b, t, _ = q.shape
    tk = k.shape[1]
    kern = functools.partial(_band_kernel, cq=cq, npad=npad, nmask=nmask, nchunks=t // cq)
    return pl.pallas_call(
        kern,
        grid=(b, H_A),
        in_specs=[pl.BlockSpec((None, t, HEAD_DIM), lambda i, h: (i, 0, h)),
                  pl.BlockSpec((None, tk, HEAD_DIM), lambda i, h: (i, 0, h)),
                  pl.BlockSpec((None, tk, HEAD_DIM), lambda i, h: (i, 0, h)),
                  pl.BlockSpec((None, cq, A_BAND), lambda i, h: (h, 0, 0))],
        out_specs=pl.BlockSpec((None, t, HEAD_DIM), lambda i, h: (i, 0, h)),
        out_shape=jax.ShapeDtypeStruct((b, t, H_A * HEAD_DIM), BF16),
        scratch_shapes=[pltpu.VMEM((npad + tk, HEAD_DIM), BF16),
                        pltpu.VMEM((npad + tk, HEAD_DIM), BF16)],
        name="band_attn",
        compiler_params=_cparams(("arbitrary", "arbitrary"), 48),
    )(q, k, v, bias)


def _pad_lanes_left(x, lanes, fill):
    if lanes == 0:
        return x
    return jnp.concatenate([jnp.full((x.shape[0], lanes), fill, x.dtype), x], axis=1)


def _col_replicate(row):
    n = row.shape[1]
    parts = [jnp.broadcast_to(row[:, c * LANES:(c + 1) * LANES], (LANES, LANES)).T for c in range(n // LANES)]
    return parts[0] if len(parts) == 1 else jnp.concatenate(parts, axis=0)


def _aug_columns(col_rep, first):
    hi, mid, lo = _split3(col_rep)
    lane = lax.broadcasted_iota(jnp.int32, col_rep.shape, 1)
    base = 0 if first else 3
    one = jnp.where((lane >= 3 - base) & (lane < 6 - base), 1.0, 0.0)
    x = jnp.where(lane == base, hi.astype(F32),
                  jnp.where(lane == base + 1, mid.astype(F32),
                            jnp.where(lane == base + 2, lo.astype(F32), one)))
    return x.astype(BF16)


def _fox_kernel(q_ref, k_ref, v_ref, g_ref, o_ref, kp_scr, vt_scr, qp_scr, acc_scr, x_scr, row_scr,
                *, tq, tk, nq, qoff, ng):
    nblk = kp_scr.shape[0]
    ones_rows = (lax.broadcasted_iota(jnp.int32, (VT_EXTRA, tk), 0) == 0).astype(BF16)

    def prep(kb, c):
        r0 = pl.multiple_of(kb * tk, tk)
        kp_scr[kb, :, 0:HEAD_DIM] = k_ref[pl.ds(r0, tk), :].astype(BF16)
        kp_scr[kb, :, HEAD_DIM:2 * HEAD_DIM] = _aug_columns(_col_replicate(-g_ref[kb]), True)
        vt_scr[kb, 0:HEAD_DIM, :] = v_ref[pl.ds(r0, tk), :].astype(F32).T.astype(BF16)
        vt_scr[kb, HEAD_DIM:HEAD_DIM + VT_EXTRA, :] = ones_rows
        return c

    lax.fori_loop(0, nblk, prep, 0)
    off = qoff % tk

    def first_lane(u, diag):
        return u * tk if diag and tq == ng * tk else 0

    def scores(g, slot, p0):
        m_blk = None
        for u in range(ng):
            l0 = first_lane(u, p0 is not None)
            x = _dot_nt(kp_scr[g * ng + u], qp_scr[l0:tq, :])
            if p0 is not None:
                row_u = lax.broadcasted_iota(jnp.int32, (tk, tq - l0), 0)
                col_u = lax.broadcasted_iota(jnp.int32, (tk, tq - l0), 1) + l0
                x = jnp.where(row_u + ((g * ng + u) * tk - p0) <= col_u, x, -jnp.inf)
            x_scr[slot, u, :, l0:tq] = x
            m_u = jnp.max(x, axis=0, keepdims=True)
            if p0 is None:
                m_blk = m_u if m_blk is None else jnp.maximum(m_blk, m_u)
            elif u == 0:
                row_scr[...] = jnp.broadcast_to(m_u, (SUBLANES, tq))
            else:
                row_scr[:, l0:tq] = jnp.maximum(row_scr[:, l0:tq], jnp.broadcast_to(m_u, (SUBLANES, tq - l0)))
        return m_blk if p0 is None else row_scr[0:1, :]

    def process(g, slot, m_cur, m_next, diag=False):
        acc = acc_scr[...]
        if diag:
            row_scr[...] = jnp.broadcast_to(m_cur, (SUBLANES, tq))
        for u in range(ng):
            l0 = first_lane(u, diag)
            m_u = m_cur if l0 == 0 else row_scr[0:1, l0:tq]
            p = jnp.exp2(x_scr[slot, u, :, l0:tq] - m_u)
            acc = acc + _pad_lanes_left(_dot(vt_scr[g * ng + u], p.astype(BF16)), l0, 0.0)
        acc_scr[...] = acc * jnp.exp2(m_cur - m_next)

    def qblock(i, carry):
        q0 = pl.multiple_of(i * tq, tq)
        p0 = qoff + i * tq
        pb = p0 // tk
        qp_scr[:, 0:HEAD_DIM] = q_ref[pl.ds(q0, tq), :]
        if tq <= tk:
            gq = g_ref[pb][:, off:off + tq]
        else:
            gq = jnp.concatenate([g_ref[pb + c] for c in range(tq // tk)], axis=1)
        qp_scr[:, HEAD_DIM:2 * HEAD_DIM] = _aug_columns(_col_replicate(gq), False)
        acc_scr[...] = jnp.zeros((HEAD_DIM + VT_EXTRA, tq), F32)
        gd = pb // ng
        m0 = scores(gd, 0, p0)
        m_pre = scores(jnp.maximum(gd - 1, 0), 1, None)
        m1 = jnp.where(gd >= 1, jnp.maximum(m0, m_pre), m0)
        process(gd, 0, m0, m1, diag=True)

        def pair(pi, m_run):
            g1 = gd - 1 - 2 * pi
            m_a = jnp.maximum(m_run, scores(g1 - 1, 0, None))
            process(g1, 1, m_run, m_a)
            m_pre = scores(jnp.maximum(g1 - 2, 0), 1, None)
            m_b = jnp.where(g1 >= 2, jnp.maximum(m_a, m_pre), m_a)
            process(g1 - 1, 0, m_a, m_b)
            return m_b

        m_last = lax.fori_loop(0, gd // 2, pair, m1)

        @pl.when(gd % 2 == 1)
        def _():
            process(0, 1, m_last, m_last)

        denom = acc_scr[HEAD_DIM:HEAD_DIM + 1, :]
        o_ref[pl.ds(q0, tq), :] = (acc_scr[0:HEAD_DIM, :] / denom).T.astype(o_ref.dtype)
        return carry

    lax.fori_loop(0, nq, qblock, 0)


def _stick_kernel(q_ref, k_ref, v_ref, o_ref, kp_scr, vt_scr, acc_scr, x_scr, row_scr, *, tq, tk, nq, qoff, ng):
    nblk = kp_scr.shape[0]
    nv = tk // SUBLANES

    r = lax.broadcasted_iota(jnp.int32, (tk, tk), 0)
    kk = lax.broadcasted_iota(jnp.int32, (tk, tk), 1)
    perm = ((r % SUBLANES) * nv + r // SUBLANES == kk).astype(BF16)
    perm_t = ((kk % SUBLANES) * nv + kk // SUBLANES == r).astype(BF16)

    def prep(kb, c):
        r0 = pl.multiple_of(kb * tk, tk)
        kp_scr[kb] = _dot(perm, k_ref[pl.ds(r0, tk), :].astype(BF16)).astype(BF16)
        vt_scr[kb] = _dot(v_ref[pl.ds(r0, tk), :].astype(F32).T.astype(BF16), perm_t).astype(BF16)
        return c

    lax.fori_loop(0, nblk, prep, 0)

    def scores(q, g, slot):
        for u in range(ng):
            x_scr[slot, u] = _dot_nt(kp_scr[g * ng + u], q)

    def tile_terms(x, kb, p0, l0):
        beta = 1.0 / (1.0 + jnp.exp2(x))
        stay = 1.0 - beta
        valid = None
        if p0 is not None:
            row_u = lax.broadcasted_iota(jnp.int32, (tk, tq - l0), 0)
            col_u = lax.broadcasted_iota(jnp.int32, (tk, tq - l0), 1) + l0
            valid = (row_u % SUBLANES) * nv + row_u // SUBLANES + (kb * tk - p0) < col_u
            stay = jnp.where(valid, stay, 1.0)
        after = [None] * nv
        run = jnp.ones((SUBLANES, tq - l0), F32)
        for v in range(nv - 1, -1, -1):
            after[v] = run
            run = run * stay[v * SUBLANES:(v + 1) * SUBLANES, :]
        incl = run
        sub_l = lax.broadcasted_iota(jnp.int32, (SUBLANES, tq - l0), 0)
        for k in (1, 2, 4):
            incl = incl * jnp.where(sub_l + k < SUBLANES, pltpu.roll(incl, SUBLANES - k, axis=0), 1.0)
        later = jnp.where(sub_l + 1 < SUBLANES, pltpu.roll(incl, SUBLANES - 1, axis=0), 1.0)
        return after, later, incl[0:1, :], beta, valid

    def process(g, slot, carry, p0):
        acc = acc_scr[...]
        skip = p0 is not None and tq == ng * tk
        if skip:
            row_scr[...] = jnp.broadcast_to(carry, (SUBLANES, tq))
        for u in range(ng - 1, -1, -1):
            l0 = u * tk if skip else 0
            after, later, total, beta, valid = tile_terms(x_scr[slot, u, :, l0:tq], g * ng + u, p0, l0)
            scale_u = later * (row_scr[:, l0:tq] if skip else carry)
            w = jnp.concatenate([after[v] * scale_u for v in range(nv)], axis=0) * beta
            if valid is not None:
                w = jnp.where(valid, w, 0.0)
            acc = acc + _pad_lanes_left(_dot(vt_scr[g * ng + u], w.astype(BF16)), l0, 0.0)
            if skip:
                row_scr[:, l0:tq] = row_scr[:, l0:tq] * total
            else:
                carry = carry * total
        acc_scr[...] = acc
        return row_scr[0:1, :] if skip else carry

    def qblock(i, c0):
        q0 = pl.multiple_of(i * tq, tq)
        p0 = qoff + i * tq
        gd = (p0 // tk) // ng
        q = q_ref[pl.ds(q0, tq), :]
        acc_scr[...] = jnp.zeros((HEAD_DIM, tq), F32)
        scores(q, gd, 0)
        scores(q, jnp.maximum(gd - 1, 0), 1)
        carry = process(gd, 0, jnp.ones((1, tq), F32), p0)

        def pair(pi, carry):
            g1 = gd - 1 - 2 * pi
            scores(q, g1 - 1, 0)
            carry = process(g1, 1, carry, None)
            scores(q, jnp.maximum(g1 - 2, 0), 1)
            return process(g1 - 1, 0, carry, None)

        carry = lax.fori_loop(0, gd // 2, pair, carry)

        @pl.when(gd % 2 == 1)
        def _():
            process(0, 1, carry, None)

        o_ref[pl.ds(q0, tq), :] = acc_scr[...].T.astype(o_ref.dtype)
        return c0

    lax.fori_loop(0, nq, qblock, 0)


def _causal_attn(kind, q, k, v, f, head0, tq, tk, qoff, ng, layer=None):
    b, t, _ = q.shape
    if layer is None:
        tkv = k.shape[1]
        nh = k.shape[2] // HEAD_DIM
        kv_spec = pl.BlockSpec((None, tkv, HEAD_DIM), lambda i, h: (i, 0, h))
    else:
        nh, tkv = k.shape[2], k.shape[3]
        kv_spec = pl.BlockSpec((None, None, None, tkv, HEAD_DIM), lambda i, h: (layer, i, h, 0, 0))
    nblk = tkv // tk
    assert tkv % tk == 0 and nblk % ng == 0 and tq % LANES == 0 and t % tq == 0
    assert (ng * tk) % tq == 0 and qoff % tq == 0 and (tq % tk == 0 or (qoff % tk) + tq <= tk and t == tq)
    common = dict(tq=tq, tk=tk, nq=t // tq, qoff=qoff, ng=ng)
    in_specs = [pl.BlockSpec((None, t, HEAD_DIM), lambda i, h: (i, 0, head0 + h)), kv_spec, kv_spec]
    if kind == "fox":
        kern = functools.partial(_fox_kernel, **common)
        g = f.reshape(b, nh, nblk, 1, tk)
        in_specs.append(pl.BlockSpec((None, None, nblk, 1, tk), lambda i, h: (i, h, 0, 0, 0)))
        scratch = [pltpu.VMEM((nblk, tk, 2 * HEAD_DIM), BF16), pltpu.VMEM((nblk, HEAD_DIM + VT_EXTRA, tk), BF16),
                   pltpu.VMEM((tq, 2 * HEAD_DIM), BF16), pltpu.VMEM((HEAD_DIM + VT_EXTRA, tq), F32),
                   pltpu.VMEM((2, ng, tk, tq), F32), pltpu.VMEM((SUBLANES, tq), F32)]
        args = (q, k, v, g)
    else:
        kern = functools.partial(_stick_kernel, **common)
        scratch = [pltpu.VMEM((nblk, tk, HEAD_DIM), BF16), pltpu.VMEM((nblk, HEAD_DIM, tk), BF16),
                   pltpu.VMEM((HEAD_DIM, tq), F32), pltpu.VMEM((2, ng, tk, tq), F32),
                   pltpu.VMEM((SUBLANES, tq), F32)]
        args = (q, k, v)
    return pl.pallas_call(
        kern,
        grid=(b, nh),
        in_specs=in_specs,
        out_specs=pl.BlockSpec((None, t, HEAD_DIM), lambda i, h: (i, 0, h)),
        out_shape=jax.ShapeDtypeStruct((b, t, nh * HEAD_DIM), BF16),
        scratch_shapes=scratch,
        name=kind + "_attn",
        compiler_params=_cparams(("arbitrary", "arbitrary"), 52),
    )(*args)


def _outproj_kernel(oa_ref, ob_ref, oc_ref, wa_ref, wb_ref, wc_ref, x_ref, gt_ref, y_ref):
    acc = _dot(oa_ref[...], wa_ref[...]) + _dot(ob_ref[...], wb_ref[...]) + _dot(oc_ref[...], wc_ref[...])
    y_ref[...] = x_ref[...] + gt_ref[...] * acc


def _outproj(oa, ob, oc, wa, wb, wc, x2, gt, tm, tiles_per_batch):
    m = x2.shape[0]
    tn = 1024
    row = lambda a: pl.BlockSpec((tm, a.shape[1]), lambda i, j: (i, 0))
    wsp = lambda a: pl.BlockSpec((a.shape[0], tn), lambda i, j: (0, j))
    return pl.pallas_call(
        _outproj_kernel,
        grid=(m // tm, D_MODEL // tn),
        in_specs=[row(oa), row(ob), row(oc), wsp(wa), wsp(wb), wsp(wc),
                  pl.BlockSpec((tm, tn), lambda i, j: (i, j)),
                  _mod_spec(gt, tm, tn, tiles_per_batch)],
        out_specs=pl.BlockSpec((tm, tn), lambda i, j: (i, j)),
        out_shape=jax.ShapeDtypeStruct((m, D_MODEL), F32),
        name="out_proj",
        compiler_params=_cparams(("arbitrary", "arbitrary"), 40),
    )(oa, ob, oc, wa, wb, wc, x2, gt)


def _ffn1_kernel(x_ref, sc_ref, sh_ref, g_ref, wg_ref, wu_ref, a_ref, h_scr):
    @pl.when(pl.program_id(1) == 0)
    def _():
        _norm_rows(x_ref, g_ref, sc_ref, sh_ref, h_scr)

    h = h_scr[...]
    gate = _dot(h, wg_ref[...])
    up = _dot(h, wu_ref[...])
    a_ref[...] = (gate * _sigmoid(gate) * up).astype(BF16)


def _ffn1(x2, sc, sh, g, wg, wu, wl, tm, tiles_per_batch):
    m = x2.shape[0]
    dff = wg.shape[2]
    tn = 512
    full_mod = lambda a: _mod_spec(a, tm, D_MODEL, tiles_per_batch)
    return pl.pallas_call(
        _ffn1_kernel,
        grid=(m // tm, dff // tn),
        in_specs=[pl.BlockSpec((tm, D_MODEL), lambda i, j: (i, 0)),
                  full_mod(sc), full_mod(sh),
                  pl.BlockSpec((1, D_MODEL), lambda i, j: (0, 0)),
                  _weight_spec(wg, wl, tn, lambda j: j),
                  _weight_spec(wu, wl, tn, lambda j: j)],
        out_specs=pl.BlockSpec((tm, tn), lambda i, j: (i, j)),
        out_shape=jax.ShapeDtypeStruct((m, dff), BF16),
        scratch_shapes=[pltpu.VMEM((tm, D_MODEL), BF16)],
        name="ffn_gate_up",
        compiler_params=_cparams(("arbitrary", "arbitrary"), 44),
    )(x2, sc, sh, g, wg, wu)


def _ffn2_kernel(a_ref, wd_ref, x_ref, gt_ref, y_ref):
    y_ref[...] = x_ref[...] + gt_ref[...] * _dot(a_ref[...], wd_ref[...])


def _ffn2(a, wd, wl, x2, gt, tm, tiles_per_batch):
    m, dff = a.shape
    tn = 512
    return pl.pallas_call(
        _ffn2_kernel,
        grid=(m // tm, D_MODEL // tn),
        in_specs=[pl.BlockSpec((tm, dff), lambda i, j: (i, 0)),
                  _weight_spec(wd, wl, tn, lambda j: j),
                  pl.BlockSpec((tm, tn), lambda i, j: (i, j)),
                  _mod_spec(gt, tm, tn, tiles_per_batch)],
        out_specs=pl.BlockSpec((tm, tn), lambda i, j: (i, j)),
        out_shape=jax.ShapeDtypeStruct((m, D_MODEL), F32),
        name="ffn_down",
        compiler_params=_cparams(("arbitrary", "arbitrary"), 48),
    )(a, wd, x2, gt)


def _final_norm_kernel(x_ref, g_ref, y_ref, *, rows_chunk=64):
    g = g_ref[...]

    def body(r, c):
        r0 = pl.multiple_of(r * rows_chunk, rows_chunk)
        x = x_ref[pl.ds(r0, rows_chunk), :]
        ms = jnp.mean(x * x, axis=-1, keepdims=True)
        y_ref[pl.ds(r0, rows_chunk), :] = x * lax.rsqrt(ms + EPS) * g
        return c

    lax.fori_loop(0, x_ref.shape[0] // rows_chunk, body, 0)


def _final_norm(x2, g, tm):
    m = x2.shape[0]
    return pl.pallas_call(
        _final_norm_kernel,
        grid=(m // tm,),
        in_specs=[pl.BlockSpec((tm, D_MODEL), lambda i: (i, 0)),
                  pl.BlockSpec((1, D_MODEL), lambda i: (0, 0))],
        out_specs=pl.BlockSpec((tm, D_MODEL), lambda i: (i, 0)),
        out_shape=jax.ShapeDtypeStruct((m, D_MODEL), F32),
        name="final_norm",
        compiler_params=_cparams(("arbitrary",), 40),
    )(x2, g.reshape(1, D_MODEL))


def _pad_time(a, total):
    pad = total - a.shape[1]
    if pad == 0:
        return a
    return jnp.pad(a, ((0, 0), (0, pad)) + ((0, 0),) * (a.ndim - 2))


def _mix_and_ffn(x2, oa, ob, oc, mods, wts, tm, tiles_per_batch):
    _, _, gt1, sh2, sc2, gt2 = mods
    m = x2.shape[0]
    r2 = lambda a: a.reshape(m, a.shape[-1])
    x2 = _outproj(r2(oa), r2(ob), r2(oc), wts["wo_a"], wts["wo_b"], wts["wo_c"], x2, gt1, tm, tiles_per_batch)
    a = _ffn1(x2, sc2, sh2, wts["g_ffn"], wts["w_gate"], wts["w_up"], wts["layer"], tm, tiles_per_batch)
    return _ffn2(a, wts["w_down"], wts["layer"], x2, gt2, tm, tiles_per_batch)


def _prompt_layer(x2, batch, mods, wts, bias, tm, layer, depth, prev_state):
    m = x2.shape[0]
    t = m // batch
    tiles_per_batch = t // tm
    sh1, sc1 = mods[0], mods[1]
    q, ka, kb, kc, va, vb, vc, lf = _inproj(
        x2, sc1, sh1, wts["g_attn"], wts["w_in"], wts["layer"], wts["w_f"], wts["b_f"], tm, tiles_per_batch,
        state=(layer, depth, batch, prev_state))
    r3 = lambda a: a.reshape(batch, t, a.shape[-1])
    q, ka, va = map(r3, (q, ka, va))
    logf = r3(lf)[:, :, :H_B]
    keep = min(A_LEFT, t)
    small_state = (ka[:, t - keep:], va[:, t - keep:], logf)
    f = _cumsum_time(jnp.swapaxes(logf, 1, 2))
    oa = _band_attn(q, ka, va, bias, CHUNK, A_BAND - CHUNK, A_BAND - CHUNK - A_LEFT)
    tq, tk, ng = 1024, 256, 4
    ob = _causal_attn("fox", q, kb, vb, f, H_A, tq, tk, 0, ng, layer=layer)
    oc = _causal_attn("stick", q, kc, vc, None, H_A + H_B, tq, tk, 0, ng, layer=layer)
    return _mix_and_ffn(x2, oa, ob, oc, mods, wts, tm, tiles_per_batch), small_state, (kb, vb, kc, vc)


def _sample_layer(x2, batch, mods, wts, bias, caches, tm):
    m = x2.shape[0]
    t = m // batch
    sh1, sc1 = mods[0], mods[1]
    q, ka, kb, kc, va, vb, vc, lf = _inproj(x2, sc1, sh1, wts["g_attn"], wts["w_in"], wts["layer"], wts["w_f"],
                                            wts["b_f"], tm, 1)
    r3 = lambda a: a.reshape(batch, t, a.shape[-1])
    q, ka, kb, kc, va, vb, vc = map(r3, (q, ka, kb, kc, va, vb, vc))
    logf = r3(lf)[:, :, :H_B]
    cak, cav, cbk, cbv, cblogf, cck, ccv = caches
    state = (ka, va, kb, vb, logf, kc, vc)
    flat = lambda c: c.reshape(c.shape[0], c.shape[1], -1)
    past = cbk.shape[1]
    tk = 256
    tkv = past + tk

    def cat(cache, new):
        heads = cache.shape[2]
        new_hm = jnp.swapaxes(new.reshape(batch, t, heads, HEAD_DIM), 1, 2)
        both = jnp.concatenate([jnp.swapaxes(cache, 1, 2), new_hm], axis=2)
        return jnp.pad(both, ((0, 0), (0, 0), (0, tkv - both.shape[2]), (0, 0)))[None]

    kka = jnp.concatenate([flat(cak), ka], axis=1)
    vva = jnp.concatenate([flat(cav), va], axis=1)
    wa = kka.shape[1]
    oa = _band_attn(q, kka, vva, bias, t, A_BAND - wa, A_BAND - wa)
    lcat = jnp.concatenate([cblogf.astype(F32), logf], axis=1)
    fpad = -(-tkv // (SUBLANES * LANES)) * (SUBLANES * LANES)
    f = _cumsum_time(jnp.swapaxes(_pad_time(lcat, fpad), 1, 2))[:, :, :tkv]
    qpad = _pad_time(q, LANES)
    ng = tkv // tk
    ob = _causal_attn("fox", qpad, cat(cbk, kb), cat(cbv, vb), f, H_A, LANES, tk, past, ng, layer=0)[:, :t]
    oc = _causal_attn("stick", qpad, cat(cck, kc), cat(ccv, vc), None, H_A + H_B, LANES, tk, past, ng,
                      layer=0)[:, :t]
    return _mix_and_ffn(x2, oa, ob, oc, mods, wts, tm, 1), state


def kernel(x_prompt, x_sample, cache_a_k, cache_a_v, cache_b_k, cache_b_v, cache_b_logf, cache_c_k, cache_c_v,
           c_prompt, c_sample, w_ada, b_ada, g_attn, g_ffn, w_in, b_f, rel_bias, w_o, w_gate, w_up, w_down, g_final):
    depth = w_ada.shape[0]
    bp, tp, _ = x_prompt.shape
    bs, ts, _ = x_sample.shape
    n_qkv = 3 * D_MODEL

    rows = -(-(bp + bs) // SUBLANES) * SUBLANES
    c_all = jnp.concatenate([c_prompt, c_sample, jnp.zeros((rows - bp - bs, D_MODEL), F32)], axis=0)
    mod = _ada(c_all, w_ada, b_ada)

    bias_p = _relbias(rel_bias, CHUNK, A_BAND)
    bias_s = _relbias(rel_bias, ts, A_BAND)

    xp = x_prompt.reshape(bp * tp, D_MODEL)
    xs = x_sample.reshape(bs * ts, D_MODEL)
    tm_p, tm_s = 1024, bs * ts
    small_p, states_s = [], []
    big_p = None
    ea = H_A * HEAD_DIM
    eb = (H_A + H_B) * HEAD_DIM
    w_gate_b, w_up_b, w_down_b = (w.astype(BF16) for w in (w_gate, w_up, w_down))
    for l in range(depth):
        wo = w_o[l].astype(BF16)
        wts = dict(
            layer=l, g_attn=g_attn[l].reshape(1, D_MODEL), g_ffn=g_ffn[l].reshape(1, D_MODEL),
            w_in=w_in,
            w_f=jnp.pad(w_in[l, :, n_qkv:], ((0, 0), (0, LANES - H_B))).astype(BF16),
            b_f=jnp.pad(b_f[l], (0, LANES - H_B)).reshape(1, LANES),
            wo_a=wo[:ea], wo_b=wo[ea:eb], wo_c=wo[eb:],
            w_gate=w_gate_b, w_up=w_up_b, w_down=w_down_b)
        chunks = jnp.split(mod[l], 6, axis=-1)
        mods_p = [c[:bp].reshape(bp, 1, D_MODEL) for c in chunks]
        mods_s = [jnp.repeat(c[bp:bp + bs], ts, axis=0).reshape(1, bs * ts, D_MODEL) for c in chunks]
        xp, st_small, big_p = _prompt_layer(xp, bp, mods_p, wts, bias_p[l], tm_p, l, depth, big_p)
        caches = (cache_a_k[l], cache_a_v[l], cache_b_k[l], cache_b_v[l], cache_b_logf[l],
                  cache_c_k[l], cache_c_v[l])
        xs, st_s = _sample_layer(xs, bs, mods_s, wts, bias_s[l], caches, tm_s)
        small_p.append(st_small)
        states_s.append(st_s)

    def stack_heads(arrs, heads):
        a = jnp.stack(arrs)
        return a if heads is None else a.reshape(a.shape[:3] + (heads, HEAD_DIM))

    a_k_p, a_v_p, b_logf_p = [stack_heads(arrs, h) for arrs, h in zip(zip(*small_p), (H_A, H_A, None))]
    b_k_p, b_v_p, c_k_p, c_v_p = [jnp.swapaxes(a, 2, 3) for a in big_p]
    sample_states = [stack_heads(arrs, h)
                     for arrs, h in zip(zip(*states_s), (H_A, H_A, H_B, H_B, None, H_C, H_C))]
    y_prompt = _final_norm(xp, g_final, 512).reshape(bp, tp, D_MODEL)
    y_sample = _final_norm(xs, g_final, bs * ts).reshape(bs, ts, D_MODEL)
    return tuple([y_prompt, y_sample, a_k_p, a_v_p, b_k_p, b_v_p, b_logf_p, c_k_p, c_v_p] + sample_states)
```

```python
import functools

import jax
import jax.numpy as jnp
from jax import lax
from jax.experimental import pallas as pl
from jax.experimental.pallas import tpu as pltpu

F32 = jnp.float32
BF16 = jnp.bfloat16

D_MODEL = 2048
HEAD_DIM = 128
N_HEADS = D_MODEL // HEAD_DIM
H_A = N_HEADS // 4
H_B = (N_HEADS - H_A) // 2
H_C = N_HEADS - H_A - H_B
CHUNK = 64
A_LEFT = 8 * CHUNK
REL_CLIP = 128
N_REL = 2 * REL_CLIP + 1
EPS = 1e-6
ATTN_SCALE = HEAD_DIM ** -0.5
LOG2E = 1.4426950408889634
Q_SCALE = ATTN_SCALE * LOG2E

V7X_VMEM_BYTES = 64 * 1024 * 1024
LANES = 128
SUBLANES = 8
HEAD_PAIR = 2 * HEAD_DIM
A_BAND = 640
VT_EXTRA = 16

PAIRS_A = H_A // 2
PAIRS_AB = (H_A + H_B) // 2
N_PAIRS = N_HEADS // 2


def _cparams(sem, vmem_mb):
    return pltpu.CompilerParams(dimension_semantics=sem,
                                vmem_limit_bytes=min(vmem_mb * 1024 * 1024, V7X_VMEM_BYTES - (4 << 20)))


def _dot(a, b):
    return jnp.dot(a, b, preferred_element_type=F32)


def _dot_nt(a, b):
    return lax.dot_general(a, b, (((1,), (1,)), ((), ())), preferred_element_type=F32)


def _sigmoid(x):
    return 1.0 / (1.0 + jnp.exp(-x))


def _log_sigmoid(x):
    return jnp.minimum(x, 0.0) - jnp.log1p(jnp.exp(-jnp.abs(x)))


def _modulated_norm(x, g, sc, sh):
    ms = jnp.mean(x * x, axis=-1, keepdims=True)
    y = x * lax.rsqrt(ms + EPS) * g
    return y * (1.0 + sc) + sh


def _norm_rows(x_ref, g_ref, sc_ref, sh_ref, h_ref, rows_chunk=128):
    tm = x_ref.shape[0]
    rows_chunk = min(rows_chunk, tm)
    assert tm % rows_chunk == 0
    g = g_ref[...]
    per_row = sc_ref.shape[0] != 1

    def body(r, c):
        r0 = pl.multiple_of(r * rows_chunk, rows_chunk)
        sc = sc_ref[pl.ds(r0, rows_chunk), :] if per_row else sc_ref[...]
        sh = sh_ref[pl.ds(r0, rows_chunk), :] if per_row else sh_ref[...]
        h = _modulated_norm(x_ref[pl.ds(r0, rows_chunk), :], g, sc, sh)
        h_ref[pl.ds(r0, rows_chunk), :] = h.astype(h_ref.dtype)
        return c

    lax.fori_loop(0, tm // rows_chunk, body, 0)


def _copy_cast(src_ref, dst_ref, rows, dst_off=0, step=256):
    n = rows // step

    def body(i, c):
        r = pl.multiple_of(i * step, step)
        d = pl.multiple_of(dst_off + i * step, 16)
        dst_ref[pl.ds(d, step), :] = src_ref[pl.ds(r, step), :].astype(dst_ref.dtype)
        return c

    lax.fori_loop(0, n, body, 0)
    if rows % step:
        dst_ref[dst_off + n * step:dst_off + rows, :] = src_ref[n * step:rows, :].astype(dst_ref.dtype)


def _split3(x):
    hi = x.astype(BF16)
    r1 = x - hi.astype(F32)
    mid = r1.astype(BF16)
    lo = (r1 - mid.astype(F32)).astype(BF16)
    return hi, mid, lo


def _ada_kernel(c_ref, w_ref, b_ref, o_ref):
    c = c_ref[...]
    a = (c * _sigmoid(c)).astype(BF16)
    o_ref[...] = _dot(a, w_ref[...].astype(BF16)) + b_ref[...]


def _ada(c_all, w_ada, b_ada):
    depth, _, n = w_ada.shape
    rows = c_all.shape[0]
    tn = 1024
    return pl.pallas_call(
        _ada_kernel,
        grid=(depth, n // tn),
        in_specs=[pl.BlockSpec((rows, D_MODEL), lambda l, j: (0, 0)),
                  pl.BlockSpec((None, D_MODEL, tn), lambda l, j: (l, 0, j)),
                  pl.BlockSpec((None, 1, tn), lambda l, j: (l, 0, j))],
        out_specs=pl.BlockSpec((None, rows, tn), lambda l, j: (l, 0, j)),
        out_shape=jax.ShapeDtypeStruct((depth, rows, n), F32),
        name="ada_mod",
        compiler_params=_cparams(("arbitrary", "arbitrary"), 40),
    )(c_all, w_ada, b_ada.reshape(depth, 1, n))


def _relbias_kernel(tab_ref, o_ref, *, cq, band):
    l = pl.program_id(0)
    h = pl.program_id(1)
    qi = lax.broadcasted_iota(jnp.int32, (cq, band), 0)
    s = lax.broadcasted_iota(jnp.int32, (cq, band), 1)
    idx = jnp.clip(qi + (band - cq) - s, -REL_CLIP, REL_CLIP) + REL_CLIP
    base = l * (N_REL * H_A) + h

    def body(r, acc):
        return jnp.where(idx == r, tab_ref[base + r * H_A], acc)

    o_ref[...] = lax.fori_loop(0, N_REL, body, jnp.zeros((cq, band), F32)) * LOG2E


def _relbias(rel_bias, cq, band):
    depth = rel_bias.shape[0]
    return pl.pallas_call(
        functools.partial(_relbias_kernel, cq=cq, band=band),
        grid=(depth, H_A),
        in_specs=[pl.BlockSpec(memory_space=pltpu.SMEM)],
        out_specs=pl.BlockSpec((None, None, cq, band), lambda l, h: (l, h, 0, 0)),
        out_shape=jax.ShapeDtypeStruct((depth, H_A, cq, band), F32),
        name="rel_bias",
    )(rel_bias.reshape(-1))


def _inproj_kernel(*refs, n_prev, head_major):
    x_ref, sc_ref, sh_ref, g_ref, wq_ref, wk_ref, wv_ref, wf_ref, bf_ref = refs[:9]
    outs = refs[9 + n_prev:-1]
    h_scr = refs[-1]
    q_ref, ka_ref, va_ref, kb_ref, vb_ref, kc_ref, vc_ref, lf_ref = outs[:8]
    j = pl.program_id(1)

    @pl.when(j == 0)
    def _():
        _norm_rows(x_ref, g_ref, sc_ref, sh_ref, h_scr)
        lf_ref[...] = _log_sigmoid(_dot(h_scr[...], wf_ref[...]) + bf_ref[...])

    h = h_scr[...]
    q_scale = jnp.where(j >= PAIRS_AB, -Q_SCALE, Q_SCALE)
    q_ref[...] = (_dot(h, wq_ref[...]) * q_scale).astype(BF16)
    kt = _dot(h, wk_ref[...])
    vt = _dot(h, wv_ref[...])

    @pl.when(j < PAIRS_A)
    def _():
        ka_ref[...] = kt
        va_ref[...] = vt

    def put(ref, tile):
        if head_major:
            for c in range(2):
                ref[c] = tile[:, c * HEAD_DIM:(c + 1) * HEAD_DIM]
        else:
            ref[...] = tile

    @pl.when((j >= PAIRS_A) & (j < PAIRS_AB))
    def _():
        put(kb_ref, kt)
        put(vb_ref, vt)

    @pl.when(j >= PAIRS_AB)
    def _():
        put(kc_ref, kt)
        put(vc_ref, vt)


def _weight_spec(w, layer, tn, col):
    return pl.BlockSpec((None, w.shape[1], tn), lambda i, j: (layer, 0, col(j)))


def _mod_spec(mod, tm, tn, tiles_per_batch):
    r = mod.shape[1]
    col = (lambda j: 0) if tn == mod.shape[2] else (lambda j: j)
    if r == 1:
        return pl.BlockSpec((None, 1, tn), lambda i, j: (i // tiles_per_batch, 0, col(j)))
    return pl.BlockSpec((None, tm, tn), lambda i, j: (0, i, col(j)))


def _inproj(x2, sc, sh, g, w_in, wl, w_f, b_f, tm, tiles_per_batch, state=None):
    m = x2.shape[0]
    tn = HEAD_PAIR

    def clamp(lo, n):
        return lambda i, j: (i, jnp.clip(j - lo, 0, n - 1))

    def full_mod(a):
        return _mod_spec(a, tm, D_MODEL, tiles_per_batch)

    groups = ((0, PAIRS_A), (PAIRS_A, PAIRS_AB - PAIRS_A), (PAIRS_AB, N_PAIRS - PAIRS_AB))
    out_shape = [jax.ShapeDtypeStruct((m, D_MODEL), BF16)]
    out_specs = [pl.BlockSpec((tm, tn), lambda i, j: (i, j))]
    for lo, n in groups:
        for _ in range(2):
            if state is None or lo == 0:
                out_shape.append(jax.ShapeDtypeStruct((m, n * tn), F32))
                out_specs.append(pl.BlockSpec((tm, tn), clamp(lo, n)))
            else:
                layer, depth, batch, _ = state
                out_shape.append(jax.ShapeDtypeStruct((depth, batch, 2 * n, m // batch, HEAD_DIM), F32))
                out_specs.append(pl.BlockSpec(
                    (None, None, 2, tm, HEAD_DIM),
                    lambda i, j, lo=lo, n=n: (layer, i // tiles_per_batch, jnp.clip(j - lo, 0, n - 1),
                                              i % tiles_per_batch, 0)))
    out_shape.append(jax.ShapeDtypeStruct((m, LANES), F32))
    out_specs.append(pl.BlockSpec((tm, LANES), lambda i, j: (i, 0)))
    in_specs = [pl.BlockSpec((tm, D_MODEL), lambda i, j: (i, 0)),
                full_mod(sc), full_mod(sh),
                pl.BlockSpec((1, D_MODEL), lambda i, j: (0, 0)),
                _weight_spec(w_in, wl, tn, lambda j: j),
                _weight_spec(w_in, wl, tn, lambda j: N_PAIRS + j),
                _weight_spec(w_in, wl, tn, lambda j: 2 * N_PAIRS + j),
                pl.BlockSpec((D_MODEL, LANES), lambda i, j: (0, 0)),
                pl.BlockSpec((1, LANES), lambda i, j: (0, 0))]
    args = [x2, sc, sh, g, w_in, w_in, w_in, w_f, b_f]
    aliases = {}
    prev = None if state is None else state[3]
    if prev is not None:
        first_state_out = 3
        for idx, arr in enumerate(prev):
            in_specs.append(pl.BlockSpec(memory_space=pl.ANY))
            aliases[len(args)] = first_state_out + idx
            args.append(arr)
    q, ka, va, kb, vb, kc, vc, lf = pl.pallas_call(
        functools.partial(_inproj_kernel, n_prev=0 if prev is None else len(prev), head_major=state is not None),
        grid=(m // tm, N_PAIRS),
        in_specs=in_specs,
        out_specs=out_specs,
        out_shape=out_shape,
        input_output_aliases=aliases,
        scratch_shapes=[pltpu.VMEM((tm, D_MODEL), BF16)],
        name="in_proj",
        compiler_params=_cparams(("arbitrary", "arbitrary"), 52),
    )(*args)
    return q, ka, kb, kc, va, vb, vc, lf


def _dot_f32_by_01(x, ones01):
    hi, mid, lo = _split3(x)
    return _dot(hi, ones01) + _dot(mid, ones01) + _dot(lo, ones01)


def _cumsum_kernel(x_ref, o_ref):
    x = x_ref[...]
    rows = x.shape[0]
    i0 = lax.broadcasted_iota(jnp.int32, (LANES, LANES), 0)
    i1 = lax.broadcasted_iota(jnp.int32, (LANES, LANES), 1)
    upper = (i0 <= i1).astype(BF16)
    c = _dot_f32_by_01(x, upper)
    tot = jnp.broadcast_to(c[:, LANES - 1:LANES], (rows, LANES))
    r0 = lax.broadcasted_iota(jnp.int32, (rows, rows), 0)
    r1 = lax.broadcasted_iota(jnp.int32, (rows, rows), 1)
    lower = (r1 < r0).astype(BF16)
    hi, mid, lo = _split3(tot)
    off = _dot(lower, hi) + _dot(lower, mid) + _dot(lower, lo)
    o_ref[...] = (c + off) * LOG2E


def _cumsum_time(logf_hm):
    b, h, t = logf_hm.shape
    rows = t // LANES
    out = pl.pallas_call(
        _cumsum_kernel,
        grid=(b * h,),
        in_specs=[pl.BlockSpec((None, rows, LANES), lambda i: (i, 0, 0))],
        out_specs=pl.BlockSpec((None, rows, LANES), lambda i: (i, 0, 0)),
        out_shape=jax.ShapeDtypeStruct((b * h, rows, LANES), F32),
        name="forget_cumsum",
    )(logf_hm.reshape(b * h, rows, LANES))
    return out.reshape(b, h, t)


def _band_kernel(q_ref, k_ref, v_ref, bias_ref, o_ref, kp_scr, vp_scr, *, cq, npad, nmask, nchunks):
    tk = k_ref.shape[0]
    kp_scr[0:npad, :] = jnp.zeros((npad, HEAD_DIM), BF16)
    vp_scr[0:npad, :] = jnp.zeros((npad, HEAD_DIM), BF16)
    _copy_cast(k_ref, kp_scr, tk, dst_off=npad)
    _copy_cast(v_ref, vp_scr, tk, dst_off=npad)
    bias = bias_ref[...]
    slot = lax.broadcasted_iota(jnp.int32, (cq, A_BAND), 1)
    per_step = min(nchunks, 8)

    def step(i, carry):
        rows = [pl.multiple_of((i * per_step + u) * cq, cq) for u in range(per_step)]
        scores = [_dot_nt(q_ref[pl.ds(r0, cq), :], kp_scr[pl.ds(r0, A_BAND), :]) for r0 in rows]
        probs, denoms = [], []
        for u in range(per_step):
            first_valid = jnp.maximum(nmask, npad - (i * per_step + u) * cq)
            s = jnp.where(slot >= first_valid, scores[u] + bias, -jnp.inf)
            p = jnp.exp2(s - jnp.max(s, axis=-1, keepdims=True))
            denoms.append(jnp.sum(p, axis=-1, keepdims=True))
            probs.append(p.astype(BF16))
        for u in range(per_step):
            o = _dot(probs[u], vp_scr[pl.ds(rows[u], A_BAND), :]) / denoms[u]
            o_ref[pl.ds(rows[u], cq), :] = o.astype(o_ref.dtype)
        return carry

    lax.fori_loop(0, nchunks // per_step, step, 0)


def _band_attn(q, k, v, bias, cq, npad, nmask):
    b, t, _ = q.shape
    tk = k.shape[1]
    kern = functools.partial(_band_kernel, cq=cq, npad=npad, nmask=nmask, nchunks=t // cq)
    return pl.pallas_call(
        kern,
        grid=(b, H_A),
        in_specs=[pl.BlockSpec((None, t, HEAD_DIM), lambda i, h: (i, 0, h)),
                  pl.BlockSpec((None, tk, HEAD_DIM), lambda i, h: (i, 0, h)),
                  pl.BlockSpec((None, tk, HEAD_DIM), lambda i, h: (i, 0, h)),
                  pl.BlockSpec((None, cq, A_BAND), lambda i, h: (h, 0, 0))],
        out_specs=pl.BlockSpec((None, t, HEAD_DIM), lambda i, h: (i, 0, h)),
        out_shape=jax.ShapeDtypeStruct((b, t, H_A * HEAD_DIM), BF16),
        scratch_shapes=[pltpu.VMEM((npad + tk, HEAD_DIM), BF16),
                        pltpu.VMEM((npad + tk, HEAD_DIM), BF16)],
        name="band_attn",
        compiler_params=_cparams(("arbitrary", "arbitrary"), 48),
    )(q, k, v, bias)


def _pad_lanes_left(x, lanes, fill):
    if lanes == 0:
        return x
    return jnp.concatenate([jnp.full((x.shape[0], lanes), fill, x.dtype), x], axis=1)


def _col_replicate(row):
    n = row.shape[1]
    parts = [jnp.broadcast_to(row[:, c * LANES:(c + 1) * LANES], (LANES, LANES)).T for c in range(n // LANES)]
    return parts[0] if len(parts) == 1 else jnp.concatenate(parts, axis=0)


def _aug_columns(col_rep, first):
    hi, mid, lo = _split3(col_rep)
    lane = lax.broadcasted_iota(jnp.int32, col_rep.shape, 1)
    base = 0 if first else 3
    one = jnp.where((lane >= 3 - base) & (lane < 6 - base), 1.0, 0.0)
    x = jnp.where(lane == base, hi.astype(F32),
                  jnp.where(lane == base + 1, mid.astype(F32),
                            jnp.where(lane == base + 2, lo.astype(F32), one)))
    return x.astype(BF16)


def _fox_kernel(q_ref, k_ref, v_ref, g_ref, o_ref, kp_scr, vt_scr, qp_scr, acc_scr, x_scr, row_scr,
                *, tq, tk, nq, qoff, ng):
    nblk = kp_scr.shape[0]
    ones_rows = (lax.broadcasted_iota(jnp.int32, (VT_EXTRA, tk), 0) == 0).astype(BF16)

    per_step = 4 if nblk % 4 == 0 else 1

    def prep(step, c):
        blocks = [step * per_step + u for u in range(per_step)]
        rows = [pl.multiple_of(kb * tk, tk) for kb in blocks]
        vts = [v_ref[pl.ds(r0, tk), :].astype(F32).T.astype(BF16) for r0 in rows]
        augs = [_aug_columns(_col_replicate(-g_ref[kb]), True) for kb in blocks]
        for kb, r0, vt, aug in zip(blocks, rows, vts, augs):
            kp_scr[kb, :, 0:HEAD_DIM] = k_ref[pl.ds(r0, tk), :].astype(BF16)
            kp_scr[kb, :, HEAD_DIM:2 * HEAD_DIM] = aug
            vt_scr[kb, 0:HEAD_DIM, :] = vt
            vt_scr[kb, HEAD_DIM:HEAD_DIM + VT_EXTRA, :] = ones_rows
        return c

    lax.fori_loop(0, nblk // per_step, prep, 0)
    off = qoff % tk

    def first_lane(u, diag):
        return u * tk if diag and tq == ng * tk else 0

    def scores(g, slot, p0):
        m_blk = None
        for u in range(ng):
            l0 = first_lane(u, p0 is not None)
            x = _dot_nt(kp_scr[g * ng + u], qp_scr[l0:tq, :])
            if p0 is not None:
                row_u = lax.broadcasted_iota(jnp.int32, (tk, tq - l0), 0)
                col_u = lax.broadcasted_iota(jnp.int32, (tk, tq - l0), 1) + l0
                x = jnp.where(row_u + ((g * ng + u) * tk - p0) <= col_u, x, -jnp.inf)
            x_scr[slot, u, :, l0:tq] = x
            m_u = jnp.max(x, axis=0, keepdims=True)
            if p0 is None:
                m_blk = m_u if m_blk is None else jnp.maximum(m_blk, m_u)
            elif u == 0:
                row_scr[...] = jnp.broadcast_to(m_u, (SUBLANES, tq))
            else:
                row_scr[:, l0:tq] = jnp.maximum(row_scr[:, l0:tq], jnp.broadcast_to(m_u, (SUBLANES, tq - l0)))
        return m_blk if p0 is None else row_scr[0:1, :]

    def process(g, slot, m_cur, m_next, diag=False):
        acc = acc_scr[...]
        if diag:
            row_scr[...] = jnp.broadcast_to(m_cur, (SUBLANES, tq))
        for u in range(ng):
            l0 = first_lane(u, diag)
            m_u = m_cur if l0 == 0 else row_scr[0:1, l0:tq]
            p = jnp.exp2(x_scr[slot, u, :, l0:tq] - m_u)
            acc = acc + _pad_lanes_left(_dot(vt_scr[g * ng + u], p.astype(BF16)), l0, 0.0)
        acc_scr[...] = acc * jnp.exp2(m_cur - m_next)

    def qblock(i, carry):
        q0 = pl.multiple_of(i * tq, tq)
        p0 = qoff + i * tq
        pb = p0 // tk
        qp_scr[:, 0:HEAD_DIM] = q_ref[pl.ds(q0, tq), :]
        if tq <= tk:
            gq = g_ref[pb][:, off:off + tq]
        else:
            gq = jnp.concatenate([g_ref[pb + c] for c in range(tq // tk)], axis=1)
        qp_scr[:, HEAD_DIM:2 * HEAD_DIM] = _aug_columns(_col_replicate(gq), False)
        acc_scr[...] = jnp.zeros((HEAD_DIM + VT_EXTRA, tq), F32)
        gd = pb // ng
        m0 = scores(gd, 0, p0)
        m_pre = scores(jnp.maximum(gd - 1, 0), 1, None)
        m1 = jnp.where(gd >= 1, jnp.maximum(m0, m_pre), m0)
        process(gd, 0, m0, m1, diag=True)

        def pair(pi, m_run):
            g1 = gd - 1 - 2 * pi
            m_a = jnp.maximum(m_run, scores(g1 - 1, 0, None))
            process(g1, 1, m_run, m_a)
            m_pre = scores(jnp.maximum(g1 - 2, 0), 1, None)
            m_b = jnp.where(g1 >= 2, jnp.maximum(m_a, m_pre), m_a)
            process(g1 - 1, 0, m_a, m_b)
            return m_b

        m_last = lax.fori_loop(0, gd // 2, pair, m1)

        @pl.when(gd % 2 == 1)
        def _():
            process(0, 1, m_last, m_last)

        denom = acc_scr[HEAD_DIM:HEAD_DIM + 1, :]
        o_ref[pl.ds(q0, tq), :] = (acc_scr[0:HEAD_DIM, :] / denom).T.astype(o_ref.dtype)
        return carry

    lax.fori_loop(0, nq, qblock, 0)


def _stick_kernel(q_ref, k_ref, v_ref, o_ref, kp_scr, vt_scr, acc_scr, x_scr, row_scr, *, tq, tk, nq, qoff, ng):
    nblk = kp_scr.shape[0]
    nv = tk // SUBLANES

    r = lax.broadcasted_iota(jnp.int32, (tk, tk), 0)
    kk = lax.broadcasted_iota(jnp.int32, (tk, tk), 1)
    perm = ((r % SUBLANES) * nv + r // SUBLANES == kk).astype(BF16)
    perm_t = ((kk % SUBLANES) * nv + kk // SUBLANES == r).astype(BF16)

    per_step = 4 if nblk % 4 == 0 else 1

    def prep(step, c):
        blocks = [step * per_step + u for u in range(per_step)]
        rows = [pl.multiple_of(kb * tk, tk) for kb in blocks]
        ks = [_dot(perm, k_ref[pl.ds(r0, tk), :].astype(BF16)) for r0 in rows]
        vts = [_dot(v_ref[pl.ds(r0, tk), :].astype(F32).T.astype(BF16), perm_t) for r0 in rows]
        for kb, k_perm, vt in zip(blocks, ks, vts):
            kp_scr[kb] = k_perm.astype(BF16)
            vt_scr[kb] = vt.astype(BF16)
        return c

    lax.fori_loop(0, nblk // per_step, prep, 0)

    def scores(q, g, slot):
        for u in range(ng):
            x_scr[slot, u] = _dot_nt(kp_scr[g * ng + u], q)

    def tile_terms(x, kb, p0, l0):
        beta = 1.0 / (1.0 + jnp.exp2(x))
        stay = 1.0 - beta
        valid = None
        if p0 is not None:
            row_u = lax.broadcasted_iota(jnp.int32, (tk, tq - l0), 0)
            col_u = lax.broadcasted_iota(jnp.int32, (tk, tq - l0), 1) + l0
            valid = (row_u % SUBLANES) * nv + row_u // SUBLANES + (kb * tk - p0) < col_u
            stay = jnp.where(valid, stay, 1.0)
        after = [None] * nv
        run = jnp.ones((SUBLANES, tq - l0), F32)
        for v in range(nv - 1, -1, -1):
            after[v] = run
            run = run * stay[v * SUBLANES:(v + 1) * SUBLANES, :]
        incl = run
        sub_l = lax.broadcasted_iota(jnp.int32, (SUBLANES, tq - l0), 0)
        for k in (1, 2, 4):
            incl = incl * jnp.where(sub_l + k < SUBLANES, pltpu.roll(incl, SUBLANES - k, axis=0), 1.0)
        later = jnp.where(sub_l + 1 < SUBLANES, pltpu.roll(incl, SUBLANES - 1, axis=0), 1.0)
        return after, later, incl[0:1, :], beta, valid

    def process(g, slot, carry, p0):
        acc = acc_scr[...]
        skip = p0 is not None and tq == ng * tk
        if skip:
            row_scr[...] = jnp.broadcast_to(carry, (SUBLANES, tq))
        for u in range(ng - 1, -1, -1):
            l0 = u * tk if skip else 0
            after, later, total, beta, valid = tile_terms(x_scr[slot, u, :, l0:tq], g * ng + u, p0, l0)
            scale_u = later * (row_scr[:, l0:tq] if skip else carry)
            w = jnp.concatenate([after[v] * scale_u for v in range(nv)], axis=0) * beta
            if valid is not None:
                w = jnp.where(valid, w, 0.0)
            acc = acc + _pad_lanes_left(_dot(vt_scr[g * ng + u], w.astype(BF16)), l0, 0.0)
            if skip:
                row_scr[:, l0:tq] = row_scr[:, l0:tq] * total
            else:
                carry = carry * total
        acc_scr[...] = acc
        return row_scr[0:1, :] if skip else carry

    def qblock(i, c0):
        q0 = pl.multiple_of(i * tq, tq)
        p0 = qoff + i * tq
        gd = (p0 // tk) // ng
        q = q_ref[pl.ds(q0, tq), :]
        acc_scr[...] = jnp.zeros((HEAD_DIM, tq), F32)
        scores(q, gd, 0)
        scores(q, jnp.maximum(gd - 1, 0), 1)
        carry = process(gd, 0, jnp.ones((1, tq), F32), p0)

        def pair(pi, carry):
            g1 = gd - 1 - 2 * pi
            scores(q, g1 - 1, 0)
            carry = process(g1, 1, carry, None)
            scores(q, jnp.maximum(g1 - 2, 0), 1)
            return process(g1 - 1, 0, carry, None)

        carry = lax.fori_loop(0, gd // 2, pair, carry)

        @pl.when(gd % 2 == 1)
        def _():
            process(0, 1, carry, None)

        o_ref[pl.ds(q0, tq), :] = acc_scr[...].T.astype(o_ref.dtype)
        return c0

    lax.fori_loop(0, nq, qblock, 0)


def _causal_attn(kind, q, k, v, f, head0, tq, tk, qoff, ng, layer=None):
    b, t, _ = q.shape
    if layer is None:
        tkv = k.shape[1]
        nh = k.shape[2] // HEAD_DIM
        kv_spec = pl.BlockSpec((None, tkv, HEAD_DIM), lambda i, h: (i, 0, h))
    else:
        nh, tkv = k.shape[2], k.shape[3]
        kv_spec = pl.BlockSpec((None, None, None, tkv, HEAD_DIM), lambda i, h: (layer, i, h, 0, 0))
    nblk = tkv // tk
    assert tkv % tk == 0 and nblk % ng == 0 and tq % LANES == 0 and t % tq == 0
    assert (ng * tk) % tq == 0 and qoff % tq == 0 and (tq % tk == 0 or (qoff % tk) + tq <= tk and t == tq)
    common = dict(tq=tq, tk=tk, nq=t // tq, qoff=qoff, ng=ng)
    in_specs = [pl.BlockSpec((None, t, HEAD_DIM), lambda i, h: (i, 0, head0 + h)), kv_spec, kv_spec]
    if kind == "fox":
        kern = functools.partial(_fox_kernel, **common)
        g = f.reshape(b, nh, nblk, 1, tk)
        in_specs.append(pl.BlockSpec((None, None, nblk, 1, tk), lambda i, h: (i, h, 0, 0, 0)))
        scratch = [pltpu.VMEM((nblk, tk, 2 * HEAD_DIM), BF16), pltpu.VMEM((nblk, HEAD_DIM + VT_EXTRA, tk), BF16),
                   pltpu.VMEM((tq, 2 * HEAD_DIM), BF16), pltpu.VMEM((HEAD_DIM + VT_EXTRA, tq), F32),
                   pltpu.VMEM((2, ng, tk, tq), F32), pltpu.VMEM((SUBLANES, tq), F32)]
        args = (q, k, v, g)
    else:
        kern = functools.partial(_stick_kernel, **common)
        scratch = [pltpu.VMEM((nblk, tk, HEAD_DIM), BF16), pltpu.VMEM((nblk, HEAD_DIM, tk), BF16),
                   pltpu.VMEM((HEAD_DIM, tq), F32), pltpu.VMEM((2, ng, tk, tq), F32),
                   pltpu.VMEM((SUBLANES, tq), F32)]
        args = (q, k, v)
    return pl.pallas_call(
        kern,
        grid=(b, nh),
        in_specs=in_specs,
        out_specs=pl.BlockSpec((None, t, HEAD_DIM), lambda i, h: (i, 0, h)),
        out_shape=jax.ShapeDtypeStruct((b, t, nh * HEAD_DIM), BF16),
        scratch_shapes=scratch,
        name=kind + "_attn",
        compiler_params=_cparams(("arbitrary", "arbitrary"), 52),
    )(*args)


def _outproj_kernel(oa_ref, ob_ref, oc_ref, wa_ref, wb_ref, wc_ref, x_ref, gt_ref, y_ref):
    acc = _dot(oa_ref[...], wa_ref[...]) + _dot(ob_ref[...], wb_ref[...]) + _dot(oc_ref[...], wc_ref[...])
    y_ref[...] = x_ref[...] + gt_ref[...] * acc


def _outproj(oa, ob, oc, wa, wb, wc, x2, gt, tm, tiles_per_batch):
    m = x2.shape[0]
    tn = 1024
    row = lambda a: pl.BlockSpec((tm, a.shape[1]), lambda i, j: (i, 0))
    wsp = lambda a: pl.BlockSpec((a.shape[0], tn), lambda i, j: (0, j))
    return pl.pallas_call(
        _outproj_kernel,
        grid=(m // tm, D_MODEL // tn),
        in_specs=[row(oa), row(ob), row(oc), wsp(wa), wsp(wb), wsp(wc),
                  pl.BlockSpec((tm, tn), lambda i, j: (i, j)),
                  _mod_spec(gt, tm, tn, tiles_per_batch)],
        out_specs=pl.BlockSpec((tm, tn), lambda i, j: (i, j)),
        out_shape=jax.ShapeDtypeStruct((m, D_MODEL), F32),
        name="out_proj",
        compiler_params=_cparams(("arbitrary", "arbitrary"), 40),
    )(oa, ob, oc, wa, wb, wc, x2, gt)


def _ffn1_kernel(x_ref, sc_ref, sh_ref, g_ref, wg_ref, wu_ref, a_ref, h_scr):
    @pl.when(pl.program_id(1) == 0)
    def _():
        _norm_rows(x_ref, g_ref, sc_ref, sh_ref, h_scr)

    h = h_scr[...]
    gate = _dot(h, wg_ref[...])
    up = _dot(h, wu_ref[...])
    a_ref[...] = (gate * _sigmoid(gate) * up).astype(BF16)


def _ffn1(x2, sc, sh, g, wg, wu, wl, tm, tiles_per_batch):
    m = x2.shape[0]
    dff = wg.shape[2]
    tn = 512
    full_mod = lambda a: _mod_spec(a, tm, D_MODEL, tiles_per_batch)
    return pl.pallas_call(
        _ffn1_kernel,
        grid=(m // tm, dff // tn),
        in_specs=[pl.BlockSpec((tm, D_MODEL), lambda i, j: (i, 0)),
                  full_mod(sc), full_mod(sh),
                  pl.BlockSpec((1, D_MODEL), lambda i, j: (0, 0)),
                  _weight_spec(wg, wl, tn, lambda j: j),
                  _weight_spec(wu, wl, tn, lambda j: j)],
        out_specs=pl.BlockSpec((tm, tn), lambda i, j: (i, j)),
        out_shape=jax.ShapeDtypeStruct((m, dff), BF16),
        scratch_shapes=[pltpu.VMEM((tm, D_MODEL), BF16)],
        name="ffn_gate_up",
        compiler_params=_cparams(("arbitrary", "arbitrary"), 44),
    )(x2, sc, sh, g, wg, wu)


def _ffn2_kernel(a_ref, wd_ref, x_ref, gt_ref, y_ref):
    y_ref[...] = x_ref[...] + gt_ref[...] * _dot(a_ref[...], wd_ref[...])


def _ffn2(a, wd, wl, x2, gt, tm, tiles_per_batch):
    m, dff = a.shape
    tn = 512
    return pl.pallas_call(
        _ffn2_kernel,
        grid=(m // tm, D_MODEL // tn),
        in_specs=[pl.BlockSpec((tm, dff), lambda i, j: (i, 0)),
                  _weight_spec(wd, wl, tn, lambda j: j),
                  pl.BlockSpec((tm, tn), lambda i, j: (i, j)),
                  _mod_spec(gt, tm, tn, tiles_per_batch)],
        out_specs=pl.BlockSpec((tm, tn), lambda i, j: (i, j)),
        out_shape=jax.ShapeDtypeStruct((m, D_MODEL), F32),
        name="ffn_down",
        compiler_params=_cparams(("arbitrary", "arbitrary"), 48),
    )(a, wd, x2, gt)


def _final_norm_kernel(x_ref, g_ref, y_ref, *, rows_chunk=64):
    g = g_ref[...]

    def body(r, c):
        r0 = pl.multiple_of(r * rows_chunk, rows_chunk)
        x = x_ref[pl.ds(r0, rows_chunk), :]
        ms = jnp.mean(x * x, axis=-1, keepdims=True)
        y_ref[pl.ds(r0, rows_chunk), :] = x * lax.rsqrt(ms + EPS) * g
        return c

    lax.fori_loop(0, x_ref.shape[0] // rows_chunk, body, 0)


def _final_norm(x2, g, tm):
    m = x2.shape[0]
    return pl.pallas_call(
        _final_norm_kernel,
        grid=(m // tm,),
        in_specs=[pl.BlockSpec((tm, D_MODEL), lambda i: (i, 0)),
                  pl.BlockSpec((1, D_MODEL), lambda i: (0, 0))],
        out_specs=pl.BlockSpec((tm, D_MODEL), lambda i: (i, 0)),
        out_shape=jax.ShapeDtypeStruct((m, D_MODEL), F32),
        name="final_norm",
        compiler_params=_cparams(("arbitrary",), 40),
    )(x2, g.reshape(1, D_MODEL))


def _pad_time(a, total):
    pad = total - a.shape[1]
    if pad == 0:
        return a
    return jnp.pad(a, ((0, 0), (0, pad)) + ((0, 0),) * (a.ndim - 2))


def _mix_and_ffn(x2, oa, ob, oc, mods, wts, tm, tiles_per_batch):
    _, _, gt1, sh2, sc2, gt2 = mods
    m = x2.shape[0]
    r2 = lambda a: a.reshape(m, a.shape[-1])
    x2 = _outproj(r2(oa), r2(ob), r2(oc), wts["wo_a"], wts["wo_b"], wts["wo_c"], x2, gt1, tm, tiles_per_batch)
    a = _ffn1(x2, sc2, sh2, wts["g_ffn"], wts["w_gate"], wts["w_up"], wts["layer"], tm, tiles_per_batch)
    return _ffn2(a, wts["w_down"], wts["layer"], x2, gt2, tm, tiles_per_batch)


def _prompt_layer(x2, batch, mods, wts, bias, tm, layer, depth, prev_state):
    m = x2.shape[0]
    t = m // batch
    tiles_per_batch = t // tm
    sh1, sc1 = mods[0], mods[1]
    q, ka, kb, kc, va, vb, vc, lf = _inproj(
        x2, sc1, sh1, wts["g_attn"], wts["w_in"], wts["layer"], wts["w_f"], wts["b_f"], tm, tiles_per_batch,
        state=(layer, depth, batch, prev_state))
    r3 = lambda a: a.reshape(batch, t, a.shape[-1])
    q, ka, va = map(r3, (q, ka, va))
    logf = r3(lf)[:, :, :H_B]
    keep = min(A_LEFT, t)
    small_state = (ka[:, t - keep:], va[:, t - keep:], logf)
    f = _cumsum_time(jnp.swapaxes(logf, 1, 2))
    oa = _band_attn(q, ka, va, bias, CHUNK, A_BAND - CHUNK, A_BAND - CHUNK - A_LEFT)
    tq, tk, ng = 1024, 256, 4
    ob = _causal_attn("fox", q, kb, vb, f, H_A, tq, tk, 0, ng, layer=layer)
    oc = _causal_attn("stick", q, kc, vc, None, H_A + H_B, tq, tk, 0, ng, layer=layer)
    return _mix_and_ffn(x2, oa, ob, oc, mods, wts, tm, tiles_per_batch), small_state, (kb, vb, kc, vc)


def _sample_layer(x2, batch, mods, wts, bias, caches, tm):
    m = x2.shape[0]
    t = m // batch
    sh1, sc1 = mods[0], mods[1]
    q, ka, kb, kc, va, vb, vc, lf = _inproj(x2, sc1, sh1, wts["g_attn"], wts["w_in"], wts["layer"], wts["w_f"],
                                            wts["b_f"], tm, 1)
    r3 = lambda a: a.reshape(batch, t, a.shape[-1])
    q, ka, kb, kc, va, vb, vc = map(r3, (q, ka, kb, kc, va, vb, vc))
    logf = r3(lf)[:, :, :H_B]
    cak, cav, cbk, cbv, cblogf, cck, ccv = caches
    state = (ka, va, kb, vb, logf, kc, vc)
    flat = lambda c: c.reshape(c.shape[0], c.shape[1], -1)
    past = cbk.shape[1]
    tk = 256
    tkv = past + tk

    def cat(cache, new):
        heads = cache.shape[2]
        new_hm = jnp.swapaxes(new.reshape(batch, t, heads, HEAD_DIM), 1, 2)
        both = jnp.concatenate([jnp.swapaxes(cache, 1, 2), new_hm], axis=2)
        return jnp.pad(both, ((0, 0), (0, 0), (0, tkv - both.shape[2]), (0, 0)))[None]

    kka = jnp.concatenate([flat(cak), ka], axis=1)
    vva = jnp.concatenate([flat(cav), va], axis=1)
    wa = kka.shape[1]
    oa = _band_attn(q, kka, vva, bias, t, A_BAND - wa, A_BAND - wa)
    lcat = jnp.concatenate([cblogf.astype(F32), logf], axis=1)
    fpad = -(-tkv // (SUBLANES * LANES)) * (SUBLANES * LANES)
    f = _cumsum_time(jnp.swapaxes(_pad_time(lcat, fpad), 1, 2))[:, :, :tkv]
    qpad = _pad_time(q, LANES)
    ng = tkv // tk
    ob = _causal_attn("fox", qpad, cat(cbk, kb), cat(cbv, vb), f, H_A, LANES, tk, past, ng, layer=0)[:, :t]
    oc = _causal_attn("stick", qpad, cat(cck, kc), cat(ccv, vc), None, H_A + H_B, LANES, tk, past, ng,
                      layer=0)[:, :t]
    return _mix_and_ffn(x2, oa, ob, oc, mods, wts, tm, 1), state


def kernel(x_prompt, x_sample, cache_a_k, cache_a_v, cache_b_k, cache_b_v, cache_b_logf, cache_c_k, cache_c_v,
           c_prompt, c_sample, w_ada, b_ada, g_attn, g_ffn, w_in, b_f, rel_bias, w_o, w_gate, w_up, w_down, g_final):
    depth = w_ada.shape[0]
    bp, tp, _ = x_prompt.shape
    bs, ts, _ = x_sample.shape
    n_qkv = 3 * D_MODEL

    rows = -(-(bp + bs) // SUBLANES) * SUBLANES
    c_all = jnp.concatenate([c_prompt, c_sample, jnp.zeros((rows - bp - bs, D_MODEL), F32)], axis=0)
    mod = _ada(c_all, w_ada, b_ada)

    bias_p = _relbias(rel_bias, CHUNK, A_BAND)
    bias_s = _relbias(rel_bias, ts, A_BAND)

    xp = x_prompt.reshape(bp * tp, D_MODEL)
    xs = x_sample.reshape(bs * ts, D_MODEL)
    tm_p, tm_s = 1024, bs * ts
    small_p, states_s = [], []
    big_p = None
    ea = H_A * HEAD_DIM
    eb = (H_A + H_B) * HEAD_DIM
    w_in_b, w_gate_b, w_up_b, w_down_b = (w.astype(BF16) for w in (w_in, w_gate, w_up, w_down))
    for l in range(depth):
        wo = w_o[l].astype(BF16)
        wts = dict(
            layer=l, g_attn=g_attn[l].reshape(1, D_MODEL), g_ffn=g_ffn[l].reshape(1, D_MODEL),
            w_in=w_in_b,
            w_f=jnp.pad(w_in[l, :, n_qkv:], ((0, 0), (0, LANES - H_B))).astype(BF16),
            b_f=jnp.pad(b_f[l], (0, LANES - H_B)).reshape(1, LANES),
            wo_a=wo[:ea], wo_b=wo[ea:eb], wo_c=wo[eb:],
            w_gate=w_gate_b, w_up=w_up_b, w_down=w_down_b)
        chunks = jnp.split(mod[l], 6, axis=-1)
        mods_p = [c[:bp].reshape(bp, 1, D_MODEL) for c in chunks]
        mods_s = [jnp.repeat(c[bp:bp + bs], ts, axis=0).reshape(1, bs * ts, D_MODEL) for c in chunks]
        xp, st_small, big_p = _prompt_layer(xp, bp, mods_p, wts, bias_p[l], tm_p, l, depth, big_p)
        caches = (cache_a_k[l], cache_a_v[l], cache_b_k[l], cache_b_v[l], cache_b_logf[l],
                  cache_c_k[l], cache_c_v[l])
        xs, st_s = _sample_layer(xs, bs, mods_s, wts, bias_s[l], caches, tm_s)
        small_p.append(st_small)
        states_s.append(st_s)

    def stack_heads(arrs, heads):
        a = jnp.stack(arrs)
        return a if heads is None else a.reshape(a.shape[:3] + (heads, HEAD_DIM))

    a_k_p, a_v_p, b_logf_p = [stack_heads(arrs, h) for arrs, h in zip(zip(*small_p), (H_A, H_A, None))]
    b_k_p, b_v_p, c_k_p, c_v_p = [jnp.swapaxes(a, 2, 3) for a in big_p]
    sample_states = [stack_heads(arrs, h)
                     for arrs, h in zip(zip(*states_s), (H_A, H_A, H_B, H_B, None, H_C, H_C))]
    y_prompt = _final_norm(xp, g_final, 512).reshape(bp, tp, D_MODEL)
    y_sample = _final_norm(xs, g_final, bs * ts).reshape(bs, ts, D_MODEL)
    return tuple([y_prompt, y_sample, a_k_p, a_v_p, b_k_p, b_v_p, b_logf_p, c_k_p, c_v_p] + sample_states)
```
